```python
import jax, jax.numpy as jnp
from jax import lax
import numpy as np


D_MODEL = 1024
BATCH = 16
SEQ = 2048
DEPTH = 2

N_A_LAYERS = DEPTH // 2
N_B_LAYERS = DEPTH - N_A_LAYERS
N_DENSE = (DEPTH + 1) // 2
N_MOE = DEPTH // 2
NORM_EPS = 1e-6

MLSTM_HEADS = 4
MLSTM_DQK = D_MODEL // (2 * MLSTM_HEADS)
MLSTM_DV = D_MODEL // MLSTM_HEADS
MLSTM_CHUNK = 64
MLSTM_PROJ = MLSTM_HEADS * (2 * MLSTM_DQK + 2 * MLSTM_DV + 2)
MLSTM_SPLITS = (MLSTM_HEADS * MLSTM_DQK,
                2 * MLSTM_HEADS * MLSTM_DQK,
                2 * MLSTM_HEADS * MLSTM_DQK + MLSTM_HEADS * MLSTM_DV,
                2 * MLSTM_HEADS * MLSTM_DQK + 2 * MLSTM_HEADS * MLSTM_DV,
                2 * MLSTM_HEADS * MLSTM_DQK + 2 * MLSTM_HEADS * MLSTM_DV + MLSTM_HEADS)

ATT_HEADS = 8
ATT_KV_HEADS = 2
HEAD_DIM = D_MODEL // ATT_HEADS
MOBA_BLOCK = 256
MOBA_TOPK = 3
QUERY_BLOCK = 128
ROPE_THETA = 500000.0
ROT_DIM = HEAD_DIM // 4

D_FF = 2816
N_EXPERTS = 8
TOP_K_EXPERTS = 2
D_EXPERT = 3584

kernel_name = 'yoco_mlstm_moba_moe_block'


def rmsnorm(x, gain):
    xf = x.astype(jnp.float32)
    y = xf * lax.rsqrt(jnp.mean(xf * xf, axis=-1, keepdims=True) + NORM_EPS)
    return (y * gain.astype(jnp.float32)).astype(x.dtype)


def rope_partial(x, cos, sin):
    c = cos.astype(x.dtype)
    s = sin.astype(x.dtype)
    half = ROT_DIM // 2
    x1 = x[..., :half]
    x2 = x[..., half:ROT_DIM]
    return jnp.concatenate([x1 * c - x2 * s, x2 * c + x1 * s, x[..., ROT_DIM:]], axis=-1)


def swiglu(x, w_gu, w_down):
    g, u = jnp.split(x @ w_gu, 2, axis=-1)
    return (jax.nn.silu(g) * u) @ w_down


def mlstm_mixer(xn, w_in, b_if, out_gain, w_out):
    B, S, _ = xn.shape
    nc = S // MLSTM_CHUNK
    H, L = MLSTM_HEADS, MLSTM_CHUNK
    f32 = jnp.float32
    proj = xn @ w_in
    q, k, v, o, ig, fg = jnp.split(proj, MLSTM_SPLITS, axis=-1)

    def heads(t, d):
        return t.reshape(B, S, H, d).transpose(0, 2, 1, 3).reshape(B, H, nc, L, d).astype(f32)

    q = heads(q, MLSTM_DQK) * (MLSTM_DQK ** -0.5)
    k = heads(k, MLSTM_DQK)
    v = heads(v, MLSTM_DV)
    ig = (ig.astype(f32) + b_if[0].astype(f32)).transpose(0, 2, 1).reshape(B, H, nc, L)
    logf = jax.nn.log_sigmoid(fg.astype(f32) + b_if[1].astype(f32)).transpose(0, 2, 1).reshape(B, H, nc, L)

    bcum = jnp.cumsum(logf, axis=-1)
    g_tot = bcum[..., -1]
    a_end = g_tot[..., None] - bcum + ig

    def step(carry, xs):
        C, n, m = carry
        k_c, v_c, a_c, g_c = xs
        m_new = jnp.maximum(g_c + m, jnp.max(a_c, axis=-1))
        w = jnp.exp(a_c - m_new[..., None])
        decay = jnp.exp(g_c + m - m_new)
        C_new = decay[..., None, None] * C + jnp.einsum('bhlv,bhlk->bhvk', w[..., None] * v_c, k_c)
        n_new = decay[..., None] * n + jnp.einsum('bhl,bhlk->bhk', w, k_c)
        return (C_new, n_new, m_new), (C, n, m)

    init = (jnp.zeros((B, H, MLSTM_DV, MLSTM_DQK), f32),
            jnp.zeros((B, H, MLSTM_DQK), f32),
            jnp.zeros((B, H), f32))
    xs = (jnp.moveaxis(k, 2, 0), jnp.moveaxis(v, 2, 0), jnp.moveaxis(a_end, 2, 0), jnp.moveaxis(g_tot, 2, 0))
    _, (C_s, n_s, m_s) = lax.scan(step, init, xs)
    C_s = jnp.moveaxis(C_s, 0, 2)
    n_s = jnp.moveaxis(n_s, 0, 2)
    m_s = jnp.moveaxis(m_s, 0, 2)

    causal = jnp.tril(jnp.ones((L, L), dtype=bool))
    D = bcum[..., :, None] - bcum[..., None, :] + ig[..., None, :]
    D = jnp.where(causal, D, -jnp.inf)
    inter = bcum + m_s[..., None]
    m_t = jnp.maximum(inter, jnp.max(D, axis=-1))
    P = jnp.einsum('bhctk,bhcsk->bhcts', q, k) * jnp.exp(D - m_t[..., None])
    inter_w = jnp.exp(inter - m_t)
    num = jnp.einsum('bhcts,bhcsv->bhctv', P, v) + inter_w[..., None] * jnp.einsum('bhcvk,bhctk->bhctv', C_s, q)
    nq = jnp.sum(P, axis=-1) + inter_w * jnp.einsum('bhck,bhctk->bhct', n_s, q)
    den = jnp.maximum(jnp.abs(nq), jnp.exp(-m_t))
    h = num / den[..., None]

    h = h.reshape(B, H, S, MLSTM_DV).transpose(0, 2, 1, 3)
    h = h * lax.rsqrt(jnp.mean(h * h, axis=-1, keepdims=True) + NORM_EPS)
    h = h * out_gain.astype(f32).reshape(H, MLSTM_DV)
    h = h * jax.nn.sigmoid(o.astype(f32)).reshape(B, S, H, MLSTM_DV)
    return h.reshape(B, S, H * MLSTM_DV).astype(xn.dtype) @ w_out


def shared_kv(x, kv_norm, kv_w, cos, sin):
    B, S, _ = x.shape
    kv = rmsnorm(x, kv_norm) @ kv_w
    k, v = jnp.split(kv, 2, axis=-1)
    k = k.reshape(B, S, ATT_KV_HEADS, HEAD_DIM).transpose(0, 2, 1, 3)
    v = v.reshape(B, S, ATT_KV_HEADS, HEAD_DIM).transpose(0, 2, 1, 3)
    k = rope_partial(k, cos, sin)
    nb = -(-S // MOBA_BLOCK)
    pad = nb * MOBA_BLOCK - S
    k = jnp.pad(k, ((0, 0), (0, 0), (0, pad), (0, 0)))
    v = jnp.pad(v, ((0, 0), (0, 0), (0, pad), (0, 0)))
    k_blocks = k.reshape(B, ATT_KV_HEADS, nb, MOBA_BLOCK, HEAD_DIM)
    v_blocks = v.reshape(B, ATT_KV_HEADS, nb, MOBA_BLOCK, HEAD_DIM)
    k_means = jnp.mean(k_blocks.astype(jnp.float32), axis=3)
    return k_blocks, v_blocks, k_means


def moba_mixer(xn, w_q, w_o, k_blocks, v_blocks, k_means, cos, sin):
    B, S, _ = xn.shape
    H = ATT_HEADS
    nb = k_blocks.shape[2]
    nqb = S // QUERY_BLOCK
    kk = min(MOBA_TOPK, nb)
    f32 = jnp.float32
    q = (xn @ w_q).reshape(B, S, H, HEAD_DIM).transpose(0, 2, 1, 3)
    q = rope_partial(q, cos, sin)
    kv_of_head = jnp.arange(H) // (H // ATT_KV_HEADS)

    gate = jnp.einsum('bhtd,bhnd->bhtn', q.astype(f32), k_means[:, kv_of_head])
    own = jnp.arange(S) // MOBA_BLOCK
    past = jnp.arange(nb)[None, :] < own[:, None]
    gate = jnp.where(past, gate, -jnp.inf)
    _, sel = lax.top_k(gate, kk)

    def to_items(t):
        t = t.reshape(B, H, nqb, QUERY_BLOCK, *t.shape[3:])
        t = jnp.moveaxis(t, 2, 1)
        return t.reshape(B * nqb, H, QUERY_BLOCK, *t.shape[4:])

    b_ids = jnp.repeat(jnp.arange(B, dtype=jnp.int32), nqb)
    qb_ids = jnp.tile(jnp.arange(nqb, dtype=jnp.int32), B)
    scale = HEAD_DIM ** -0.5

    def attend(item):
        q_i, sel_i, b_i, qb_i = item
        kb = k_blocks[b_i]
        vb = v_blocks[b_i]
        own_blk = (qb_i * QUERY_BLOCK) // MOBA_BLOCK
        k_sel = kb[kv_of_head[:, None, None], sel_i]
        v_sel = vb[kv_of_head[:, None, None], sel_i]
        k_own = kb[kv_of_head, own_blk]
        v_own = vb[kv_of_head, own_blk]
        s_sel = jnp.einsum('htd,htjsd->htjs', q_i, k_sel).astype(f32) * scale
        slot_ok = jnp.arange(kk) < own_blk
        s_sel = jnp.where(slot_ok[:, None], s_sel, -jnp.inf)
        s_own = jnp.einsum('htd,hsd->hts', q_i, k_own).astype(f32) * scale
        t_pos = qb_i * QUERY_BLOCK + jnp.arange(QUERY_BLOCK)
        s_pos = own_blk * MOBA_BLOCK + jnp.arange(MOBA_BLOCK)
        s_own = jnp.where(s_pos[None, :] <= t_pos[:, None], s_own, -jnp.inf)
        scores = jnp.concatenate([s_sel.reshape(H, QUERY_BLOCK, kk * MOBA_BLOCK), s_own], axis=-1)
        p = jax.nn.softmax(scores, axis=-1).astype(v_sel.dtype)
        p_sel = p[..., :kk * MOBA_BLOCK].reshape(H, QUERY_BLOCK, kk, MOBA_BLOCK)
        p_own = p[..., kk * MOBA_BLOCK:]
        return jnp.einsum('htjs,htjsd->htd', p_sel, v_sel) + jnp.einsum('hts,hsd->htd', p_own, v_own)

    out = lax.map(attend, (to_items(q), to_items(sel), b_ids, qb_ids))
    out = out.reshape(B, nqb, H, QUERY_BLOCK, HEAD_DIM).transpose(0, 1, 3, 2, 4).reshape(B, S, H * HEAD_DIM)
    return out @ w_o


def moe_ffn(xn, router, w_gu, w_down):
    B, S, D = xn.shape
    xt = xn.reshape(B * S, D)
    logits = (xt @ router).astype(jnp.float32)
    top_logits, top_idx = lax.top_k(logits, TOP_K_EXPERTS)
    top_w = jax.nn.softmax(top_logits, axis=-1)
    combine = jnp.einsum('nk,nke->ne', top_w, jax.nn.one_hot(top_idx, N_EXPERTS, dtype=jnp.float32)).astype(xn.dtype)
    y = jnp.zeros_like(xt)
    for e in range(N_EXPERTS):
        y = y + combine[:, e:e + 1] * swiglu(xt, w_gu[e], w_down[e])
    return y.reshape(B, S, D)


def setup_inputs(seed: int = 0) -> dict:
    key = jax.random.key(seed)
    ks = jax.random.split(key, 20)
    f32 = jnp.float32

    def w(k, shape, fan_in):
        return jax.random.normal(k, shape, f32) * (fan_in ** -0.5)

    def g(k, shape):
        return 1.0 + 0.02 * jax.random.normal(k, shape, f32)

    x = jax.random.normal(ks[0], (BATCH, SEQ, D_MODEL), f32)
    b_i = 0.1 * jax.random.normal(ks[3], (N_A_LAYERS, MLSTM_HEADS), f32)
    b_f = jnp.linspace(3.0, 6.0, MLSTM_HEADS, dtype=f32)[None, :] + 0.1 * jax.random.normal(ks[4], (N_A_LAYERS, MLSTM_HEADS), f32)
    return {
        'x': x,
        'mlstm_norm': g(ks[1], (N_A_LAYERS, D_MODEL)),
        'mlstm_w_in': w(ks[2], (N_A_LAYERS, D_MODEL, MLSTM_PROJ), D_MODEL),
        'mlstm_b_if': jnp.stack([b_i, b_f], axis=1),
        'mlstm_out_norm': g(ks[5], (N_A_LAYERS, MLSTM_HEADS * MLSTM_DV)),
        'mlstm_w_out': w(ks[6], (N_A_LAYERS, MLSTM_HEADS * MLSTM_DV, D_MODEL), MLSTM_HEADS * MLSTM_DV),
        'ffn_norm': g(ks[7], (DEPTH, D_MODEL)),
        'dense_w_gu': w(ks[8], (N_DENSE, D_MODEL, 2 * D_FF), D_MODEL),
        'dense_w_down': w(ks[9], (N_DENSE, D_FF, D_MODEL), D_FF),
        'moe_router': w(ks[10], (N_MOE, D_MODEL, N_EXPERTS), D_MODEL),
        'moe_w_gu': w(ks[11], (N_MOE, N_EXPERTS, D_MODEL, 2 * D_EXPERT), D_MODEL),
        'moe_w_down': w(ks[12], (N_MOE, N_EXPERTS, D_EXPERT, D_MODEL), D_EXPERT),
        'kv_norm': g(ks[13], (D_MODEL,)),
        'kv_w': w(ks[14], (D_MODEL, 2 * ATT_KV_HEADS * HEAD_DIM), D_MODEL),
        'moba_norm': g(ks[15], (N_B_LAYERS, D_MODEL)),
        'moba_w_q': w(ks[16], (N_B_LAYERS, D_MODEL, ATT_HEADS * HEAD_DIM), D_MODEL),
        'moba_w_o': w(ks[17], (N_B_LAYERS, ATT_HEADS * HEAD_DIM, D_MODEL), ATT_HEADS * HEAD_DIM),
        'final_norm': g(ks[18], (D_MODEL,)),
    }


def reference(x, mlstm_norm, mlstm_w_in, mlstm_b_if, mlstm_out_norm, mlstm_w_out, ffn_norm,
              dense_w_gu, dense_w_down, moe_router, moe_w_gu, moe_w_down, kv_norm, kv_w,
              moba_norm, moba_w_q, moba_w_o, final_norm):
    S = x.shape[1]
    pos = jnp.arange(S, dtype=jnp.float32)
    inv_freq = ROPE_THETA ** (-jnp.arange(0, ROT_DIM, 2, dtype=jnp.float32) / ROT_DIM)
    ang = pos[:, None] * inv_freq[None, :]
    cos, sin = jnp.cos(ang), jnp.sin(ang)

    shared = None
    for l in range(DEPTH):
        if l < N_A_LAYERS:
            x = x + mlstm_mixer(rmsnorm(x, mlstm_norm[l]), mlstm_w_in[l], mlstm_b_if[l],
                                mlstm_out_norm[l], mlstm_w_out[l])
        else:
            if shared is None:
                shared = shared_kv(x, kv_norm, kv_w, cos, sin)
            j = l - N_A_LAYERS
            k_blocks, v_blocks, k_means = shared
            x = x + moba_mixer(rmsnorm(x, moba_norm[j]), moba_w_q[j], moba_w_o[j],
                               k_blocks, v_blocks, k_means, cos, sin)
        hn = rmsnorm(x, ffn_norm[l])
        if l % 2 == 0:
            x = x + swiglu(hn, dense_w_gu[l // 2], dense_w_down[l // 2])
        else:
            x = x + moe_ffn(hn, moe_router[l // 2], moe_w_gu[l // 2], moe_w_down[l // 2])
    return rmsnorm(x, final_norm)
```

```python
from functools import partial

import jax
import jax.numpy as jnp
from jax import lax
from jax.experimental import pallas as pl
from jax.experimental.pallas import tpu as pltpu

NORM_EPS = 1e-6

MLSTM_HEADS = 4
MLSTM_DQK = 128
MLSTM_DV = 256
MLSTM_CHUNK = 256

ATT_HEADS = 8
ATT_KV_HEADS = 2
HEAD_DIM = 128
MOBA_BLOCK = 256
MOBA_TOPK = 3
ROPE_THETA = 500000.0
ROT_DIM = HEAD_DIM // 4

N_EXPERTS = 8

LANES = 128
NEG_BIG = -1e30
VMEM_LIMIT = 56 * 1024 * 1024

F32 = jnp.float32
BF16 = jnp.bfloat16

_NT = (((1,), (1,)), ((), ()))


def _dot(a, b):
    return jnp.dot(a, b, preferred_element_type=F32)


def _dot_nt(a, b, precision=None):
    return lax.dot_general(a, b, _NT, precision=precision, preferred_element_type=F32)


def _rms_scale(x):
    return lax.rsqrt(jnp.mean(x * x, axis=-1, keepdims=True) + NORM_EPS)


def _sigmoid(x):
    return 1.0 / (1.0 + jnp.exp(-x))


def _params(*sem):
    return pltpu.CompilerParams(dimension_semantics=sem, vmem_limit_bytes=VMEM_LIMIT)


def _const_spec(shape):
    nd = len(shape)
    return pl.BlockSpec(shape, lambda *_: (0,) * nd, pipeline_mode=pl.Buffered(1))


def _mlstm_proj_kernel(x_ref, g_ref, w_ref, wg_ref, b_ref, proj_ref, gates_ref, *, col_chunk):
    x = x_ref[...]
    xn = x * _rms_scale(x) * g_ref[...]
    xb = xn.astype(BF16)
    for c in range(w_ref.shape[1] // col_chunk):
        sl = slice(c * col_chunk, (c + 1) * col_chunk)
        proj_ref[:, sl] = _dot(xb, w_ref[:, sl]).astype(BF16)
    gates_ref[...] = _dot_nt(wg_ref[...], xn, precision=lax.Precision.HIGHEST) + b_ref[...]


def mlstm_proj(x, gain, w_main, w_gate_t, bias, *, tm=512, col_chunk=512):
    n, d = x.shape
    p = w_main.shape[1]
    return pl.pallas_call(
        partial(_mlstm_proj_kernel, col_chunk=col_chunk),
        out_shape=(jax.ShapeDtypeStruct((n, p), BF16), jax.ShapeDtypeStruct((8, n), F32)),
        grid=(n // tm,),
        in_specs=[pl.BlockSpec((tm, d), lambda i: (i, 0)),
                  _const_spec((1, d)), _const_spec((d, p)), _const_spec((8, d)), _const_spec((8, 1))],
        out_specs=(pl.BlockSpec((tm, p), lambda i: (i, 0)), pl.BlockSpec((8, tm), lambda i: (0, i))),
        compiler_params=_params("parallel"),
        name="mlstm_proj",
    )(x, gain, w_main, w_gate_t, bias)


def _mlstm_core_kernel(q_ref, k_ref, v_ref, o_ref, gates_ref, gain_ref, out_ref, state_ref, m_ref):
    L = q_ref.shape[0]
    H, DK, DV = MLSTM_HEADS, MLSTM_DQK, MLSTM_DV
    scale = DK ** -0.5

    @pl.when(pl.program_id(1) == 0)
    def _():
        state_ref[...] = jnp.zeros_like(state_ref)
        m_ref[...] = jnp.zeros_like(m_ref)

    row = lax.broadcasted_iota(jnp.int32, (L, L), 0)
    col = lax.broadcasted_iota(jnp.int32, (L, L), 1)
    causal = col <= row
    upper_incl = (row <= col).astype(F32)
    ones_blk = jnp.ones((L, LANES), BF16)

    for h in range(H):
        ig = gates_ref[h:h + 1, :]
        fg = gates_ref[H + h:H + h + 1, :]
        logf = jnp.minimum(fg, 0.0) - jnp.log1p(jnp.exp(-jnp.abs(fg)))
        bcum = jnp.dot(jnp.broadcast_to(logf, (8, L)), upper_incl,
                       precision=lax.Precision.HIGHEST, preferred_element_type=F32)[0:1, :]
        bcum_s = jnp.broadcast_to(bcum, (L, L))
        bcum_t = bcum_s.T
        m_prev = m_ref[h, 0:1, 0:1]

        dmat = jnp.where(causal, bcum_t - bcum_s + ig, -jnp.inf)
        inter = bcum_t[:, 0:1] + m_prev
        m_t = jnp.maximum(inter, jnp.max(dmat, axis=-1, keepdims=True))
        inter_w = jnp.exp(inter - m_t) * scale

        q = q_ref[:, h * DK:(h + 1) * DK]
        k = k_ref[:, h * DK:(h + 1) * DK]
        v_aug = jnp.concatenate([v_ref[:, h * DV:(h + 1) * DV], ones_blk], axis=1)
        state = state_ref[h]

        p = _dot_nt(q, k) * scale * jnp.exp(dmat - m_t)
        num = _dot(p.astype(BF16), v_aug) + inter_w * _dot(q, state.astype(BF16))
        nq = num[:, DV:DV + 1]
        den = jnp.maximum(jnp.abs(nq), jnp.exp(-m_t))
        hval = num[:, :DV] / den

        g_tot = bcum[:, L - 1:L]
        a_end = g_tot - bcum + ig
        m_new = jnp.maximum(g_tot + m_prev, jnp.max(a_end, axis=-1, keepdims=True))
        w_row = jnp.exp(a_end - m_new)
        decay = jnp.exp(g_tot + m_prev - m_new)
        ktw = (k.astype(F32).T * w_row).astype(BF16)
        state_ref[h] = decay * state + _dot(ktw, v_aug)
        m_ref[h] = jnp.broadcast_to(m_new, m_ref.shape[1:])

        hn = hval * _rms_scale(hval) * gain_ref[:, h * DV:(h + 1) * DV]
        og = o_ref[:, h * DV:(h + 1) * DV].astype(F32)
        out_ref[:, h * DV:(h + 1) * DV] = (hn * _sigmoid(og)).astype(BF16)


def mlstm_core(proj, gates, out_gain, *, batch, seq):
    n = proj.shape[0]
    L = MLSTM_CHUNK
    nc = seq // L
    H, DK, DV = MLSTM_HEADS, MLSTM_DQK, MLSTM_DV
    qw, vw = H * DK, H * DV
    assert 2 * qw == vw
    rows = lambda b, c: b * nc + c
    return pl.pallas_call(
        _mlstm_core_kernel,
        out_shape=jax.ShapeDtypeStruct((n, vw), BF16),
        grid=(batch, nc),
        in_specs=[pl.BlockSpec((L, qw), lambda b, c: (rows(b, c), 0)),
                  pl.BlockSpec((L, qw), lambda b, c: (rows(b, c), 1)),
                  pl.BlockSpec((L, vw), lambda b, c: (rows(b, c), 1)),
                  pl.BlockSpec((L, vw), lambda b, c: (rows(b, c), 2)),
                  pl.BlockSpec((8, L), lambda b, c: (0, rows(b, c))),
                  _const_spec((1, vw))],
        out_specs=pl.BlockSpec((L, vw), lambda b, c: (rows(b, c), 0)),
        scratch_shapes=[pltpu.VMEM((H, DK, DV + LANES), F32), pltpu.VMEM((H, 8, LANES), F32)],
        compiler_params=_params("parallel", "arbitrary"),
        name="mlstm_core",
    )(proj, proj, proj, proj, gates, out_gain)


def _silu_mul(g, u):
    return g * _sigmoid(g) * u


def _mix_ffn_kernel(x_ref, h_ref, wo_ref, g_ref, wgu_ref, wd_ref, out_ref, *, f_chunk):
    d_ff = wd_ref.shape[0]
    x1 = x_ref[...] + _dot(h_ref[...], wo_ref[...])
    hn = (x1 * _rms_scale(x1) * g_ref[...]).astype(BF16)
    acc = jnp.zeros_like(x1)
    for c in range(d_ff // f_chunk):
        lo = c * f_chunk
        g = _dot(hn, wgu_ref[:, lo:lo + f_chunk])
        u = _dot(hn, wgu_ref[:, d_ff + lo:d_ff + lo + f_chunk])
        acc = acc + _dot(_silu_mul(g, u).astype(BF16), wd_ref[lo:lo + f_chunk, :])
    out_ref[...] = x1 + acc


def mix_ffn(x, h, w_out, gain, w_gu, w_down, *, tm=512, f_chunk=256):
    n, d = x.shape
    d_ff = w_down.shape[0]
    assert d_ff % f_chunk == 0
    return pl.pallas_call(
        partial(_mix_ffn_kernel, f_chunk=f_chunk),
        out_shape=jax.ShapeDtypeStruct((n, d), F32),
        grid=(n // tm,),
        in_specs=[pl.BlockSpec((tm, d), lambda i: (i, 0)), pl.BlockSpec((tm, h.shape[1]), lambda i: (i, 0)),
                  _const_spec(w_out.shape), _const_spec((1, d)), _const_spec(w_gu.shape),
                  _const_spec(w_down.shape)],
        out_specs=pl.BlockSpec((tm, d), lambda i: (i, 0)),
        compiler_params=_params("parallel"),
        name="mix_ffn",
    )(x, h, w_out, gain, w_gu, w_down)


def _rope(t, cos, sin_lo, sin_hi):
    half = ROT_DIM // 2
    return t * cos + pltpu.roll(t, LANES - half, 1) * sin_lo + pltpu.roll(t, half, 1) * sin_hi


def _kvq_kernel(x_ref, gkv_ref, gq_ref, wkv_ref, wq_ref, cos_ref, slo_ref, shi_ref,
                q_ref, k_ref, v_ref, km_ref, *, blocks_per_seq):
    tm = x_ref.shape[0]
    nsub = tm // MOBA_BLOCK
    G, DH = ATT_KV_HEADS, HEAD_DIM
    x = x_ref[...]
    xs = x * _rms_scale(x)
    cos, slo, shi = cos_ref[...], slo_ref[...], shi_ref[...]

    kv = _dot((xs * gkv_ref[...]).astype(BF16), wkv_ref[...])
    v_ref[...] = kv[:, G * DH:].astype(BF16)
    first_blk = (pl.program_id(0) * nsub) % blocks_per_seq
    blk = first_blk + lax.broadcasted_iota(jnp.int32, (tm, LANES), 0) // MOBA_BLOCK
    ind = (lax.broadcasted_iota(jnp.int32, (tm, LANES), 1) == blk).astype(BF16)
    for g in range(G):
        kg = _rope(kv[:, g * DH:(g + 1) * DH], cos, slo, shi)
        k_ref[:, 2 * g * DH:(2 * g + 1) * DH] = kg.astype(BF16)
        k_ref[:, (2 * g + 1) * DH:(2 * g + 2) * DH] = ind
        for s in range(nsub):
            km_ref[0, s:s + 1, g * DH:(g + 1) * DH] = jnp.mean(
                kg[s * MOBA_BLOCK:(s + 1) * MOBA_BLOCK, :], axis=0, keepdims=True)

    q = _dot((xs * gq_ref[...]).astype(BF16), wq_ref[...])
    for h in range(ATT_HEADS):
        q_ref[:, h * DH:(h + 1) * DH] = _rope(q[:, h * DH:(h + 1) * DH], cos, slo, shi).astype(BF16)


def kvq_proj(x, g_kv, g_q, w_kv, w_q, cos, sin_lo, sin_hi, *, seq, tm=512):
    n, d = x.shape
    G, DH = ATT_KV_HEADS, HEAD_DIM
    nsub = tm // MOBA_BLOCK
    tiles_per_seq = seq // tm
    tab = pl.BlockSpec((tm, LANES), lambda i: (i % tiles_per_seq, 0))
    return pl.pallas_call(
        partial(_kvq_kernel, blocks_per_seq=seq // MOBA_BLOCK),
        out_shape=(jax.ShapeDtypeStruct((n, ATT_HEADS * DH), BF16),
                   jax.ShapeDtypeStruct((n, 2 * G * DH), BF16),
                   jax.ShapeDtypeStruct((n, G * DH), BF16),
                   jax.ShapeDtypeStruct((n // tm, nsub, G * DH), F32)),
        grid=(n // tm,),
        in_specs=[pl.BlockSpec((tm, d), lambda i: (i, 0)), _const_spec((1, d)), _const_spec((1, d)),
                  _const_spec(w_kv.shape), _const_spec(w_q.shape), tab, tab, tab],
        out_specs=(pl.BlockSpec((tm, ATT_HEADS * DH), lambda i: (i, 0)),
                   pl.BlockSpec((tm, 2 * G * DH), lambda i: (i, 0)),
                   pl.BlockSpec((tm, G * DH), lambda i: (i, 0)),
                   pl.BlockSpec((1, nsub, G * DH), lambda i: (i, 0, 0))),
        compiler_params=_params("parallel"),
        name="kvq_proj",
    )(x, g_kv, g_q, w_kv, w_q, cos, sin_lo, sin_hi)


def _moba_kernel(q_ref, k_ref, v_ref, km_ref, out_ref, m_ref, l_ref, acc_ref):
    QB, DH = MOBA_BLOCK, HEAD_DIM
    hpg = ATT_HEADS // ATT_KV_HEADS
    R = hpg * QB
    scale = DH ** -0.5
    j = pl.program_id(2)

    q = jnp.concatenate([q_ref[:, h * DH:(h + 1) * DH] for h in range(hpg)], axis=0)

    nb = km_ref.shape[0]
    km = jnp.concatenate([km_ref[...], jnp.zeros((LANES - nb, DH), F32)], axis=0)
    gate = _dot_nt(q.astype(F32), km, precision=lax.Precision.HIGHEST)
    lane = lax.broadcasted_iota(jnp.int32, (R, LANES), 1)
    beats = jnp.zeros((R, LANES), jnp.int32)
    for mb in range(nb):
        gm = gate[:, mb:mb + 1]
        ahead = (gm > gate) | ((gm == gate) & (mb < lane))
        beats = beats + jnp.where(ahead, 1, 0) * (mb < j).astype(jnp.int32)
    selected = (lane < j) & (beats < MOBA_TOPK)
    pen = jnp.where(selected, 0.0, NEG_BIG).astype(BF16)
    q_aug = jnp.concatenate([q, pen], axis=1)

    k_own = k_ref[pl.ds(pl.multiple_of(j * QB, QB), QB), :]
    v_own = v_ref[pl.ds(pl.multiple_of(j * QB, QB), QB), :]
    s = _dot_nt(q, k_own[:, :DH]) * scale
    t_pos = lax.broadcasted_iota(jnp.int32, (R, QB), 0) % QB
    s_pos = lax.broadcasted_iota(jnp.int32, (R, QB), 1)
    s = jnp.where(s_pos <= t_pos, s, -jnp.inf)
    m0 = jnp.max(s, axis=-1, keepdims=True)
    p = jnp.exp(s - m0)
    m_ref[...] = jnp.broadcast_to(m0, m_ref.shape)
    l_ref[...] = jnp.broadcast_to(jnp.sum(p, axis=-1, keepdims=True), l_ref.shape)
    acc_ref[...] = _dot(p.astype(BF16), v_own)

    def body(n, carry):
        start = pl.multiple_of(n * QB, QB)
        k_n = k_ref[pl.ds(start, QB), :]
        v_n = v_ref[pl.ds(start, QB), :]
        s = _dot_nt(q_aug, k_n) * scale
        m_old = m_ref[...]
        m_new = jnp.maximum(m_old, jnp.max(s, axis=-1, keepdims=True))
        alpha = jnp.exp(m_old - m_new)
        p = jnp.exp(s - m_new[:, 0:1])
        l_ref[...] = alpha * l_ref[...] + jnp.sum(p, axis=-1, keepdims=True)
        acc_ref[...] = alpha * acc_ref[...] + _dot(p.astype(BF16), v_n)
        m_ref[...] = m_new
        return carry

    lax.fori_loop(0, j, body, 0)

    out = acc_ref[...] / l_ref[...]
    for h in range(hpg):
        out_ref[:, h * DH:(h + 1) * DH] = out[h * QB:(h + 1) * QB, :].astype(BF16)


def moba_attn(q, k_aug, v, k_means, *, batch, seq):
    n = q.shape[0]
    G, DH, QB = ATT_KV_HEADS, HEAD_DIM, MOBA_BLOCK
    hpg = ATT_HEADS // G
    nb = seq // QB
    assert DH == LANES and nb <= LANES
    R = hpg * QB
    return pl.pallas_call(
        _moba_kernel,
        out_shape=jax.ShapeDtypeStruct((n, ATT_HEADS * DH), BF16),
        grid=(batch, G, nb),
        in_specs=[pl.BlockSpec((QB, hpg * DH), lambda b, g, j: (b * nb + j, g)),
                  pl.BlockSpec((seq, 2 * DH), lambda b, g, j: (b, g)),
                  pl.BlockSpec((seq, DH), lambda b, g, j: (b, g)),
                  pl.BlockSpec((nb, DH), lambda b, g, j: (b, g))],
        out_specs=pl.BlockSpec((QB, hpg * DH), lambda b, g, j: (b * nb + j, g)),
        scratch_shapes=[pltpu.VMEM((R, LANES), F32), pltpu.VMEM((R, LANES), F32), pltpu.VMEM((R, DH), F32)],
        compiler_params=_params("parallel", "parallel", "arbitrary"),
        name="moba_attn",
    )(q, k_aug, v, k_means)


def _oproj_route_kernel(x_ref, a_ref, wo_ref, g_ref, r_ref, x_out_ref, hn_ref, comb_ref):
    x3 = x_ref[...] + _dot(a_ref[...], wo_ref[...])
    x_out_ref[...] = x3
    hn = x3 * _rms_scale(x3) * g_ref[...]
    hn_ref[...] = hn.astype(BF16)
    logits = jnp.dot(hn, r_ref[...], precision=lax.Precision.HIGHEST, preferred_element_type=F32)
    idx = lax.broadcasted_iota(jnp.int32, logits.shape, 1)
    ne = logits.shape[1]
    m1 = jnp.max(logits, axis=-1, keepdims=True)
    i1 = jnp.min(jnp.where(logits == m1, idx, ne), axis=-1, keepdims=True)
    first = idx == i1
    rest = jnp.where(first, -jnp.inf, logits)
    m2 = jnp.max(rest, axis=-1, keepdims=True)
    i2 = jnp.min(jnp.where(rest == m2, idx, ne), axis=-1, keepdims=True)
    second = idx == i2
    e = jnp.exp(m2 - m1)
    comb_ref[...] = jnp.where(first, 1.0 / (1.0 + e), 0.0) + jnp.where(second, e / (1.0 + e), 0.0)


def oproj_route(x, att, w_o, gain, router, *, tm=512):
    n, d = x.shape
    ne = router.shape[1]
    row = lambda w: pl.BlockSpec((tm, w), lambda i: (i, 0))
    return pl.pallas_call(
        _oproj_route_kernel,
        out_shape=(jax.ShapeDtypeStruct((n, d), F32), jax.ShapeDtypeStruct((n, d), BF16),
                   jax.ShapeDtypeStruct((n, ne), F32)),
        grid=(n // tm,),
        in_specs=[row(d), row(att.shape[1]), _const_spec(w_o.shape), _const_spec((1, d)),
                  _const_spec(router.shape)],
        out_specs=(row(d), row(d), row(ne)),
        compiler_params=_params("parallel"),
        name="oproj_route",
    )(x, att, w_o, gain, router)


def _moe_kernel(hn_ref, comb_ref, wg_ref, wu_ref, wd_ref, x_ref, gf_ref, out_ref, acc_ref):
    e, f = pl.program_id(1), pl.program_id(2)

    @pl.when((e == 0) & (f == 0))
    def _():
        acc_ref[...] = jnp.zeros_like(acc_ref)

    hn = hn_ref[...]
    comb = comb_ref[...]
    lane = lax.broadcasted_iota(jnp.int32, comb.shape, 1)
    c_e = jnp.sum(jnp.where(lane == e, comb, 0.0), axis=-1, keepdims=True)
    a = _silu_mul(_dot(hn, wg_ref[0]), _dot(hn, wu_ref[0])) * c_e
    acc_ref[...] += _dot(a.astype(BF16), wd_ref[0])

    @pl.when((e == pl.num_programs(1) - 1) & (f == pl.num_programs(2) - 1))
    def _():
        y = x_ref[...] + acc_ref[...]
        out_ref[...] = y * _rms_scale(y) * gf_ref[...]


def moe_ffn(hn, comb, w_gu, w_down, x, g_final, *, tm=1024, f_chunk=512):
    n, d = x.shape
    ne, d_exp = w_down.shape[0], w_down.shape[1]
    nf = d_exp // f_chunk
    assert nf * f_chunk == d_exp
    return pl.pallas_call(
        _moe_kernel,
        out_shape=jax.ShapeDtypeStruct((n, d), F32),
        grid=(n // tm, ne, nf),
        in_specs=[pl.BlockSpec((tm, d), lambda i, e, f: (i, 0)),
                  pl.BlockSpec((tm, ne), lambda i, e, f: (i, 0)),
                  pl.BlockSpec((1, d, f_chunk), lambda i, e, f: (e, 0, f)),
                  pl.BlockSpec((1, d, f_chunk), lambda i, e, f: (e, 0, nf + f)),
                  pl.BlockSpec((1, f_chunk, d), lambda i, e, f: (e, f, 0)),
                  pl.BlockSpec((tm, d), lambda i, e, f: (i, 0)),
                  pl.BlockSpec((1, d), lambda i, e, f: (0, 0))],
        out_specs=pl.BlockSpec((tm, d), lambda i, e, f: (i, 0)),
        scratch_shapes=[pltpu.VMEM((tm, d), F32)],
        compiler_params=_params("parallel", "arbitrary", "arbitrary"),
        name="moe_ffn",
    )(hn, comb, w_gu, w_gu, w_down, x, g_final)


def _rope_tables(seq):
    half = ROT_DIM // 2
    pos = jnp.arange(seq, dtype=F32)
    inv_freq = ROPE_THETA ** (-jnp.arange(0, ROT_DIM, 2, dtype=F32) / ROT_DIM)
    ang = pos[:, None] * inv_freq[None, :]
    cos, sin = jnp.cos(ang), jnp.sin(ang)
    pad = jnp.zeros((seq, LANES - ROT_DIM), F32)
    zeros = jnp.zeros((seq, half), F32)
    cos_t = jnp.concatenate([cos, cos, pad + 1.0], axis=1)
    sin_lo = jnp.concatenate([-sin, zeros, pad], axis=1)
    sin_hi = jnp.concatenate([zeros, sin, pad], axis=1)
    return cos_t, sin_lo, sin_hi


def kernel(x, mlstm_norm, mlstm_w_in, mlstm_b_if, mlstm_out_norm, mlstm_w_out, ffn_norm, dense_w_gu,
           dense_w_down, moe_router, moe_w_gu, moe_w_down, kv_norm, kv_w, moba_norm, moba_w_q, moba_w_o,
           final_norm):
    batch, seq, d = x.shape
    n = batch * seq
    H = MLSTM_HEADS
    n_main = H * (2 * MLSTM_DQK + 2 * MLSTM_DV)
    xf = x.reshape(n, d)
    row = lambda g: g.reshape(1, -1).astype(F32)

    w_in = mlstm_w_in[0]
    proj, gates = mlstm_proj(xf, row(mlstm_norm[0]), w_in[:, :n_main].astype(BF16),
                             w_in[:, n_main:].T, mlstm_b_if[0].reshape(2 * H, 1))
    hg = mlstm_core(proj, gates, row(mlstm_out_norm[0]), batch=batch, seq=seq)
    x2 = mix_ffn(xf, hg, mlstm_w_out[0].astype(BF16), row(ffn_norm[0]),
                 dense_w_gu[0].astype(BF16), dense_w_down[0].astype(BF16))

    cos_t, sin_lo, sin_hi = _rope_tables(seq)
    q, k_aug, v, k_means = kvq_proj(x2, row(kv_norm), row(moba_norm[0]), kv_w.astype(BF16),
                                    moba_w_q[0].astype(BF16), cos_t, sin_lo, sin_hi, seq=seq)
    k_means = k_means.reshape(n // MOBA_BLOCK, ATT_KV_HEADS * HEAD_DIM)
    att = moba_attn(q, k_aug, v, k_means, batch=batch, seq=seq)
    x3, hn, comb = oproj_route(x2, att, moba_w_o[0].astype(BF16), row(ffn_norm[1]), moe_router[0])
    out = moe_ffn(hn, comb, moe_w_gu[0].astype(BF16), moe_w_down[0].astype(BF16), x3, row(final_norm))
    return out.reshape(batch, seq, d)
```

```python
from functools import partial

import jax
import jax.numpy as jnp
from jax import lax
from jax.experimental import pallas as pl
from jax.experimental.pallas import tpu as pltpu

NORM_EPS = 1e-6

MLSTM_HEADS = 4
MLSTM_DQK = 128
MLSTM_DV = 256
MLSTM_CHUNK = 256

ATT_HEADS = 8
ATT_KV_HEADS = 2
HEAD_DIM = 128
MOBA_BLOCK = 256
MOBA_TOPK = 3
ROPE_THETA = 500000.0
ROT_DIM = HEAD_DIM // 4

N_EXPERTS = 8

LANES = 128
NEG_BIG = -1e30
VMEM_LIMIT = 56 * 1024 * 1024

F32 = jnp.float32
BF16 = jnp.bfloat16

_NT = (((1,), (1,)), ((), ()))


def _dot(a, b):
    return jnp.dot(a, b, preferred_element_type=F32)


def _dot_nt(a, b, precision=None):
    return lax.dot_general(a, b, _NT, precision=precision, preferred_element_type=F32)


def _split_bf16(a):
    hi = a.astype(BF16)
    return hi, (a - hi.astype(F32)).astype(BF16)


def _dot_nt_split(w, x_hi, x_lo):
    r = w.shape[0]
    w_hi, w_lo = _split_bf16(w)
    a = _dot_nt(jnp.concatenate([w_hi, w_lo], axis=0), x_hi)
    return a[:r] + a[r:] + _dot_nt(w_hi, x_lo)


def _rms_scale(x):
    return lax.rsqrt(jnp.mean(x * x, axis=-1, keepdims=True) + NORM_EPS)


def _sigmoid(x):
    return 1.0 / (1.0 + jnp.exp(-x))


def _params(*sem):
    return pltpu.CompilerParams(dimension_semantics=sem, vmem_limit_bytes=VMEM_LIMIT)


def _const_spec(shape):
    nd = len(shape)
    return pl.BlockSpec(shape, lambda *_: (0,) * nd, pipeline_mode=pl.Buffered(1))


def _mlstm_proj_kernel(x_ref, g_ref, w_ref, wg_ref, b_ref, proj_ref, gates_ref, *, col_chunk):
    x = x_ref[...]
    xn = x * _rms_scale(x) * g_ref[...]
    xb, x_lo = _split_bf16(xn)
    for c in range(w_ref.shape[1] // col_chunk):
        sl = slice(c * col_chunk, (c + 1) * col_chunk)
        proj_ref[:, sl] = _dot(xb, w_ref[:, sl]).astype(BF16)
    gates_ref[...] = _dot_nt_split(wg_ref[...], xb, x_lo) + b_ref[...]


def mlstm_proj(x, gain, w_main, w_gate_t, bias, *, tm=512, col_chunk=512):
    n, d = x.shape
    p = w_main.shape[1]
    return pl.pallas_call(
        partial(_mlstm_proj_kernel, col_chunk=col_chunk),
        out_shape=(jax.ShapeDtypeStruct((n, p), BF16), jax.ShapeDtypeStruct((8, n), F32)),
        grid=(n // tm,),
        in_specs=[pl.BlockSpec((tm, d), lambda i: (i, 0)),
                  _const_spec((1, d)), _const_spec((d, p)), _const_spec((8, d)), _const_spec((8, 1))],
        out_specs=(pl.BlockSpec((tm, p), lambda i: (i, 0)), pl.BlockSpec((8, tm), lambda i: (0, i))),
        compiler_params=_params("parallel"),
        name="mlstm_proj",
    )(x, gain, w_main, w_gate_t, bias)


def _mlstm_core_kernel(q_ref, k_ref, v_ref, o_ref, gates_ref, gain_ref, out_ref, state_ref, m_ref):
    L = q_ref.shape[0]
    H, DK, DV = MLSTM_HEADS, MLSTM_DQK, MLSTM_DV
    scale = DK ** -0.5

    @pl.when(pl.program_id(1) == 0)
    def _():
        state_ref[...] = jnp.zeros_like(state_ref)
        m_ref[...] = jnp.zeros_like(m_ref)

    row = lax.broadcasted_iota(jnp.int32, (L, L), 0)
    col = lax.broadcasted_iota(jnp.int32, (L, L), 1)
    causal = col <= row
    upper_incl = (row <= col).astype(F32)
    ones_blk = jnp.ones((L, LANES), BF16)

    for h in range(H):
        ig = gates_ref[h:h + 1, :]
        fg = gates_ref[H + h:H + h + 1, :]
        logf = jnp.minimum(fg, 0.0) - jnp.log1p(jnp.exp(-jnp.abs(fg)))
        bcum = jnp.dot(jnp.broadcast_to(logf, (8, L)), upper_incl,
                       precision=lax.Precision.HIGHEST, preferred_element_type=F32)[0:1, :]
        bcum_s = jnp.broadcast_to(bcum, (L, L))
        bcum_t = bcum_s.T
        m_prev = m_ref[h, 0:1, 0:1]

        dmat = jnp.where(causal, bcum_t - bcum_s + ig, -jnp.inf)
        inter = bcum_t[:, 0:1] + m_prev
        m_t = jnp.maximum(inter, jnp.max(dmat, axis=-1, keepdims=True))
        inter_w = jnp.exp(inter - m_t) * scale

        q = q_ref[:, h * DK:(h + 1) * DK]
        k = k_ref[:, h * DK:(h + 1) * DK]
        v_aug = jnp.concatenate([v_ref[:, h * DV:(h + 1) * DV], ones_blk], axis=1)
        state = state_ref[h]

        p = _dot_nt(q, k) * scale * jnp.exp(dmat - m_t)
        num = _dot(p.astype(BF16), v_aug) + inter_w * _dot(q, state.astype(BF16))
        nq = num[:, DV:DV + 1]
        den = jnp.maximum(jnp.abs(nq), jnp.exp(-m_t))
        hval = num[:, :DV] / den

        g_tot = bcum[:, L - 1:L]
        a_end = g_tot - bcum + ig
        m_new = jnp.maximum(g_tot + m_prev, jnp.max(a_end, axis=-1, keepdims=True))
        w_row = jnp.exp(a_end - m_new)
        decay = jnp.exp(g_tot + m_prev - m_new)
        ktw = (k.astype(F32).T * w_row).astype(BF16)
        state_ref[h] = decay * state + _dot(ktw, v_aug)
        m_ref[h] = jnp.broadcast_to(m_new, m_ref.shape[1:])

        hn = hval * _rms_scale(hval) * gain_ref[:, h * DV:(h + 1) * DV]
        og = o_ref[:, h * DV:(h + 1) * DV].astype(F32)
        out_ref[:, h * DV:(h + 1) * DV] = (hn * _sigmoid(og)).astype(BF16)


def mlstm_core(proj, gates, out_gain, *, batch, seq):
    n = proj.shape[0]
    L = MLSTM_CHUNK
    nc = seq // L
    H, DK, DV = MLSTM_HEADS, MLSTM_DQK, MLSTM_DV
    qw, vw = H * DK, H * DV
    assert 2 * qw == vw
    rows = lambda b, c: b * nc + c
    return pl.pallas_call(
        _mlstm_core_kernel,
        out_shape=jax.ShapeDtypeStruct((n, vw), BF16),
        grid=(batch, nc),
        in_specs=[pl.BlockSpec((L, qw), lambda b, c: (rows(b, c), 0)),
                  pl.BlockSpec((L, qw), lambda b, c: (rows(b, c), 1)),
                  pl.BlockSpec((L, vw), lambda b, c: (rows(b, c), 1)),
                  pl.BlockSpec((L, vw), lambda b, c: (rows(b, c), 2)),
                  pl.BlockSpec((8, L), lambda b, c: (0, rows(b, c))),
                  _const_spec((1, vw))],
        out_specs=pl.BlockSpec((L, vw), lambda b, c: (rows(b, c), 0)),
        scratch_shapes=[pltpu.VMEM((H, DK, DV + LANES), F32), pltpu.VMEM((H, 8, LANES), F32)],
        compiler_params=_params("parallel", "arbitrary"),
        name="mlstm_core",
    )(proj, proj, proj, proj, gates, out_gain)


def _silu_mul(g, u):
    return g * _sigmoid(g) * u


def _mix_ffn_kernel(x_ref, h_ref, wo_ref, g_ref, wgu_ref, wd_ref, out_ref, *, f_chunk):
    d_ff = wd_ref.shape[0]
    x1 = x_ref[...] + _dot(h_ref[...], wo_ref[...])
    hn = (x1 * _rms_scale(x1) * g_ref[...]).astype(BF16)
    acc = jnp.zeros_like(x1)
    for c in range(d_ff // f_chunk):
        lo = c * f_chunk
        g = _dot(hn, wgu_ref[:, lo:lo + f_chunk])
        u = _dot(hn, wgu_ref[:, d_ff + lo:d_ff + lo + f_chunk])
        acc = acc + _dot(_silu_mul(g, u).astype(BF16), wd_ref[lo:lo + f_chunk, :])
    out_ref[...] = x1 + acc


def mix_ffn(x, h, w_out, gain, w_gu, w_down, *, tm=512, f_chunk=256):
    n, d = x.shape
    d_ff = w_down.shape[0]
    assert d_ff % f_chunk == 0
    return pl.pallas_call(
        partial(_mix_ffn_kernel, f_chunk=f_chunk),
        out_shape=jax.ShapeDtypeStruct((n, d), F32),
        grid=(n // tm,),
        in_specs=[pl.BlockSpec((tm, d), lambda i: (i, 0)), pl.BlockSpec((tm, h.shape[1]), lambda i: (i, 0)),
                  _const_spec(w_out.shape), _const_spec((1, d)), _const_spec(w_gu.shape),
                  _const_spec(w_down.shape)],
        out_specs=pl.BlockSpec((tm, d), lambda i: (i, 0)),
        compiler_params=_params("parallel"),
        name="mix_ffn",
    )(x, h, w_out, gain, w_gu, w_down)


def _rope(t, cos, sin_lo, sin_hi):
    half = ROT_DIM // 2
    return t * cos + pltpu.roll(t, LANES - half, 1) * sin_lo + pltpu.roll(t, half, 1) * sin_hi


def _kvq_kernel(x_ref, gkv_ref, gq_ref, wkv_ref, wq_ref, cos_ref, slo_ref, shi_ref,
                q_ref, k_ref, v_ref, km_ref, *, blocks_per_seq):
    tm = x_ref.shape[0]
    nsub = tm // MOBA_BLOCK
    G, DH = ATT_KV_HEADS, HEAD_DIM
    x = x_ref[...]
    xs = x * _rms_scale(x)
    cos, slo, shi = cos_ref[...], slo_ref[...], shi_ref[...]

    kv = _dot((xs * gkv_ref[...]).astype(BF16), wkv_ref[...])
    v_ref[...] = kv[:, G * DH:].astype(BF16)
    first_blk = (pl.program_id(0) * nsub) % blocks_per_seq
    blk = first_blk + lax.broadcasted_iota(jnp.int32, (tm, LANES), 0) // MOBA_BLOCK
    ind = (lax.broadcasted_iota(jnp.int32, (tm, LANES), 1) == blk).astype(BF16)
    for g in range(G):
        kg = _rope(kv[:, g * DH:(g + 1) * DH], cos, slo, shi)
        k_ref[:, 2 * g * DH:(2 * g + 1) * DH] = kg.astype(BF16)
        k_ref[:, (2 * g + 1) * DH:(2 * g + 2) * DH] = ind
        for s in range(nsub):
            km_ref[0, s:s + 1, g * DH:(g + 1) * DH] = jnp.mean(
                kg[s * MOBA_BLOCK:(s + 1) * MOBA_BLOCK, :], axis=0, keepdims=True)

    q = _dot((xs * gq_ref[...]).astype(BF16), wq_ref[...])
    for h in range(ATT_HEADS):
        q_ref[:, h * DH:(h + 1) * DH] = _rope(q[:, h * DH:(h + 1) * DH], cos, slo, shi).astype(BF16)


def kvq_proj(x, g_kv, g_q, w_kv, w_q, cos, sin_lo, sin_hi, *, seq, tm=512):
    n, d = x.shape
    G, DH = ATT_KV_HEADS, HEAD_DIM
    nsub = tm // MOBA_BLOCK
    tiles_per_seq = seq // tm
    tab = pl.BlockSpec((tm, LANES), lambda i: (i % tiles_per_seq, 0))
    return pl.pallas_call(
        partial(_kvq_kernel, blocks_per_seq=seq // MOBA_BLOCK),
        out_shape=(jax.ShapeDtypeStruct((n, ATT_HEADS * DH), BF16),
                   jax.ShapeDtypeStruct((n, 2 * G * DH), BF16),
                   jax.ShapeDtypeStruct((n, G * DH), BF16),
                   jax.ShapeDtypeStruct((n // tm, nsub, G * DH), F32)),
        grid=(n // tm,),
        in_specs=[pl.BlockSpec((tm, d), lambda i: (i, 0)), _const_spec((1, d)), _const_spec((1, d)),
                  _const_spec(w_kv.shape), _const_spec(w_q.shape), tab, tab, tab],
        out_specs=(pl.BlockSpec((tm, ATT_HEADS * DH), lambda i: (i, 0)),
                   pl.BlockSpec((tm, 2 * G * DH), lambda i: (i, 0)),
                   pl.BlockSpec((tm, G * DH), lambda i: (i, 0)),
                   pl.BlockSpec((1, nsub, G * DH), lambda i: (i, 0, 0))),
        compiler_params=_params("parallel"),
        name="kvq_proj",
    )(x, g_kv, g_q, w_kv, w_q, cos, sin_lo, sin_hi)


def _moba_kernel(q_ref, k_ref, v_ref, km_ref, out_ref, m_ref, l_ref, acc_ref):
    QB, DH = MOBA_BLOCK, HEAD_DIM
    hpg = ATT_HEADS // ATT_KV_HEADS
    R = hpg * QB
    scale = DH ** -0.5
    j = pl.program_id(2)

    q = jnp.concatenate([q_ref[:, h * DH:(h + 1) * DH] for h in range(hpg)], axis=0)

    nb = km_ref.shape[0]
    km_hi, km_lo = _split_bf16(km_ref[...])
    g2 = _dot_nt(jnp.concatenate([km_hi, km_lo], axis=0), q)
    gate = g2[:nb] + g2[nb:]
    blk = lax.broadcasted_iota(jnp.int32, (nb, R), 0)
    beats = jnp.zeros((nb, R), jnp.int32)
    for mb in range(nb):
        gm = gate[mb:mb + 1, :]
        ahead = (gm > gate) | ((gm == gate) & (mb < blk))
        beats = beats + jnp.where(ahead, 1, 0) * (mb < j).astype(jnp.int32)
    selected = jnp.where((blk < j) & (beats < MOBA_TOPK), 1.0, 0.0).astype(BF16)
    eye = jnp.where(lax.broadcasted_iota(jnp.int32, (nb, LANES), 0)
                    == lax.broadcasted_iota(jnp.int32, (nb, LANES), 1), 1.0, 0.0).astype(BF16)
    sel_rows = lax.dot_general(selected, eye, (((0,), (0,)), ((), ())), preferred_element_type=F32)
    pen = ((1.0 - sel_rows) * NEG_BIG).astype(BF16)
    q_aug = jnp.concatenate([q, pen], axis=1)

    k_own = k_ref[pl.ds(pl.multiple_of(j * QB, QB), QB), :]
    v_own = v_ref[pl.ds(pl.multiple_of(j * QB, QB), QB), :]
    s = _dot_nt(q, k_own[:, :DH]) * scale
    t_pos = lax.broadcasted_iota(jnp.int32, (R, QB), 0) % QB
    s_pos = lax.broadcasted_iota(jnp.int32, (R, QB), 1)
    s = jnp.where(s_pos <= t_pos, s, -jnp.inf)
    m0 = jnp.max(s, axis=-1, keepdims=True)
    p = jnp.exp(s - m0)
    m_ref[...] = jnp.broadcast_to(m0, m_ref.shape)
    l_ref[...] = jnp.broadcast_to(jnp.sum(p, axis=-1, keepdims=True), l_ref.shape)
    acc_ref[...] = _dot(p.astype(BF16), v_own)

    def body(n, carry):
        start = pl.multiple_of(n * QB, QB)
        k_n = k_ref[pl.ds(start, QB), :]
        v_n = v_ref[pl.ds(start, QB), :]
        s = _dot_nt(q_aug, k_n) * scale
        m_old = m_ref[...]
        m_new = jnp.maximum(m_old, jnp.max(s, axis=-1, keepdims=True))
        alpha = jnp.exp(m_old - m_new)
        p = jnp.exp(s - jnp.concatenate([m_new] * (QB // LANES), axis=1))
        l_ref[...] = alpha * l_ref[...] + jnp.sum(p, axis=-1, keepdims=True)
        acc_ref[...] = alpha * acc_ref[...] + _dot(p.astype(BF16), v_n)
        m_ref[...] = m_new
        return carry

    lax.fori_loop(0, j, body, 0)

    out = acc_ref[...] / l_ref[...]
    for h in range(hpg):
        out_ref[:, h * DH:(h + 1) * DH] = out[h * QB:(h + 1) * QB, :].astype(BF16)


def moba_attn(q, k_aug, v, k_means, *, batch, seq):
    n = q.shape[0]
    G, DH, QB = ATT_KV_HEADS, HEAD_DIM, MOBA_BLOCK
    hpg = ATT_HEADS // G
    nb = seq // QB
    assert DH == LANES and nb <= LANES
    R = hpg * QB
    return pl.pallas_call(
        _moba_kernel,
        out_shape=jax.ShapeDtypeStruct((n, ATT_HEADS * DH), BF16),
        grid=(batch, G, nb),
        in_specs=[pl.BlockSpec((QB, hpg * DH), lambda b, g, j: (b * nb + j, g)),
                  pl.BlockSpec((seq, 2 * DH), lambda b, g, j: (b, g)),
                  pl.BlockSpec((seq, DH), lambda b, g, j: (b, g)),
                  pl.BlockSpec((nb, DH), lambda b, g, j: (b, g))],
        out_specs=pl.BlockSpec((QB, hpg * DH), lambda b, g, j: (b * nb + j, g)),
        scratch_shapes=[pltpu.VMEM((R, LANES), F32), pltpu.VMEM((R, LANES), F32), pltpu.VMEM((R, DH), F32)],
        compiler_params=_params("parallel", "parallel", "arbitrary"),
        name="moba_attn",
    )(q, k_aug, v, k_means)


def _oproj_route_kernel(x_ref, a_ref, wo_ref, g_ref, r_ref, x_out_ref, hn_ref, sel_ref, comb_ref):
    x3 = x_ref[...] + _dot(a_ref[...], wo_ref[...])
    x_out_ref[...] = x3
    hn = x3 * _rms_scale(x3) * g_ref[...]
    hn_hi, hn_lo = _split_bf16(hn)
    hn_ref[...] = hn_hi
    logits = _dot_nt_split(r_ref[...], hn_hi, hn_lo)
    ne = logits.shape[0]
    idx = lax.broadcasted_iota(jnp.int32, logits.shape, 0)
    m1 = jnp.max(logits, axis=0, keepdims=True)
    i1 = jnp.min(jnp.where(logits == m1, idx, ne), axis=0, keepdims=True)
    first = idx == i1
    rest = jnp.where(first, -jnp.inf, logits)
    m2 = jnp.max(rest, axis=0, keepdims=True)
    i2 = jnp.min(jnp.where(rest == m2, idx, ne), axis=0, keepdims=True)
    second = idx == i2
    e = jnp.exp(m2 - m1)
    sel_ref[...] = jnp.where(first | second, 1.0, 0.0)
    comb_ref[...] = jnp.where(first, 1.0 / (1.0 + e), 0.0) + jnp.where(second, e / (1.0 + e), 0.0)


def oproj_route(x, att, w_o, gain, router_t, *, tm=512):
    n, d = x.shape
    ne = router_t.shape[0]
    row = lambda w: pl.BlockSpec((tm, w), lambda i: (i, 0))
    lane_major = pl.BlockSpec((ne, tm), lambda i: (0, i))
    return pl.pallas_call(
        _oproj_route_kernel,
        out_shape=(jax.ShapeDtypeStruct((n, d), F32), jax.ShapeDtypeStruct((n, d), BF16),
                   jax.ShapeDtypeStruct((ne, n), F32), jax.ShapeDtypeStruct((ne, n), F32)),
        grid=(n // tm,),
        in_specs=[row(d), row(att.shape[1]), _const_spec(w_o.shape), _const_spec((1, d)),
                  _const_spec(router_t.shape)],
        out_specs=(row(d), row(d), lane_major, lane_major),
        compiler_params=_params("parallel"),
        name="oproj_route",
    )(x, att, w_o, gain, router_t)


TOK_BLOCK = 256
SEG_ALIGN = 8
MOE_TM = 512


def _loc_rows(ne):
    need = 2 * TOK_BLOCK + ne * (SEG_ALIGN - 1)
    return -(-need // LANES) * LANES


def _dispatch_plan(sel_t, n_tiles):
    ne, n = sel_t.shape
    nblk = n // TOK_BLOCK
    i32 = jnp.int32
    cnt = sel_t.reshape(ne, nblk, TOK_BLOCK).sum(-1).astype(i32)
    pad = (cnt + SEG_ALIGN - 1) // SEG_ALIGN * SEG_ALIGN
    loff = jnp.cumsum(pad, axis=0) - pad
    tot = pad.sum(axis=1)
    reg = (tot + MOE_TM - 1) // MOE_TM * MOE_TM
    reg_start = jnp.cumsum(reg) - reg
    goff = reg_start[:, None] + jnp.cumsum(pad, axis=1) - pad
    tiles_end = jnp.cumsum(reg // MOE_TM)
    n_valid = tiles_end[-1]
    t = jnp.arange(n_tiles, dtype=i32)
    t_eff = jnp.minimum(t, n_valid - 1)
    tile_expert = (tiles_end[None, :] <= t_eff[:, None]).sum(axis=1).astype(i32)
    tile_valid = (t < n_valid).astype(i32)
    used = reg.sum()
    tail = jnp.concatenate([reg_start + tot, used[None], (reg - tot) // SEG_ALIGN,
                            ((n_tiles * MOE_TM - used) // SEG_ALIGN)[None]]).astype(i32)
    flat = lambda a: a.reshape(-1).astype(i32)
    return flat(pad), flat(loff), flat(goff), tail, tile_expert, t_eff.astype(i32), tile_valid


def _block_rows(pad_ref, blk, nblk, ne):
    tot = pad_ref[blk]
    for e in range(1, ne):
        tot = tot + pad_ref[e * nblk + blk]
    return tot


def _segment_copies(pad_ref, loff_ref, goff_ref, blk, nblk, ne, start_one):
    for e in range(ne):
        n_chunks = lax.shift_right_logical(pad_ref[e * nblk + blk], 3)
        lo, go = loff_ref[e * nblk + blk], goff_ref[e * nblk + blk]

        def body(i, carry, lo=lo, go=go):
            start_one(pl.multiple_of(lo + i * SEG_ALIGN, SEG_ALIGN), pl.multiple_of(go + i * SEG_ALIGN, SEG_ALIGN))
            return carry

        lax.fori_loop(0, n_chunks, body, 0)


def _local_positions(sel, loff_ref, blk, nblk):
    ne, tb = sel.shape
    sub = lax.broadcasted_iota(jnp.int32, (ne, tb), 0)
    row = lax.broadcasted_iota(jnp.int32, (tb, tb), 0)
    col = lax.broadcasted_iota(jnp.int32, (tb, tb), 1)
    before = (row < col).astype(BF16)
    rank = _dot(sel.astype(BF16), before)
    sub1 = lax.broadcasted_iota(jnp.int32, (ne, 1), 0)
    loff = jnp.zeros((ne, 1), F32)
    for e in range(ne):
        loff = jnp.where(sub1 == e, loff_ref[e * nblk + blk].astype(F32), loff)
    chosen = sel > 0.0
    e_lo = jnp.min(jnp.where(chosen, sub, ne), axis=0, keepdims=True)
    e_hi = jnp.max(jnp.where(chosen, sub, -1), axis=0, keepdims=True)
    return loff + rank, sub == e_lo, sub == e_hi


def _pick(a, mask):
    return jnp.sum(jnp.where(mask, a, 0.0), axis=0, keepdims=True)


def _dispatch_kernel(pad_ref, loff_ref, goff_ref, tail_ref, hn_ref, sel_ref, xs_ref, buf_ref, zero_ref,
                     sem, tail_sem, *, nblk, ne):
    b = pl.program_id(0)
    slot = b % 2
    rows = buf_ref.shape[1]

    def wait_block(blk, s):
        tot = pl.multiple_of(_block_rows(pad_ref, blk, nblk, ne), SEG_ALIGN)
        pltpu.make_async_copy(buf_ref.at[s, pl.ds(0, tot)], xs_ref.at[pl.ds(0, tot)], sem.at[s]).wait()

    @pl.when(b >= 2)
    def _():
        wait_block(b - 2, slot)

    lpos, lo_mask, hi_mask = _local_positions(sel_ref[...], loff_ref, b, nblk)
    p1 = _pick(lpos, lo_mask).astype(jnp.int32)
    p2 = _pick(lpos, hi_mask).astype(jnp.int32)
    r = lax.broadcasted_iota(jnp.int32, (rows, TOK_BLOCK), 0)
    perm = jnp.where((r == p1) | (r == p2), 1.0, 0.0).astype(BF16)
    buf_ref[slot] = _dot(perm, hn_ref[...])

    def start_one(lrow, grow):
        pltpu.make_async_copy(buf_ref.at[slot, pl.ds(lrow, SEG_ALIGN)], xs_ref.at[pl.ds(grow, SEG_ALIGN)],
                              sem.at[slot]).start()

    _segment_copies(pad_ref, loff_ref, goff_ref, b, nblk, ne, start_one)

    @pl.when(b == nblk - 1)
    def _():
        zero_ref[...] = jnp.zeros_like(zero_ref)
        n_span = ne + 1
        for sp in range(n_span):
            def start_zero(i, carry, sp=sp):
                grow = pl.multiple_of(tail_ref[sp] + i * SEG_ALIGN, SEG_ALIGN)
                pltpu.make_async_copy(zero_ref, xs_ref.at[pl.ds(grow, SEG_ALIGN)], tail_sem).start()
                return carry

            lax.fori_loop(0, tail_ref[n_span + sp], start_zero, 0)

        for sp in range(n_span):
            def wait_zero(i, carry):
                pltpu.make_async_copy(zero_ref, xs_ref.at[pl.ds(0, SEG_ALIGN)], tail_sem).wait()
                return carry

            lax.fori_loop(0, tail_ref[n_span + sp], wait_zero, 0)

        wait_block(b, slot)
        if nblk > 1:
            wait_block(b - 1, 1 - slot)


def moe_dispatch(plan, hn, sel_t, *, n_tiles):
    pad, loff, goff, tail = plan[:4]
    n, d = hn.shape
    ne = sel_t.shape[0]
    nblk = n // TOK_BLOCK
    rows = _loc_rows(ne)
    grid_spec = pltpu.PrefetchScalarGridSpec(
        num_scalar_prefetch=4,
        grid=(nblk,),
        in_specs=[pl.BlockSpec((TOK_BLOCK, d), lambda b, *_: (b, 0)),
                  pl.BlockSpec((ne, TOK_BLOCK), lambda b, *_: (0, b))],
        out_specs=pl.BlockSpec(memory_space=pltpu.HBM),
        scratch_shapes=[pltpu.VMEM((2, rows, d), F32), pltpu.VMEM((SEG_ALIGN, d), F32),
                        pltpu.SemaphoreType.DMA((2,)), pltpu.SemaphoreType.DMA(())],
    )
    return pl.pallas_call(
        partial(_dispatch_kernel, nblk=nblk, ne=ne),
        out_shape=jax.ShapeDtypeStruct((n_tiles * MOE_TM, d), F32),
        grid_spec=grid_spec,
        compiler_params=_params("arbitrary"),
        name="moe_dispatch",
    )(pad, loff, goff, tail, hn, sel_t)


def _expert_ffn_kernel(texp_ref, teff_ref, tvalid_ref, x_ref, wg_ref, wu_ref, wd_ref, out_ref, xb_ref):
    i, f = pl.program_id(0), pl.program_id(1)

    @pl.when(tvalid_ref[i] == 1)
    def _():
        @pl.when(f == 0)
        def _():
            xb_ref[...] = x_ref[...].astype(BF16)

        xb = xb_ref[...]
        y = _dot(_silu_mul(_dot(xb, wg_ref[0]), _dot(xb, wu_ref[0])).astype(BF16), wd_ref[0])

        @pl.when(f == 0)
        def _():
            out_ref[...] = y

        @pl.when(f > 0)
        def _():
            out_ref[...] += y

    @pl.when((tvalid_ref[i] == 0) & (f == 0))
    def _():
        out_ref[...] = jnp.zeros_like(out_ref)


def expert_ffn(plan, xs, w_gu, w_down, *, f_chunk=1792):
    tile_expert, tile_eff, tile_valid = plan[4:]
    n_rows, d = xs.shape
    d_exp = w_down.shape[1]
    nf = d_exp // f_chunk
    assert nf * f_chunk == d_exp and f_chunk % LANES == 0
    n_tiles = n_rows // MOE_TM
    f_eff = lambda i, f, tv: jnp.where(tv[i] == 1, f, nf - 1)
    grid_spec = pltpu.PrefetchScalarGridSpec(
        num_scalar_prefetch=3,
        grid=(n_tiles, nf),
        in_specs=[pl.BlockSpec((MOE_TM, d), lambda i, f, te, tf, tv: (tf[i], 0)),
                  pl.BlockSpec((1, d, f_chunk), lambda i, f, te, tf, tv: (te[i], 0, f_eff(i, f, tv))),
                  pl.BlockSpec((1, d, f_chunk), lambda i, f, te, tf, tv: (te[i], 0, nf + f_eff(i, f, tv))),
                  pl.BlockSpec((1, f_chunk, d), lambda i, f, te, tf, tv: (te[i], f_eff(i, f, tv), 0))],
        out_specs=pl.BlockSpec((MOE_TM, d), lambda i, f, te, tf, tv: (i, 0)),
        scratch_shapes=[pltpu.VMEM((MOE_TM, d), BF16)],
    )
    return pl.pallas_call(
        _expert_ffn_kernel,
        out_shape=jax.ShapeDtypeStruct((n_rows, d), F32),
        grid_spec=grid_spec,
        compiler_params=_params("arbitrary", "arbitrary"),
        name="expert_ffn",
    )(tile_expert, tile_eff, tile_valid, xs, w_gu, w_gu, w_down)


def _combine_kernel(pad_ref, loff_ref, goff_ref, ys_ref, x_ref, sel_ref, comb_ref, g_ref, out_ref, buf_ref, sem,
                    *, nblk, ne):
    b = pl.program_id(0)
    slot = b % 2
    rows = buf_ref.shape[1]

    def fetch_block(blk, s):
        def start_one(lrow, grow):
            pltpu.make_async_copy(ys_ref.at[pl.ds(grow, SEG_ALIGN)], buf_ref.at[s, pl.ds(lrow, SEG_ALIGN)],
                                  sem.at[s]).start()

        _segment_copies(pad_ref, loff_ref, goff_ref, blk, nblk, ne, start_one)

    @pl.when(b == 0)
    def _():
        buf_ref[...] = jnp.zeros_like(buf_ref)
        fetch_block(0, 0)

    @pl.when(b + 1 < nblk)
    def _():
        fetch_block(b + 1, 1 - slot)

    tot = pl.multiple_of(_block_rows(pad_ref, b, nblk, ne), SEG_ALIGN)
    pltpu.make_async_copy(ys_ref.at[pl.ds(0, tot)], buf_ref.at[slot, pl.ds(0, tot)], sem.at[slot]).wait()

    lpos, lo_mask, hi_mask = _local_positions(sel_ref[...], loff_ref, b, nblk)
    comb = comb_ref[...]
    info = jnp.concatenate([_pick(lpos, lo_mask), _pick(lpos, hi_mask), _pick(comb, lo_mask),
                            _pick(comb, hi_mask), jnp.zeros((4, TOK_BLOCK), F32)], axis=0)
    eye = jnp.where(lax.broadcasted_iota(jnp.int32, (TOK_BLOCK, TOK_BLOCK), 0)
                    == lax.broadcasted_iota(jnp.int32, (TOK_BLOCK, TOK_BLOCK), 1), 1.0, 0.0)
    cols = _dot_nt(eye, info, precision=lax.Precision.HIGHEST)
    c = lax.broadcasted_iota(jnp.int32, (TOK_BLOCK, rows), 1)
    unsort = (jnp.where(c == cols[:, 0:1].astype(jnp.int32), cols[:, 2:3], 0.0)
              + jnp.where(c == cols[:, 1:2].astype(jnp.int32), cols[:, 3:4], 0.0))
    w_hi, w_lo = _split_bf16(unsort)
    local = buf_ref[slot].astype(BF16)
    y = x_ref[...] + _dot(w_hi, local) + _dot(w_lo, local)
    out_ref[...] = y * _rms_scale(y) * g_ref[...]


def moe_combine(plan, ys, x, sel_t, comb_t, g_final):
    pad, loff, goff = plan[:3]
    n, d = x.shape
    ne = sel_t.shape[0]
    nblk = n // TOK_BLOCK
    rows = _loc_rows(ne)
    lane_major = pl.BlockSpec((ne, TOK_BLOCK), lambda b, *_: (0, b))
    grid_spec = pltpu.PrefetchScalarGridSpec(
        num_scalar_prefetch=3,
        grid=(nblk,),
        in_specs=[pl.BlockSpec(memory_space=pltpu.HBM),
                  pl.BlockSpec((TOK_BLOCK, d), lambda b, *_: (b, 0)), lane_major, lane_major,
                  pl.BlockSpec((1, d), lambda b, *_: (0, 0))],
        out_specs=pl.BlockSpec((TOK_BLOCK, d), lambda b, *_: (b, 0)),
        scratch_shapes=[pltpu.VMEM((2, rows, d), F32), pltpu.SemaphoreType.DMA((2,))],
    )
    return pl.pallas_call(
        partial(_combine_kernel, nblk=nblk, ne=ne),
        out_shape=jax.ShapeDtypeStruct((n, d), F32),
        grid_spec=grid_spec,
        compiler_params=_params("arbitrary"),
        name="moe_combine",
    )(pad, loff, goff, ys, x, sel_t, comb_t, g_final)


def moe_ffn(hn, sel_t, comb_t, w_gu, w_down, x, g_final):
    n = hn.shape[0]
    ne = sel_t.shape[0]
    nblk = n // TOK_BLOCK
    n_tiles = -(-(2 * n + nblk * ne * (SEG_ALIGN - 1)) // MOE_TM) + ne
    plan = _dispatch_plan(sel_t, n_tiles)
    xs = moe_dispatch(plan, hn, sel_t, n_tiles=n_tiles)
    ys = expert_ffn(plan, xs, w_gu, w_down)
    return moe_combine(plan, ys, x, sel_t, comb_t, g_final)


def _rope_tables(seq):
    half = ROT_DIM // 2
    pos = jnp.arange(seq, dtype=F32)
    inv_freq = ROPE_THETA ** (-jnp.arange(0, ROT_DIM, 2, dtype=F32) / ROT_DIM)
    ang = pos[:, None] * inv_freq[None, :]
    cos, sin = jnp.cos(ang), jnp.sin(ang)
    pad = jnp.zeros((seq, LANES - ROT_DIM), F32)
    zeros = jnp.zeros((seq, half), F32)
    cos_t = jnp.concatenate([cos, cos, pad + 1.0], axis=1)
    sin_lo = jnp.concatenate([-sin, zeros, pad], axis=1)
    sin_hi = jnp.concatenate([zeros, sin, pad], axis=1)
    return cos_t, sin_lo, sin_hi


def kernel(x, mlstm_norm, mlstm_w_in, mlstm_b_if, mlstm_out_norm, mlstm_w_out, ffn_norm, dense_w_gu,
           dense_w_down, moe_router, moe_w_gu, moe_w_down, kv_norm, kv_w, moba_norm, moba_w_q, moba_w_o,
           final_norm):
    batch, seq, d = x.shape
    n = batch * seq
    H = MLSTM_HEADS
    n_main = H * (2 * MLSTM_DQK + 2 * MLSTM_DV)
    xf = x.reshape(n, d)
    row = lambda g: g.reshape(1, -1).astype(F32)

    w_in = mlstm_w_in[0]
    proj, gates = mlstm_proj(xf, row(mlstm_norm[0]), w_in[:, :n_main].astype(BF16),
                             w_in[:, n_main:].T, mlstm_b_if[0].reshape(2 * H, 1))
    hg = mlstm_core(proj, gates, row(mlstm_out_norm[0]), batch=batch, seq=seq)
    x2 = mix_ffn(xf, hg, mlstm_w_out[0].astype(BF16), row(ffn_norm[0]),
                 dense_w_gu[0].astype(BF16), dense_w_down[0].astype(BF16))

    cos_t, sin_lo, sin_hi = _rope_tables(seq)
    q, k_aug, v, k_means = kvq_proj(x2, row(kv_norm), row(moba_norm[0]), kv_w.astype(BF16),
                                    moba_w_q[0].astype(BF16), cos_t, sin_lo, sin_hi, seq=seq)
    k_means = k_means.reshape(n // MOBA_BLOCK, ATT_KV_HEADS * HEAD_DIM)
    att = moba_attn(q, k_aug, v, k_means, batch=batch, seq=seq)
    x3, hn, sel_t, comb_t = oproj_route(x2, att, moba_w_o[0].astype(BF16), row(ffn_norm[1]), moe_router[0].T)
    out = moe_ffn(hn, sel_t, comb_t, moe_w_gu[0].astype(BF16), moe_w_down[0].astype(BF16), x3, row(final_norm))
    return out.reshape(batch, seq, d)
```

```python
import math
from functools import partial

import jax
import jax.numpy as jnp
from jax import lax
from jax.experimental import pallas as pl
from jax.experimental.pallas import tpu as pltpu

NORM_EPS = 1e-6

MLSTM_HEADS = 4
MLSTM_DQK = 128
MLSTM_DV = 256
MLSTM_CHUNK = 256
MLSTM_ROWS = 256

ATT_HEADS = 8
ATT_KV_HEADS = 2
HEAD_DIM = 128
MOBA_BLOCK = 256
MOBA_TOPK = 3
ROPE_THETA = 500000.0
ROT_DIM = HEAD_DIM // 4

N_EXPERTS = 8

LANES = 128
NEG_BIG = -1e30
LOG2E = 1.4426950408889634
VMEM_LIMIT = 56 * 1024 * 1024

F32 = jnp.float32
BF16 = jnp.bfloat16

_NT = (((1,), (1,)), ((), ()))


def _dot(a, b):
    return jnp.dot(a, b, preferred_element_type=F32)


def _dot_nt(a, b, precision=None):
    return lax.dot_general(a, b, _NT, precision=precision, preferred_element_type=F32)


def _split_bf16(a):
    hi = a.astype(BF16)
    return hi, (a - hi.astype(F32)).astype(BF16)


def _dot_nt_split(w, x_hi, x_lo):
    r = w.shape[0]
    w_hi, w_lo = _split_bf16(w)
    a = _dot_nt(jnp.concatenate([w_hi, w_lo], axis=0), x_hi)
    return a[:r] + a[r:] + _dot_nt(w_hi, x_lo)


def _rms_scale(x):
    return lax.rsqrt(jnp.mean(x * x, axis=-1, keepdims=True) + NORM_EPS)


def _sigmoid(x):
    return 1.0 / (1.0 + jnp.exp(-x))


def _params(*sem):
    return pltpu.CompilerParams(dimension_semantics=sem, vmem_limit_bytes=VMEM_LIMIT)


def _const_spec(shape):
    nd = len(shape)
    return pl.BlockSpec(shape, lambda *_: (0,) * nd, pipeline_mode=pl.Buffered(1))


def _mlstm_proj_kernel(x_ref, g_ref, w_ref, wg_ref, b_ref, proj_ref, gates_ref, *, col_chunk):
    x = x_ref[...]
    xn = x * _rms_scale(x) * g_ref[...]
    xb, x_lo = _split_bf16(xn)
    for c in range(w_ref.shape[1] // col_chunk):
        sl = slice(c * col_chunk, (c + 1) * col_chunk)
        proj_ref[:, sl] = _dot(xb, w_ref[:, sl]).astype(BF16)
    gates_ref[...] = _dot_nt_split(wg_ref[...], xb, x_lo) + b_ref[...]


def mlstm_proj(x, gain, w_main, w_gate_t, bias, *, tm=512, col_chunk=512):
    n, d = x.shape
    p = w_main.shape[1]
    return pl.pallas_call(
        partial(_mlstm_proj_kernel, col_chunk=col_chunk),
        out_shape=(jax.ShapeDtypeStruct((n, p), BF16), jax.ShapeDtypeStruct((8, n), F32)),
        grid=(n // tm,),
        in_specs=[pl.BlockSpec((tm, d), lambda i: (i, 0)),
                  _const_spec((1, d)), _const_spec((d, p)), _const_spec((8, d)), _const_spec((8, 1))],
        out_specs=(pl.BlockSpec((tm, p), lambda i: (i, 0)), pl.BlockSpec((8, tm), lambda i: (0, i))),
        compiler_params=_params("parallel"),
        name="mlstm_proj",
    )(x, gain, w_main, w_gate_t, bias)


def _mlstm_core_kernel(q_ref, k_ref, v_ref, o_ref, gates_ref, gain_ref, out_ref, state_ref, m_ref):
    L = MLSTM_CHUNK
    H, DK, DV = MLSTM_HEADS, MLSTM_DQK, MLSTM_DV
    scale = DK ** -0.5

    @pl.when(pl.program_id(1) == 0)
    def _():
        state_ref[...] = jnp.zeros_like(state_ref)
        m_ref[...] = jnp.zeros_like(m_ref)

    row = lax.broadcasted_iota(jnp.int32, (L, L), 0)
    col = lax.broadcasted_iota(jnp.int32, (L, L), 1)
    causal = col <= row
    upper_incl = (row <= col).astype(F32)
    ones_blk = jnp.ones((L, LANES), BF16)

    hs = range(H)
    qs = [q_ref[:, h * DK:(h + 1) * DK] for h in hs]
    ks = [k_ref[:, h * DK:(h + 1) * DK] for h in hs]
    v_augs = [jnp.concatenate([v_ref[:, h * DV:(h + 1) * DV], ones_blk], axis=1) for h in hs]
    states = [state_ref[h] for h in hs]
    scores = [_dot_nt(qs[h], ks[h]) for h in hs]
    inters = [_dot(qs[h], states[h].astype(BF16)) for h in hs]

    log2_scale = math.log2(scale)
    gate = []
    for h in hs:
        ig = gates_ref[h:h + 1, :] * LOG2E
        fg = gates_ref[H + h:H + h + 1, :]
        logf = (jnp.minimum(fg, 0.0) - jnp.log1p(jnp.exp(-jnp.abs(fg)))) * LOG2E
        bcum = jnp.dot(jnp.broadcast_to(logf, (8, L)), upper_incl,
                       precision=lax.Precision.HIGHEST, preferred_element_type=F32)[0:1, :]
        bcum_s = jnp.broadcast_to(bcum, (L, L))
        bcum_t = bcum_s.T
        m_prev = m_ref[h, 0:1, 0:1]
        dmat = jnp.where(causal, bcum_t - bcum_s + ig, -jnp.inf)
        inter = bcum_t[:, 0:1] + m_prev
        m_t = jnp.maximum(inter, jnp.max(dmat, axis=-1, keepdims=True))
        m_s = m_t - log2_scale
        g_tot = bcum[:, L - 1:L]
        a_end = g_tot - bcum + ig
        m_new = jnp.maximum(g_tot + m_prev, jnp.max(a_end, axis=-1, keepdims=True))
        gate.append(dict(m_t=m_t, inter_w=jnp.exp2(inter - m_s), e=jnp.exp2(dmat - m_s),
                         w_row=jnp.exp2(a_end - m_new), decay=jnp.exp2(g_tot + m_prev - m_new), m_new=m_new))

    ps = [(scores[h] * gate[h]["e"]).astype(BF16) for h in hs]
    ktws = [(ks[h].astype(F32).T * gate[h]["w_row"]).astype(BF16) for h in hs]
    nums = [_dot(ps[h], v_augs[h]) + gate[h]["inter_w"] * inters[h] for h in hs]
    for h in hs:
        state_ref[h] = gate[h]["decay"] * states[h] + _dot(ktws[h], v_augs[h])
        m_ref[h] = jnp.broadcast_to(gate[h]["m_new"], m_ref.shape[1:])
    for h in hs:
        nq = nums[h][:, DV:DV + 1]
        den = jnp.maximum(jnp.abs(nq), jnp.exp2(-gate[h]["m_t"]))
        hval = nums[h][:, :DV] / den
        hn = hval * _rms_scale(hval) * gain_ref[:, h * DV:(h + 1) * DV]
        og = o_ref[:, h * DV:(h + 1) * DV].astype(F32)
        out_ref[:, h * DV:(h + 1) * DV] = (hn * _sigmoid(og)).astype(BF16)


def mlstm_core(proj, gates, out_gain, *, batch, seq):
    n = proj.shape[0]
    L = MLSTM_ROWS
    assert L % MLSTM_CHUNK == 0
    nc = seq // L
    H, DK, DV = MLSTM_HEADS, MLSTM_DQK, MLSTM_DV
    qw, vw = H * DK, H * DV
    assert 2 * qw == vw
    rows = lambda b, c: b * nc + c
    return pl.pallas_call(
        _mlstm_core_kernel,
        out_shape=jax.ShapeDtypeStruct((n, vw), BF16),
        grid=(batch, nc),
        in_specs=[pl.BlockSpec((L, qw), lambda b, c: (rows(b, c), 0)),
                  pl.BlockSpec((L, qw), lambda b, c: (rows(b, c), 1)),
                  pl.BlockSpec((L, vw), lambda b, c: (rows(b, c), 1)),
                  pl.BlockSpec((L, vw), lambda b, c: (rows(b, c), 2)),
                  pl.BlockSpec((8, L), lambda b, c: (0, rows(b, c))),
                  _const_spec((1, vw))],
        out_specs=pl.BlockSpec((L, vw), lambda b, c: (rows(b, c), 0)),
        scratch_shapes=[pltpu.VMEM((H, DK, DV + LANES), F32), pltpu.VMEM((H, 8, LANES), F32)],
        compiler_params=_params("parallel", "arbitrary"),
        name="mlstm_core",
    )(proj, proj, proj, proj, gates, out_gain)


def _silu_mul(g, u):
    return g * _sigmoid(g) * u


def _mix_ffn_kernel(x_ref, h_ref, wo_ref, g_ref, wgu_ref, wd_ref, *rest, f_chunk, n_riders):
    rider_in, out_ref, rider_out = rest[:n_riders], rest[n_riders], rest[n_riders + 1:]
    d_ff = wd_ref.shape[0]
    x1 = x_ref[...] + _dot(h_ref[...], wo_ref[...])
    hn = (x1 * _rms_scale(x1) * g_ref[...]).astype(BF16)
    acc = jnp.zeros_like(x1)
    for c in range(d_ff // f_chunk):
        lo = c * f_chunk
        g = _dot(hn, wgu_ref[:, lo:lo + f_chunk])
        u = _dot(hn, wgu_ref[:, d_ff + lo:d_ff + lo + f_chunk])
        acc = acc + _dot(_silu_mul(g, u).astype(BF16), wd_ref[lo:lo + f_chunk, :])
    out_ref[...] = x1 + acc
    for src, dst in zip(rider_in, rider_out):
        dst[...] = src[...].astype(BF16)


def mix_ffn(x, h, w_out, gain, w_gu, w_down, riders=(), *, tm=512, f_chunk=256):
    n, d = x.shape
    d_ff = w_down.shape[0]
    steps = n // tm
    assert d_ff % f_chunk == 0
    slabs = []
    for w in riders:
        rows = w.size // (steps * w.shape[-1])
        assert rows * steps * w.shape[-1] == w.size and rows % 16 == 0 and w.shape[-2] % rows == 0
        slabs.append(w.reshape(steps, rows, w.shape[-1]))
    slab_spec = lambda s: pl.BlockSpec((1,) + s.shape[1:], lambda i: (i, 0, 0))
    outs = pl.pallas_call(
        partial(_mix_ffn_kernel, f_chunk=f_chunk, n_riders=len(slabs)),
        out_shape=(jax.ShapeDtypeStruct((n, d), F32),) + tuple(jax.ShapeDtypeStruct(s.shape, BF16) for s in slabs),
        grid=(steps,),
        in_specs=[pl.BlockSpec((tm, d), lambda i: (i, 0)), pl.BlockSpec((tm, h.shape[1]), lambda i: (i, 0)),
                  _const_spec(w_out.shape), _const_spec((1, d)), _const_spec(w_gu.shape),
                  _const_spec(w_down.shape)] + [slab_spec(s) for s in slabs],
        out_specs=(pl.BlockSpec((tm, d), lambda i: (i, 0)),) + tuple(slab_spec(s) for s in slabs),
        compiler_params=_params("parallel"),
        name="mix_ffn",
    )(x, h, w_out, gain, w_gu, w_down, *slabs)
    return (outs[0],) + tuple(o.reshape(w.shape) for o, w in zip(outs[1:], riders))


def _rope(t, cos, sin_lo, sin_hi):
    half = ROT_DIM // 2
    return t * cos + pltpu.roll(t, LANES - half, 1) * sin_lo + pltpu.roll(t, half, 1) * sin_hi


def _kvq_kernel(x_ref, gkv_ref, gq_ref, wkv_ref, wq_ref, cos_ref, slo_ref, shi_ref,
                q_ref, k_ref, v_ref, km_ref, *, blocks_per_seq):
    tm = x_ref.shape[0]
    nsub = tm // MOBA_BLOCK
    G, DH = ATT_KV_HEADS, HEAD_DIM
    x = x_ref[...]
    xs = x * _rms_scale(x)
    cos, slo, shi = cos_ref[...], slo_ref[...], shi_ref[...]

    kv = _dot((xs * gkv_ref[...]).astype(BF16), wkv_ref[...])
    for g in range(G):
        v_ref[:, 2 * g * DH:(2 * g + 1) * DH] = kv[:, (G + g) * DH:(G + g + 1) * DH].astype(BF16)
        v_ref[:, (2 * g + 1) * DH:(2 * g + 2) * DH] = jnp.ones((tm, DH), BF16)
    first_blk = (pl.program_id(0) * nsub) % blocks_per_seq
    blk = first_blk + lax.broadcasted_iota(jnp.int32, (tm, LANES), 0) // MOBA_BLOCK
    ind = (lax.broadcasted_iota(jnp.int32, (tm, LANES), 1) == blk).astype(BF16)
    for g in range(G):
        kg = _rope(kv[:, g * DH:(g + 1) * DH], cos, slo, shi)
        k_ref[:, 2 * g * DH:(2 * g + 1) * DH] = kg.astype(BF16)
        k_ref[:, (2 * g + 1) * DH:(2 * g + 2) * DH] = ind
        for s in range(nsub):
            km_ref[0, s:s + 1, g * DH:(g + 1) * DH] = jnp.mean(
                kg[s * MOBA_BLOCK:(s + 1) * MOBA_BLOCK, :], axis=0, keepdims=True)

    q = _dot((xs * gq_ref[...]).astype(BF16), wq_ref[...])
    for h in range(ATT_HEADS):
        q_ref[:, h * DH:(h + 1) * DH] = _rope(q[:, h * DH:(h + 1) * DH], cos, slo, shi).astype(BF16)


def kvq_proj(x, g_kv, g_q, w_kv, w_q, cos, sin_lo, sin_hi, *, seq, tm=512):
    n, d = x.shape
    G, DH = ATT_KV_HEADS, HEAD_DIM
    nsub = tm // MOBA_BLOCK
    tiles_per_seq = seq // tm
    tab = pl.BlockSpec((tm, LANES), lambda i: (i % tiles_per_seq, 0))
    return pl.pallas_call(
        partial(_kvq_kernel, blocks_per_seq=seq // MOBA_BLOCK),
        out_shape=(jax.ShapeDtypeStruct((n, ATT_HEADS * DH), BF16),
                   jax.ShapeDtypeStruct((n, 2 * G * DH), BF16),
                   jax.ShapeDtypeStruct((n, 2 * G * DH), BF16),
                   jax.ShapeDtypeStruct((n // tm, nsub, G * DH), F32)),
        grid=(n // tm,),
        in_specs=[pl.BlockSpec((tm, d), lambda i: (i, 0)), _const_spec((1, d)), _const_spec((1, d)),
                  _const_spec(w_kv.shape), _const_spec(w_q.shape), tab, tab, tab],
        out_specs=(pl.BlockSpec((tm, ATT_HEADS * DH), lambda i: (i, 0)),
                   pl.BlockSpec((tm, 2 * G * DH), lambda i: (i, 0)),
                   pl.BlockSpec((tm, 2 * G * DH), lambda i: (i, 0)),
                   pl.BlockSpec((1, nsub, G * DH), lambda i: (i, 0, 0))),
        compiler_params=_params("parallel"),
        name="kvq_proj",
    )(x, g_kv, g_q, w_kv, w_q, cos, sin_lo, sin_hi)


def _moba_kernel(q_ref, k_ref, v_ref, km_ref, out_ref, m_ref, acc_ref):
    QB, DH = MOBA_BLOCK, HEAD_DIM
    hpg = ATT_HEADS // ATT_KV_HEADS
    R = hpg * QB
    scale = DH ** -0.5
    j = pl.program_id(2)

    q = jnp.concatenate([q_ref[:, h * DH:(h + 1) * DH] for h in range(hpg)], axis=0)

    nb = km_ref.shape[0]
    km_hi, km_lo = _split_bf16(km_ref[...])
    g2 = _dot_nt(jnp.concatenate([km_hi, km_lo], axis=0), q)
    gate = g2[:nb] + g2[nb:]
    blk = lax.broadcasted_iota(jnp.int32, (nb, R), 0)
    beats = jnp.zeros((nb, R), jnp.int32)
    for mb in range(nb):
        gm = gate[mb:mb + 1, :]
        ahead = (gm > gate) | ((gm == gate) & (mb < blk))
        beats = beats + jnp.where(ahead, 1, 0) * (mb < j).astype(jnp.int32)
    selected = jnp.where((blk < j) & (beats < MOBA_TOPK), 1.0, 0.0).astype(BF16)
    eye = jnp.where(lax.broadcasted_iota(jnp.int32, (nb, LANES), 0)
                    == lax.broadcasted_iota(jnp.int32, (nb, LANES), 1), 1.0, 0.0).astype(BF16)
    sel_rows = lax.dot_general(selected, eye, (((0,), (0,)), ((), ())), preferred_element_type=F32)
    pen = ((1.0 - sel_rows) * NEG_BIG).astype(BF16)
    q_aug = jnp.concatenate([q, pen], axis=1)

    exp_scale = scale * LOG2E
    wide = lambda a: jnp.concatenate([a] * (QB // LANES), axis=1)

    k_own = k_ref[pl.ds(pl.multiple_of(j * QB, QB), QB), :]
    v_own = v_ref[pl.ds(pl.multiple_of(j * QB, QB), QB), :]
    s = _dot_nt(q, k_own[:, :DH])
    t_pos = lax.broadcasted_iota(jnp.int32, (R, QB), 0) % QB
    s_pos = lax.broadcasted_iota(jnp.int32, (R, QB), 1)
    s = jnp.where(s_pos <= t_pos, s, -jnp.inf)
    m0 = jnp.max(s, axis=-1, keepdims=True)
    p = jnp.exp2((s - m0) * exp_scale)
    m_ref[...] = jnp.broadcast_to(m0, m_ref.shape)
    acc_ref[...] = _dot(p.astype(BF16), v_own)

    def body(n, carry):
        start = pl.multiple_of(n * QB, QB)
        k_n = k_ref[pl.ds(start, QB), :]
        v_n = v_ref[pl.ds(start, QB), :]
        s = _dot_nt(q_aug, k_n)
        m_old = m_ref[...]
        m_new = jnp.maximum(m_old, jnp.max(s, axis=-1, keepdims=True))
        alpha = jnp.exp2((m_old - m_new) * exp_scale)
        p = jnp.exp2((s - wide(m_new)) * exp_scale)
        acc_ref[...] = wide(alpha) * acc_ref[...] + _dot(p.astype(BF16), v_n)
        m_ref[...] = m_new
        return carry

    lax.fori_loop(0, j, body, 0)

    acc = acc_ref[...]
    out = acc[:, :DH] / acc[:, DH:]
    for h in range(hpg):
        out_ref[:, h * DH:(h + 1) * DH] = out[h * QB:(h + 1) * QB, :].astype(BF16)


def moba_attn(q, k_aug, v, k_means, *, batch, seq):
    n = q.shape[0]
    G, DH, QB = ATT_KV_HEADS, HEAD_DIM, MOBA_BLOCK
    hpg = ATT_HEADS // G
    nb = seq // QB
    assert DH == LANES and nb <= LANES
    R = hpg * QB
    return pl.pallas_call(
        _moba_kernel,
        out_shape=jax.ShapeDtypeStruct((n, ATT_HEADS * DH), BF16),
        grid=(batch, G, nb),
        in_specs=[pl.BlockSpec((QB, hpg * DH), lambda b, g, j: (b * nb + j, g)),
                  pl.BlockSpec((seq, 2 * DH), lambda b, g, j: (b, g)),
                  pl.BlockSpec((seq, 2 * DH), lambda b, g, j: (b, g)),
                  pl.BlockSpec((nb, DH), lambda b, g, j: (b, g))],
        out_specs=pl.BlockSpec((QB, hpg * DH), lambda b, g, j: (b * nb + j, g)),
        scratch_shapes=[pltpu.VMEM((R, LANES), F32), pltpu.VMEM((R, 2 * DH), F32)],
        compiler_params=_params("parallel", "parallel", "arbitrary"),
        name="moba_attn",
    )(q, k_aug, v, k_means)


def _oproj_route_kernel(x_ref, a_ref, wo_ref, g_ref, r_ref, x_out_ref, hn_ref, sel_ref, comb_ref):
    x3 = x_ref[...] + _dot(a_ref[...], wo_ref[...])
    x_out_ref[...] = x3
    hn = x3 * _rms_scale(x3) * g_ref[...]
    hn_hi, hn_lo = _split_bf16(hn)
    hn_ref[...] = hn_hi
    logits = _dot_nt_split(r_ref[...], hn_hi, hn_lo)
    ne = logits.shape[0]
    idx = lax.broadcasted_iota(jnp.int32, logits.shape, 0)
    m1 = jnp.max(logits, axis=0, keepdims=True)
    i1 = jnp.min(jnp.where(logits == m1, idx, ne), axis=0, keepdims=True)
    first = idx == i1
    rest = jnp.where(first, -jnp.inf, logits)
    m2 = jnp.max(rest, axis=0, keepdims=True)
    i2 = jnp.min(jnp.where(rest == m2, idx, ne), axis=0, keepdims=True)
    second = idx == i2
    e = jnp.exp(m2 - m1)
    sel_ref[...] = jnp.where(first | second, 1.0, 0.0)
    comb_ref[...] = jnp.where(first, 1.0 / (1.0 + e), 0.0) + jnp.where(second, e / (1.0 + e), 0.0)


def oproj_route(x, att, w_o, gain, router_t, *, tm=512):
    n, d = x.shape
    ne = router_t.shape[0]
    row = lambda w: pl.BlockSpec((tm, w), lambda i: (i, 0))
    lane_major = pl.BlockSpec((ne, tm), lambda i: (0, i))
    return pl.pallas_call(
        _oproj_route_kernel,
        out_shape=(jax.ShapeDtypeStruct((n, d), F32), jax.ShapeDtypeStruct((n, d), BF16),
                   jax.ShapeDtypeStruct((ne, n), F32), jax.ShapeDtypeStruct((ne, n), F32)),
        grid=(n // tm,),
        in_specs=[row(d), row(att.shape[1]), _const_spec(w_o.shape), _const_spec((1, d)),
                  _const_spec(router_t.shape)],
        out_specs=(row(d), row(d), lane_major, lane_major),
        compiler_params=_params("parallel"),
        name="oproj_route",
    )(x, att, w_o, gain, router_t)


TOK_BLOCK = 256
SEG_ALIGN = 8
MOE_TM = 512


def _loc_rows(ne):
    need = 2 * TOK_BLOCK + ne * (SEG_ALIGN - 1)
    return -(-need // LANES) * LANES


def _dispatch_plan(sel_t, n_tiles):
    ne, n = sel_t.shape
    nblk = n // TOK_BLOCK
    i32 = jnp.int32
    cnt = sel_t.reshape(ne, nblk, TOK_BLOCK).sum(-1).astype(i32)
    pad = (cnt + SEG_ALIGN - 1) // SEG_ALIGN * SEG_ALIGN
    loff = jnp.cumsum(pad, axis=0) - pad
    tot = pad.sum(axis=1)
    reg = (tot + MOE_TM - 1) // MOE_TM * MOE_TM
    reg_start = jnp.cumsum(reg) - reg
    goff = reg_start[:, None] + jnp.cumsum(pad, axis=1) - pad
    tiles_end = jnp.cumsum(reg // MOE_TM)
    n_valid = tiles_end[-1]
    t = jnp.arange(n_tiles, dtype=i32)
    t_eff = jnp.minimum(t, n_valid - 1)
    tile_expert = (tiles_end[None, :] <= t_eff[:, None]).sum(axis=1).astype(i32)
    tile_valid = (t < n_valid).astype(i32)
    used = reg.sum()
    tail = jnp.concatenate([reg_start + tot, used[None], (reg - tot) // SEG_ALIGN,
                            ((n_tiles * MOE_TM - used) // SEG_ALIGN)[None]]).astype(i32)
    flat = lambda a: a.reshape(-1).astype(i32)
    return flat(pad), flat(loff), flat(goff), tail, tile_expert, t_eff.astype(i32), tile_valid


def _block_rows(pad_ref, blk, nblk, ne):
    tot = pad_ref[blk]
    for e in range(1, ne):
        tot = tot + pad_ref[e * nblk + blk]
    return tot


def _segment_copies(pad_ref, loff_ref, goff_ref, blk, nblk, ne, start_one):
    for e in range(ne):
        n_chunks = lax.shift_right_logical(pad_ref[e * nblk + blk], 3)
        lo, go = loff_ref[e * nblk + blk], goff_ref[e * nblk + blk]

        def body(i, carry, lo=lo, go=go):
            start_one(pl.multiple_of(lo + i * SEG_ALIGN, SEG_ALIGN), pl.multiple_of(go + i * SEG_ALIGN, SEG_ALIGN))
            return carry

        lax.fori_loop(0, n_chunks, body, 0)


def _local_positions(sel, loff_ref, blk, nblk):
    ne, tb = sel.shape
    sub = lax.broadcasted_iota(jnp.int32, (ne, tb), 0)
    row = lax.broadcasted_iota(jnp.int32, (tb, tb), 0)
    col = lax.broadcasted_iota(jnp.int32, (tb, tb), 1)
    before = (row < col).astype(BF16)
    rank = _dot(sel.astype(BF16), before)
    sub1 = lax.broadcasted_iota(jnp.int32, (ne, 1), 0)
    loff = jnp.zeros((ne, 1), F32)
    for e in range(ne):
        loff = jnp.where(sub1 == e, loff_ref[e * nblk + blk].astype(F32), loff)
    chosen = sel > 0.0
    e_lo = jnp.min(jnp.where(chosen, sub, ne), axis=0, keepdims=True)
    e_hi = jnp.max(jnp.where(chosen, sub, -1), axis=0, keepdims=True)
    return loff + rank, sub == e_lo, sub == e_hi


def _pick(a, mask):
    return jnp.sum(jnp.where(mask, a, 0.0), axis=0, keepdims=True)


def _dispatch_kernel(pad_ref, loff_ref, goff_ref, tail_ref, hn_ref, sel_ref, xs_ref, buf_ref, zero_ref,
                     sem, tail_sem, *, nblk, ne):
    b = pl.program_id(0)
    slot = b % 2
    rows = buf_ref.shape[1]

    def wait_block(blk, s):
        tot = pl.multiple_of(_block_rows(pad_ref, blk, nblk, ne), SEG_ALIGN)
        pltpu.make_async_copy(buf_ref.at[s, pl.ds(0, tot)], xs_ref.at[pl.ds(0, tot)], sem.at[s]).wait()

    @pl.when(b >= 2)
    def _():
        wait_block(b - 2, slot)

    lpos, lo_mask, hi_mask = _local_positions(sel_ref[...], loff_ref, b, nblk)
    p1 = _pick(lpos, lo_mask).astype(jnp.int32)
    p2 = _pick(lpos, hi_mask).astype(jnp.int32)
    r = lax.broadcasted_iota(jnp.int32, (rows, TOK_BLOCK), 0)
    perm = jnp.where((r == p1) | (r == p2), 1.0, 0.0).astype(BF16)
    buf_ref[slot] = _dot(perm, hn_ref[...])

    def start_one(lrow, grow):
        pltpu.make_async_copy(buf_ref.at[slot, pl.ds(lrow, SEG_ALIGN)], xs_ref.at[pl.ds(grow, SEG_ALIGN)],
                              sem.at[slot]).start()

    _segment_copies(pad_ref, loff_ref, goff_ref, b, nblk, ne, start_one)

    @pl.when(b == nblk - 1)
    def _():
        zero_ref[...] = jnp.zeros_like(zero_ref)
        n_span = ne + 1
        for sp in range(n_span):
            def start_zero(i, carry, sp=sp):
                grow = pl.multiple_of(tail_ref[sp] + i * SEG_ALIGN, SEG_ALIGN)
                pltpu.make_async_copy(zero_ref, xs_ref.at[pl.ds(grow, SEG_ALIGN)], tail_sem).start()
                return carry

            lax.fori_loop(0, tail_ref[n_span + sp], start_zero, 0)

        for sp in range(n_span):
            def wait_zero(i, carry):
                pltpu.make_async_copy(zero_ref, xs_ref.at[pl.ds(0, SEG_ALIGN)], tail_sem).wait()
                return carry

            lax.fori_loop(0, tail_ref[n_span + sp], wait_zero, 0)

        wait_block(b, slot)
        if nblk > 1:
            wait_block(b - 1, 1 - slot)


def moe_dispatch(plan, hn, sel_t, *, n_tiles):
    pad, loff, goff, tail = plan[:4]
    n, d = hn.shape
    ne = sel_t.shape[0]
    nblk = n // TOK_BLOCK
    rows = _loc_rows(ne)
    grid_spec = pltpu.PrefetchScalarGridSpec(
        num_scalar_prefetch=4,
        grid=(nblk,),
        in_specs=[pl.BlockSpec((TOK_BLOCK, d), lambda b, *_: (b, 0)),
                  pl.BlockSpec((ne, TOK_BLOCK), lambda b, *_: (0, b))],
        out_specs=pl.BlockSpec(memory_space=pltpu.HBM),
        scratch_shapes=[pltpu.VMEM((2, rows, d), F32), pltpu.VMEM((SEG_ALIGN, d), F32),
                        pltpu.SemaphoreType.DMA((2,)), pltpu.SemaphoreType.DMA(())],
    )
    return pl.pallas_call(
        partial(_dispatch_kernel, nblk=nblk, ne=ne),
        out_shape=jax.ShapeDtypeStruct((n_tiles * MOE_TM, d), F32),
        grid_spec=grid_spec,
        compiler_params=_params("arbitrary"),
        name="moe_dispatch",
    )(pad, loff, goff, tail, hn, sel_t)


def _expert_ffn_kernel(texp_ref, teff_ref, tvalid_ref, x_ref, wg_ref, wu_ref, wd_ref, out_ref, xb_ref):
    i, f = pl.program_id(0), pl.program_id(1)

    @pl.when(tvalid_ref[i] == 1)
    def _():
        @pl.when(f == 0)
        def _():
            xb_ref[...] = x_ref[...].astype(BF16)

        xb = xb_ref[...]
        y = _dot(_silu_mul(_dot(xb, wg_ref[0]), _dot(xb, wu_ref[0])).astype(BF16), wd_ref[0])

        @pl.when(f == 0)
        def _():
            out_ref[...] = y

        @pl.when(f > 0)
        def _():
            out_ref[...] += y

    @pl.when((tvalid_ref[i] == 0) & (f == 0))
    def _():
        out_ref[...] = jnp.zeros_like(out_ref)


def expert_ffn(plan, xs, w_gu, w_down, *, f_chunk=1792):
    tile_expert, tile_eff, tile_valid = plan[4:]
    n_rows, d = xs.shape
    d_exp = w_down.shape[1]
    nf = d_exp // f_chunk
    assert nf * f_chunk == d_exp and f_chunk % LANES == 0
    n_tiles = n_rows // MOE_TM
    f_eff = lambda i, f, tv: jnp.where(tv[i] == 1, f, nf - 1)
    grid_spec = pltpu.PrefetchScalarGridSpec(
        num_scalar_prefetch=3,
        grid=(n_tiles, nf),
        in_specs=[pl.BlockSpec((MOE_TM, d), lambda i, f, te, tf, tv: (tf[i], 0)),
                  pl.BlockSpec((1, d, f_chunk), lambda i, f, te, tf, tv: (te[i], 0, f_eff(i, f, tv))),
                  pl.BlockSpec((1, d, f_chunk), lambda i, f, te, tf, tv: (te[i], 0, nf + f_eff(i, f, tv))),
                  pl.BlockSpec((1, f_chunk, d), lambda i, f, te, tf, tv: (te[i], f_eff(i, f, tv), 0))],
        out_specs=pl.BlockSpec((MOE_TM, d), lambda i, f, te, tf, tv: (i, 0)),
        scratch_shapes=[pltpu.VMEM((MOE_TM, d), BF16)],
    )
    return pl.pallas_call(
        _expert_ffn_kernel,
        out_shape=jax.ShapeDtypeStruct((n_rows, d), F32),
        grid_spec=grid_spec,
        compiler_params=_params("arbitrary", "arbitrary"),
        name="expert_ffn",
    )(tile_expert, tile_eff, tile_valid, xs, w_gu, w_gu, w_down)


def _combine_kernel(pad_ref, loff_ref, goff_ref, ys_ref, x_ref, sel_ref, comb_ref, g_ref, out_ref, buf_ref, sem,
                    *, nblk, ne):
    b = pl.program_id(0)
    slot = b % 2
    rows = buf_ref.shape[1]

    def fetch_block(blk, s):
        def start_one(lrow, grow):
            pltpu.make_async_copy(ys_ref.at[pl.ds(grow, SEG_ALIGN)], buf_ref.at[s, pl.ds(lrow, SEG_ALIGN)],
                                  sem.at[s]).start()

        _segment_copies(pad_ref, loff_ref, goff_ref, blk, nblk, ne, start_one)

    @pl.when(b == 0)
    def _():
        buf_ref[...] = jnp.zeros_like(buf_ref)
        fetch_block(0, 0)

    @pl.when(b + 1 < nblk)
    def _():
        fetch_block(b + 1, 1 - slot)

    tot = pl.multiple_of(_block_rows(pad_ref, b, nblk, ne), SEG_ALIGN)
    pltpu.make_async_copy(ys_ref.at[pl.ds(0, tot)], buf_ref.at[slot, pl.ds(0, tot)], sem.at[slot]).wait()

    lpos, lo_mask, hi_mask = _local_positions(sel_ref[...], loff_ref, b, nblk)
    comb = comb_ref[...]
    p1, p2 = _pick(lpos, lo_mask), _pick(lpos, hi_mask)
    w1, w2 = _pick(comb, lo_mask), _pick(comb, hi_mask)
    r = lax.broadcasted_iota(jnp.int32, (rows, TOK_BLOCK), 0)
    w_rows = jnp.sum(jnp.where(r == p1.astype(jnp.int32), w1, 0.0) + jnp.where(r == p2.astype(jnp.int32), w2, 0.0),
                     axis=1, keepdims=True)
    local = (buf_ref[slot] * w_rows).astype(BF16)
    hi1, hi2 = jnp.floor(p1 * (1.0 / 32.0)), jnp.floor(p2 * (1.0 / 32.0))
    info = jnp.concatenate([hi1, p1 - 32.0 * hi1, hi2, p2 - 32.0 * hi2, jnp.zeros((4, TOK_BLOCK), F32)], axis=0)
    eye = jnp.where(lax.broadcasted_iota(jnp.int32, (TOK_BLOCK, TOK_BLOCK), 0)
                    == lax.broadcasted_iota(jnp.int32, (TOK_BLOCK, TOK_BLOCK), 1), 1.0, 0.0).astype(BF16)
    cols = _dot_nt(eye, info.astype(BF16))
    c1 = (32.0 * cols[:, 0:1] + cols[:, 1:2]).astype(jnp.int32)
    c2 = (32.0 * cols[:, 2:3] + cols[:, 3:4]).astype(jnp.int32)
    c = lax.broadcasted_iota(jnp.int32, (TOK_BLOCK, rows), 1)
    unsort = jnp.where((c == c1) | (c == c2), 1.0, 0.0).astype(BF16)
    y = x_ref[...] + _dot(unsort, local)
    out_ref[...] = y * _rms_scale(y) * g_ref[...]


def moe_combine(plan, ys, x, sel_t, comb_t, g_final):
    pad, loff, goff = plan[:3]
    n, d = x.shape
    ne = sel_t.shape[0]
    nblk = n // TOK_BLOCK
    rows = _loc_rows(ne)
    lane_major = pl.BlockSpec((ne, TOK_BLOCK), lambda b, *_: (0, b))
    grid_spec = pltpu.PrefetchScalarGridSpec(
        num_scalar_prefetch=3,
        grid=(nblk,),
        in_specs=[pl.BlockSpec(memory_space=pltpu.HBM),
                  pl.BlockSpec((TOK_BLOCK, d), lambda b, *_: (b, 0)), lane_major, lane_major,
                  pl.BlockSpec((1, d), lambda b, *_: (0, 0))],
        out_specs=pl.BlockSpec((TOK_BLOCK, d), lambda b, *_: (b, 0)),
        scratch_shapes=[pltpu.VMEM((2, rows, d), F32), pltpu.SemaphoreType.DMA((2,))],
    )
    return pl.pallas_call(
        partial(_combine_kernel, nblk=nblk, ne=ne),
        out_shape=jax.ShapeDtypeStruct((n, d), F32),
        grid_spec=grid_spec,
        compiler_params=_params("arbitrary"),
        name="moe_combine",
    )(pad, loff, goff, ys, x, sel_t, comb_t, g_final)


def moe_ffn(hn, sel_t, comb_t, w_gu, w_down, x, g_final):
    n = hn.shape[0]
    ne = sel_t.shape[0]
    nblk = n // TOK_BLOCK
    n_tiles = -(-(2 * n + nblk * ne * (SEG_ALIGN - 1)) // MOE_TM) + ne
    plan = _dispatch_plan(sel_t, n_tiles)
    xs = moe_dispatch(plan, hn, sel_t, n_tiles=n_tiles)
    ys = expert_ffn(plan, xs, w_gu, w_down)
    return moe_combine(plan, ys, x, sel_t, comb_t, g_final)


def _rope_tables(seq):
    half = ROT_DIM // 2
    pos = jnp.arange(seq, dtype=F32)
    inv_freq = ROPE_THETA ** (-jnp.arange(0, ROT_DIM, 2, dtype=F32) / ROT_DIM)
    ang = pos[:, None] * inv_freq[None, :]
    cos, sin = jnp.cos(ang), jnp.sin(ang)
    pad = jnp.zeros((seq, LANES - ROT_DIM), F32)
    zeros = jnp.zeros((seq, half), F32)
    cos_t = jnp.concatenate([cos, cos, pad + 1.0], axis=1)
    sin_lo = jnp.concatenate([-sin, zeros, pad], axis=1)
    sin_hi = jnp.concatenate([zeros, sin, pad], axis=1)
    return cos_t, sin_lo, sin_hi


def kernel(x, mlstm_norm, mlstm_w_in, mlstm_b_if, mlstm_out_norm, mlstm_w_out, ffn_norm, dense_w_gu,
           dense_w_down, moe_router, moe_w_gu, moe_w_down, kv_norm, kv_w, moba_norm, moba_w_q, moba_w_o,
           final_norm):
    batch, seq, d = x.shape
    n = batch * seq
    H = MLSTM_HEADS
    n_main = H * (2 * MLSTM_DQK + 2 * MLSTM_DV)
    xf = x.reshape(n, d)
    row = lambda g: g.reshape(1, -1).astype(F32)

    w_in = mlstm_w_in[0]
    proj, gates = mlstm_proj(xf, row(mlstm_norm[0]), w_in[:, :n_main].astype(BF16),
                             w_in[:, n_main:].T, mlstm_b_if[0].reshape(2 * H, 1))
    hg = mlstm_core(proj, gates, row(mlstm_out_norm[0]), batch=batch, seq=seq)
    x2, moe_gu_bf16, moe_down_bf16 = mix_ffn(xf, hg, mlstm_w_out[0].astype(BF16), row(ffn_norm[0]),
                                             dense_w_gu[0].astype(BF16), dense_w_down[0].astype(BF16),
                                             riders=(moe_w_gu[0], moe_w_down[0]))

    cos_t, sin_lo, sin_hi = _rope_tables(seq)
    q, k_aug, v, k_means = kvq_proj(x2, row(kv_norm), row(moba_norm[0]), kv_w.astype(BF16),
                                    moba_w_q[0].astype(BF16), cos_t, sin_lo, sin_hi, seq=seq)
    k_means = k_means.reshape(n // MOBA_BLOCK, ATT_KV_HEADS * HEAD_DIM)
    att = moba_attn(q, k_aug, v, k_means, batch=batch, seq=seq)
    x3, hn, sel_t, comb_t = oproj_route(x2, att, moba_w_o[0].astype(BF16), row(ffn_norm[1]), moe_router[0].T)
    out = moe_ffn(hn, sel_t, comb_t, moe_gu_bf16, moe_down_bf16, x3, row(final_norm))
    return out.reshape(batch, seq, d)
```

```python
import math
from functools import partial

import jax
import jax.numpy as jnp
from jax import lax
from jax.experimental import pallas as pl
from jax.experimental.pallas import tpu as pltpu

NORM_EPS = 1e-6

MLSTM_HEADS = 4
MLSTM_DQK = 128
MLSTM_DV = 256
MLSTM_CHUNK = 256
MLSTM_ROWS = 256

ATT_HEADS = 8
ATT_KV_HEADS = 2
HEAD_DIM = 128
MOBA_BLOCK = 256
MOBA_TOPK = 3
ROW_GROUPS = 4
ROPE_THETA = 500000.0
ROT_DIM = HEAD_DIM // 4

N_EXPERTS = 8

LANES = 128
NEG_BIG = -1e30
LOG2E = 1.4426950408889634
VMEM_LIMIT = 56 * 1024 * 1024

F32 = jnp.float32
BF16 = jnp.bfloat16

_NT = (((1,), (1,)), ((), ()))


def _dot(a, b):
    return jnp.dot(a, b, preferred_element_type=F32)


def _dot_nt(a, b, precision=None):
    return lax.dot_general(a, b, _NT, precision=precision, preferred_element_type=F32)


def _split_bf16(a):
    hi = a.astype(BF16)
    return hi, (a - hi.astype(F32)).astype(BF16)


def _dot_nt_split(w, x_hi, x_lo):
    r = w.shape[0]
    w_hi, w_lo = _split_bf16(w)
    a = _dot_nt(jnp.concatenate([w_hi, w_lo], axis=0), x_hi)
    return a[:r] + a[r:] + _dot_nt(w_hi, x_lo)


def _rms_scale(x):
    return lax.rsqrt(jnp.mean(x * x, axis=-1, keepdims=True) + NORM_EPS)


def _sigmoid(x):
    return 1.0 / (1.0 + jnp.exp(-x))


def _params(*sem):
    return pltpu.CompilerParams(dimension_semantics=sem, vmem_limit_bytes=VMEM_LIMIT)


def _const_spec(shape):
    nd = len(shape)
    return pl.BlockSpec(shape, lambda *_: (0,) * nd, pipeline_mode=pl.Buffered(1))


def _mlstm_proj_kernel(x_ref, g_ref, w_ref, wg_ref, b_ref, proj_ref, gates_ref, *, col_chunk):
    x = x_ref[...]
    xn = x * _rms_scale(x) * g_ref[...]
    xb, x_lo = _split_bf16(xn)
    for c in range(w_ref.shape[1] // col_chunk):
        sl = slice(c * col_chunk, (c + 1) * col_chunk)
        proj_ref[:, sl] = _dot(xb, w_ref[:, sl]).astype(BF16)
    gates_ref[...] = _dot_nt_split(wg_ref[...], xb, x_lo) + b_ref[...]


def mlstm_proj(x, gain, w_main, w_gate_t, bias, *, tm=512, col_chunk=512):
    n, d = x.shape
    p = w_main.shape[1]
    return pl.pallas_call(
        partial(_mlstm_proj_kernel, col_chunk=col_chunk),
        out_shape=(jax.ShapeDtypeStruct((n, p), BF16), jax.ShapeDtypeStruct((8, n), F32)),
        grid=(n // tm,),
        in_specs=[pl.BlockSpec((tm, d), lambda i: (i, 0)),
                  _const_spec((1, d)), _const_spec((d, p)), _const_spec((8, d)), _const_spec((8, 1))],
        out_specs=(pl.BlockSpec((tm, p), lambda i: (i, 0)), pl.BlockSpec((8, tm), lambda i: (0, i))),
        compiler_params=_params("parallel"),
        name="mlstm_proj",
    )(x, gain, w_main, w_gate_t, bias)


def _mlstm_core_kernel(q_ref, k_ref, v_ref, o_ref, gates_ref, gain_ref, out_ref, state_ref, m_ref):
    L = MLSTM_CHUNK
    H, DK, DV = MLSTM_HEADS, MLSTM_DQK, MLSTM_DV
    scale = DK ** -0.5

    @pl.when(pl.program_id(1) == 0)
    def _():
        state_ref[...] = jnp.zeros_like(state_ref)
        m_ref[...] = jnp.zeros_like(m_ref)

    row = lax.broadcasted_iota(jnp.int32, (L, L), 0)
    col = lax.broadcasted_iota(jnp.int32, (L, L), 1)
    causal = col <= row
    upper_incl = (row <= col).astype(F32)
    ones_blk = jnp.ones((L, LANES), BF16)

    hs = range(H)
    qs = [q_ref[:, h * DK:(h + 1) * DK] for h in hs]
    ks = [k_ref[:, h * DK:(h + 1) * DK] for h in hs]
    v_augs = [jnp.concatenate([v_ref[:, h * DV:(h + 1) * DV], ones_blk], axis=1) for h in hs]
    states = [state_ref[h] for h in hs]
    scores = [_dot_nt(qs[h], ks[h]) for h in hs]
    inters = [_dot(qs[h], states[h].astype(BF16)) for h in hs]

    log2_scale = math.log2(scale)
    gate = []
    for h in hs:
        ig = gates_ref[h:h + 1, :] * LOG2E
        fg = gates_ref[H + h:H + h + 1, :]
        logf = (jnp.minimum(fg, 0.0) - jnp.log1p(jnp.exp(-jnp.abs(fg)))) * LOG2E
        bcum = jnp.dot(jnp.broadcast_to(logf, (8, L)), upper_incl,
                       precision=lax.Precision.HIGHEST, preferred_element_type=F32)[0:1, :]
        bcum_s = jnp.broadcast_to(bcum, (L, L))
        bcum_t = bcum_s.T
        m_prev = m_ref[h, 0:1, 0:1]
        dmat = jnp.where(causal, bcum_t - bcum_s + ig, -jnp.inf)
        inter = bcum_t[:, 0:1] + m_prev
        m_t = jnp.maximum(inter, jnp.max(dmat, axis=-1, keepdims=True))
        m_s = m_t - log2_scale
        g_tot = bcum[:, L - 1:L]
        a_end = g_tot - bcum + ig
        m_new = jnp.maximum(g_tot + m_prev, jnp.max(a_end, axis=-1, keepdims=True))
        gate.append(dict(m_t=m_t, inter_w=jnp.exp2(inter - m_s), e=jnp.exp2(dmat - m_s),
                         w_row=jnp.exp2(a_end - m_new), decay=jnp.exp2(g_tot + m_prev - m_new), m_new=m_new))

    ps = [(scores[h] * gate[h]["e"]).astype(BF16) for h in hs]
    ktws = [(ks[h].astype(F32).T * gate[h]["w_row"]).astype(BF16) for h in hs]
    nums = [_dot(ps[h], v_augs[h]) + gate[h]["inter_w"] * inters[h] for h in hs]
    for h in hs:
        state_ref[h] = gate[h]["decay"] * states[h] + _dot(ktws[h], v_augs[h])
        m_ref[h] = jnp.broadcast_to(gate[h]["m_new"], m_ref.shape[1:])
    for h in hs:
        nq = nums[h][:, DV:DV + 1]
        den = jnp.maximum(jnp.abs(nq), jnp.exp2(-gate[h]["m_t"]))
        hval = nums[h][:, :DV] / den
        hn = hval * _rms_scale(hval) * gain_ref[:, h * DV:(h + 1) * DV]
        og = o_ref[:, h * DV:(h + 1) * DV].astype(F32)
        out_ref[:, h * DV:(h + 1) * DV] = (hn * _sigmoid(og)).astype(BF16)


def mlstm_core(proj, gates, out_gain, *, batch, seq):
    n = proj.shape[0]
    L = MLSTM_ROWS
    assert L % MLSTM_CHUNK == 0
    nc = seq // L
    H, DK, DV = MLSTM_HEADS, MLSTM_DQK, MLSTM_DV
    qw, vw = H * DK, H * DV
    assert 2 * qw == vw
    rows = lambda b, c: b * nc + c
    return pl.pallas_call(
        _mlstm_core_kernel,
        out_shape=jax.ShapeDtypeStruct((n, vw), BF16),
        grid=(batch, nc),
        in_specs=[pl.BlockSpec((L, qw), lambda b, c: (rows(b, c), 0)),
                  pl.BlockSpec((L, qw), lambda b, c: (rows(b, c), 1)),
                  pl.BlockSpec((L, vw), lambda b, c: (rows(b, c), 1)),
                  pl.BlockSpec((L, vw), lambda b, c: (rows(b, c), 2)),
                  pl.BlockSpec((8, L), lambda b, c: (0, rows(b, c))),
                  _const_spec((1, vw))],
        out_specs=pl.BlockSpec((L, vw), lambda b, c: (rows(b, c), 0)),
        scratch_shapes=[pltpu.VMEM((H, DK, DV + LANES), F32), pltpu.VMEM((H, 8, LANES), F32)],
        compiler_params=_params("parallel", "arbitrary"),
        name="mlstm_core",
    )(proj, proj, proj, proj, gates, out_gain)


def _silu_mul(g, u):
    return g * _sigmoid(g) * u


def _mix_ffn_kernel(x_ref, h_ref, wo_ref, g_ref, wgu_ref, wd_ref, *rest, f_chunk, n_riders):
    rider_in, out_ref, rider_out = rest[:n_riders], rest[n_riders], rest[n_riders + 1:]
    d_ff = wd_ref.shape[0]
    x1 = x_ref[...] + _dot(h_ref[...], wo_ref[...])
    hn = (x1 * _rms_scale(x1) * g_ref[...]).astype(BF16)
    acc = jnp.zeros_like(x1)
    for c in range(d_ff // f_chunk):
        lo = c * f_chunk
        g = _dot(hn, wgu_ref[:, lo:lo + f_chunk])
        u = _dot(hn, wgu_ref[:, d_ff + lo:d_ff + lo + f_chunk])
        acc = acc + _dot(_silu_mul(g, u).astype(BF16), wd_ref[lo:lo + f_chunk, :])
    out_ref[...] = x1 + acc
    for src, dst in zip(rider_in, rider_out):
        dst[...] = src[...].astype(BF16)


def mix_ffn(x, h, w_out, gain, w_gu, w_down, riders=(), *, tm=512, f_chunk=256):
    n, d = x.shape
    d_ff = w_down.shape[0]
    steps = n // tm
    assert d_ff % f_chunk == 0
    slabs = []
    for w in riders:
        rows = w.size // (steps * w.shape[-1])
        assert rows * steps * w.shape[-1] == w.size and rows % 16 == 0 and w.shape[-2] % rows == 0
        slabs.append(w.reshape(steps, rows, w.shape[-1]))
    slab_spec = lambda s: pl.BlockSpec((1,) + s.shape[1:], lambda i: (i, 0, 0))
    outs = pl.pallas_call(
        partial(_mix_ffn_kernel, f_chunk=f_chunk, n_riders=len(slabs)),
        out_shape=(jax.ShapeDtypeStruct((n, d), F32),) + tuple(jax.ShapeDtypeStruct(s.shape, BF16) for s in slabs),
        grid=(steps,),
        in_specs=[pl.BlockSpec((tm, d), lambda i: (i, 0)), pl.BlockSpec((tm, h.shape[1]), lambda i: (i, 0)),
                  _const_spec(w_out.shape), _const_spec((1, d)), _const_spec(w_gu.shape),
                  _const_spec(w_down.shape)] + [slab_spec(s) for s in slabs],
        out_specs=(pl.BlockSpec((tm, d), lambda i: (i, 0)),) + tuple(slab_spec(s) for s in slabs),
        compiler_params=_params("parallel"),
        name="mix_ffn",
    )(x, h, w_out, gain, w_gu, w_down, *slabs)
    return (outs[0],) + tuple(o.reshape(w.shape) for o, w in zip(outs[1:], riders))


def _rope(t, cos, sin_lo, sin_hi):
    half = ROT_DIM // 2
    return t * cos + pltpu.roll(t, LANES - half, 1) * sin_lo + pltpu.roll(t, half, 1) * sin_hi


def _kvq_kernel(x_ref, gkv_ref, gq_ref, wkv_ref, wq_ref, cos_ref, slo_ref, shi_ref,
                q_ref, k_ref, v_ref, km_ref, *, blocks_per_seq):
    tm = x_ref.shape[0]
    nsub = tm // MOBA_BLOCK
    G, DH = ATT_KV_HEADS, HEAD_DIM
    x = x_ref[...]
    xs = x * _rms_scale(x)
    cos, slo, shi = cos_ref[...], slo_ref[...], shi_ref[...]

    kv = _dot((xs * gkv_ref[...]).astype(BF16), wkv_ref[...])
    for g in range(G):
        v_ref[:, 2 * g * DH:(2 * g + 1) * DH] = kv[:, (G + g) * DH:(G + g + 1) * DH].astype(BF16)
        v_ref[:, (2 * g + 1) * DH:(2 * g + 2) * DH] = jnp.ones((tm, DH), BF16)
    first_blk = (pl.program_id(0) * nsub) % blocks_per_seq
    blk = first_blk + lax.broadcasted_iota(jnp.int32, (tm, LANES), 0) // MOBA_BLOCK
    ind = (lax.broadcasted_iota(jnp.int32, (tm, LANES), 1) == blk).astype(BF16)
    for g in range(G):
        kg = _rope(kv[:, g * DH:(g + 1) * DH], cos, slo, shi)
        k_ref[:, 2 * g * DH:(2 * g + 1) * DH] = kg.astype(BF16)
        k_ref[:, (2 * g + 1) * DH:(2 * g + 2) * DH] = ind
        for s in range(nsub):
            km_ref[0, s:s + 1, g * DH:(g + 1) * DH] = jnp.mean(
                kg[s * MOBA_BLOCK:(s + 1) * MOBA_BLOCK, :], axis=0, keepdims=True)

    q = _dot((xs * gq_ref[...]).astype(BF16), wq_ref[...])
    for h in range(ATT_HEADS):
        q_ref[:, h * DH:(h + 1) * DH] = _rope(q[:, h * DH:(h + 1) * DH], cos, slo, shi).astype(BF16)


def kvq_proj(x, g_kv, g_q, w_kv, w_q, cos, sin_lo, sin_hi, *, seq, tm=512):
    n, d = x.shape
    G, DH = ATT_KV_HEADS, HEAD_DIM
    nsub = tm // MOBA_BLOCK
    tiles_per_seq = seq // tm
    tab = pl.BlockSpec((tm, LANES), lambda i: (i % tiles_per_seq, 0))
    return pl.pallas_call(
        partial(_kvq_kernel, blocks_per_seq=seq // MOBA_BLOCK),
        out_shape=(jax.ShapeDtypeStruct((n, ATT_HEADS * DH), BF16),
                   jax.ShapeDtypeStruct((n, 2 * G * DH), BF16),
                   jax.ShapeDtypeStruct((n, 2 * G * DH), BF16),
                   jax.ShapeDtypeStruct((n // tm, nsub, G * DH), F32)),
        grid=(n // tm,),
        in_specs=[pl.BlockSpec((tm, d), lambda i: (i, 0)), _const_spec((1, d)), _const_spec((1, d)),
                  _const_spec(w_kv.shape), _const_spec(w_q.shape), tab, tab, tab],
        out_specs=(pl.BlockSpec((tm, ATT_HEADS * DH), lambda i: (i, 0)),
                   pl.BlockSpec((tm, 2 * G * DH), lambda i: (i, 0)),
                   pl.BlockSpec((tm, 2 * G * DH), lambda i: (i, 0)),
                   pl.BlockSpec((1, nsub, G * DH), lambda i: (i, 0, 0))),
        compiler_params=_params("parallel"),
        name="kvq_proj",
    )(x, g_kv, g_q, w_kv, w_q, cos, sin_lo, sin_hi)


def _moba_kernel(q_ref, k_ref, v_ref, km_ref, out_ref, m_ref, acc_ref):
    QB, DH = MOBA_BLOCK, HEAD_DIM
    hpg = ATT_HEADS // ATT_KV_HEADS
    R = hpg * QB
    scale = DH ** -0.5
    j = pl.program_id(2)

    q = jnp.concatenate([q_ref[:, h * DH:(h + 1) * DH] for h in range(hpg)], axis=0)

    exp_scale = scale * LOG2E
    wide = lambda a: jnp.concatenate([a] * (QB // LANES), axis=1)
    groups = [slice(i * R // ROW_GROUPS, (i + 1) * R // ROW_GROUPS) for i in range(ROW_GROUPS)]

    k_own = k_ref[pl.ds(pl.multiple_of(j * QB, QB), QB), :]
    v_own = v_ref[pl.ds(pl.multiple_of(j * QB, QB), QB), :]
    own_scores = [_dot_nt(q[rs], k_own[:, :DH]) for rs in groups]

    nb = km_ref.shape[0]
    km_hi, km_lo = _split_bf16(km_ref[...])
    g2 = _dot_nt(jnp.concatenate([km_hi, km_lo], axis=0), q)
    gate = g2[:nb] + g2[nb:]
    blk = lax.broadcasted_iota(jnp.int32, (nb, R), 0)
    beats = jnp.zeros((nb, R), jnp.int32)
    for mb in range(nb):
        gm = gate[mb:mb + 1, :]
        ahead = (gm > gate) | ((gm == gate) & (mb < blk))
        beats = beats + jnp.where(ahead, 1, 0) * (mb < j).astype(jnp.int32)
    selected = jnp.where((blk < j) & (beats < MOBA_TOPK), 1.0, 0.0).astype(BF16)
    eye = jnp.where(lax.broadcasted_iota(jnp.int32, (nb, LANES), 0)
                    == lax.broadcasted_iota(jnp.int32, (nb, LANES), 1), 1.0, 0.0).astype(BF16)
    sel_rows = lax.dot_general(selected, eye, (((0,), (0,)), ((), ())), preferred_element_type=F32)
    pen = ((1.0 - sel_rows) * NEG_BIG).astype(BF16)
    q_aug = jnp.concatenate([q, pen], axis=1)

    m0s, p0s = [], []
    for rs, s in zip(groups, own_scores):
        rows = rs.stop - rs.start
        t_pos = (lax.broadcasted_iota(jnp.int32, (rows, QB), 0) + rs.start) % QB
        s_pos = lax.broadcasted_iota(jnp.int32, (rows, QB), 1)
        s = jnp.where(s_pos <= t_pos, s, -jnp.inf)
        m0 = jnp.max(s, axis=-1, keepdims=True)
        m0s.append(m0)
        p0s.append(jnp.exp2((s - m0) * exp_scale).astype(BF16))
    for rs, m0, p0 in zip(groups, m0s, p0s):
        m_ref[rs, :] = jnp.broadcast_to(m0, (rs.stop - rs.start, LANES))
        acc_ref[rs, :] = _dot(p0, v_own)

    def attend(first_blk, n_blk):
        start = pl.multiple_of(first_blk * QB, QB)
        k_n = k_ref[pl.ds(start, n_blk * QB), :]
        v_n = v_ref[pl.ds(start, n_blk * QB), :]
        ss = [_dot_nt(q_aug[rs], k_n) for rs in groups]
        m_olds = [m_ref[rs, :] for rs in groups]
        m_news = [jnp.maximum(mo, jnp.max(s, axis=-1, keepdims=True)) for mo, s in zip(m_olds, ss)]
        ps = [jnp.exp2((s - jnp.concatenate([mn] * (n_blk * QB // LANES), axis=1)) * exp_scale).astype(BF16)
              for s, mn in zip(ss, m_news)]
        pvs = [_dot(p, v_n) for p in ps]
        for rs, mo, mn, pv in zip(groups, m_olds, m_news, pvs):
            alpha = jnp.exp2((mo - mn) * exp_scale)
            acc_ref[rs, :] = wide(alpha) * acc_ref[rs, :] + pv
            m_ref[rs, :] = mn

    def body(i, carry):
        attend(2 * i, 2)
        return carry

    n_pairs = lax.shift_right_logical(j, 1)
    lax.fori_loop(0, n_pairs, body, 0)

    @pl.when(j % 2 == 1)
    def _():
        attend(j - 1, 1)

    acc = acc_ref[...]
    out = acc[:, :DH] / acc[:, DH:]
    for h in range(hpg):
        out_ref[:, h * DH:(h + 1) * DH] = out[h * QB:(h + 1) * QB, :].astype(BF16)


def moba_attn(q, k_aug, v, k_means, *, batch, seq):
    n = q.shape[0]
    G, DH, QB = ATT_KV_HEADS, HEAD_DIM, MOBA_BLOCK
    hpg = ATT_HEADS // G
    nb = seq // QB
    assert DH == LANES and nb <= LANES
    R = hpg * QB
    return pl.pallas_call(
        _moba_kernel,
        out_shape=jax.ShapeDtypeStruct((n, ATT_HEADS * DH), BF16),
        grid=(batch, G, nb),
        in_specs=[pl.BlockSpec((QB, hpg * DH), lambda b, g, j: (b * nb + j, g)),
                  pl.BlockSpec((seq, 2 * DH), lambda b, g, j: (b, g)),
                  pl.BlockSpec((seq, 2 * DH), lambda b, g, j: (b, g)),
                  pl.BlockSpec((nb, DH), lambda b, g, j: (b, g))],
        out_specs=pl.BlockSpec((QB, hpg * DH), lambda b, g, j: (b * nb + j, g)),
        scratch_shapes=[pltpu.VMEM((R, LANES), F32), pltpu.VMEM((R, 2 * DH), F32)],
        compiler_params=_params("parallel", "parallel", "arbitrary"),
        name="moba_attn",
    )(q, k_aug, v, k_means)


def _oproj_route_kernel(x_ref, a_ref, wo_ref, g_ref, r_ref, x_out_ref, hn_ref, sel_ref, comb_ref):
    x3 = x_ref[...] + _dot(a_ref[...], wo_ref[...])
    x_out_ref[...] = x3
    hn = x3 * _rms_scale(x3) * g_ref[...]
    hn_hi, hn_lo = _split_bf16(hn)
    hn_ref[...] = hn_hi
    logits = _dot_nt_split(r_ref[...], hn_hi, hn_lo)
    ne = logits.shape[0]
    idx = lax.broadcasted_iota(jnp.int32, logits.shape, 0)
    m1 = jnp.max(logits, axis=0, keepdims=True)
    i1 = jnp.min(jnp.where(logits == m1, idx, ne), axis=0, keepdims=True)
    first = idx == i1
    rest = jnp.where(first, -jnp.inf, logits)
    m2 = jnp.max(rest, axis=0, keepdims=True)
    i2 = jnp.min(jnp.where(rest == m2, idx, ne), axis=0, keepdims=True)
    second = idx == i2
    e = jnp.exp(m2 - m1)
    sel_ref[...] = jnp.where(first | second, 1.0, 0.0)
    comb_ref[...] = jnp.where(first, 1.0 / (1.0 + e), 0.0) + jnp.where(second, e / (1.0 + e), 0.0)


def oproj_route(x, att, w_o, gain, router_t, *, tm=512):
    n, d = x.shape
    ne = router_t.shape[0]
    row = lambda w: pl.BlockSpec((tm, w), lambda i: (i, 0))
    lane_major = pl.BlockSpec((ne, tm), lambda i: (0, i))
    return pl.pallas_call(
        _oproj_route_kernel,
        out_shape=(jax.ShapeDtypeStruct((n, d), F32), jax.ShapeDtypeStruct((n, d), BF16),
                   jax.ShapeDtypeStruct((ne, n), F32), jax.ShapeDtypeStruct((ne, n), F32)),
        grid=(n // tm,),
        in_specs=[row(d), row(att.shape[1]), _const_spec(w_o.shape), _const_spec((1, d)),
                  _const_spec(router_t.shape)],
        out_specs=(row(d), row(d), lane_major, lane_major),
        compiler_params=_params("parallel"),
        name="oproj_route",
    )(x, att, w_o, gain, router_t)


TOK_BLOCK = 256
SEG_ALIGN = 8
MOE_TM = 512

def _loc_rows(ne):
    need = 2 * TOK_BLOCK + ne * (SEG_ALIGN - 1)
    return -(-need // LANES) * LANES


def _dispatch_plan(sel_t, n_tiles):
    ne, n = sel_t.shape
    nblk = n // TOK_BLOCK
    i32 = jnp.int32
    cnt = sel_t.reshape(ne, nblk, TOK_BLOCK).sum(-1).astype(i32)
    pad = (cnt + SEG_ALIGN - 1) // SEG_ALIGN * SEG_ALIGN
    loff = jnp.cumsum(pad, axis=0) - pad
    tot = pad.sum(axis=1)
    reg = (tot + MOE_TM - 1) // MOE_TM * MOE_TM
    reg_start = jnp.cumsum(reg) - reg
    goff = reg_start[:, None] + jnp.cumsum(pad, axis=1) - pad
    tiles_end = jnp.cumsum(reg // MOE_TM)
    n_valid = tiles_end[-1]
    t = jnp.arange(n_tiles, dtype=i32)
    t_eff = jnp.minimum(t, n_valid - 1)
    tile_expert = (tiles_end[None, :] <= t_eff[:, None]).sum(axis=1).astype(i32)
    tile_valid = (t < n_valid).astype(i32)
    used = reg.sum()
    tail = jnp.concatenate([reg_start + tot, used[None], (reg - tot) // SEG_ALIGN,
                            ((n_tiles * MOE_TM - used) // SEG_ALIGN)[None]]).astype(i32)
    flat = lambda a: a.reshape(-1).astype(i32)
    return flat(pad), flat(loff), flat(goff), tail, tile_expert, t_eff.astype(i32), tile_valid


def _block_rows(pad_ref, blk, nblk, ne):
    tot = pad_ref[blk]
    for e in range(1, ne):
        tot = tot + pad_ref[e * nblk + blk]
    return tot


def _segment_copies(pad_ref, loff_ref, goff_ref, blk, nblk, ne, start_one):
    for e in range(ne):
        n_chunks = lax.shift_right_logical(pad_ref[e * nblk + blk], 3)
        lo, go = loff_ref[e * nblk + blk], goff_ref[e * nblk + blk]

        def body(i, carry, lo=lo, go=go):
            start_one(pl.multiple_of(lo + i * SEG_ALIGN, SEG_ALIGN), pl.multiple_of(go + i * SEG_ALIGN, SEG_ALIGN))
            return carry

        lax.fori_loop(0, n_chunks, body, 0)


def _local_positions(sel, loff_ref, blk, nblk):
    ne, tb = sel.shape
    sub = lax.broadcasted_iota(jnp.int32, (ne, tb), 0)
    row = lax.broadcasted_iota(jnp.int32, (tb, tb), 0)
    col = lax.broadcasted_iota(jnp.int32, (tb, tb), 1)
    before = (row < col).astype(BF16)
    rank = _dot(sel.astype(BF16), before)
    sub1 = lax.broadcasted_iota(jnp.int32, (ne, 1), 0)
    loff = jnp.zeros((ne, 1), F32)
    for e in range(ne):
        loff = jnp.where(sub1 == e, loff_ref[e * nblk + blk].astype(F32), loff)
    chosen = sel > 0.0
    e_lo = jnp.min(jnp.where(chosen, sub, ne), axis=0, keepdims=True)
    e_hi = jnp.max(jnp.where(chosen, sub, -1), axis=0, keepdims=True)
    return loff + rank, sub == e_lo, sub == e_hi


def _pick(a, mask):
    return jnp.sum(jnp.where(mask, a, 0.0), axis=0, keepdims=True)


def _dispatch_kernel(pad_ref, loff_ref, goff_ref, tail_ref, hn_ref, sel_ref, xs_ref, buf_ref, zero_ref,
                     sem, tail_sem, *, nblk, ne):
    b = pl.program_id(0)
    slot = b % 2
    rows = buf_ref.shape[1]

    def wait_block(blk, s):
        tot = pl.multiple_of(_block_rows(pad_ref, blk, nblk, ne), SEG_ALIGN)
        pltpu.make_async_copy(buf_ref.at[s, pl.ds(0, tot)], xs_ref.at[pl.ds(0, tot)], sem.at[s]).wait()

    @pl.when(b >= 2)
    def _():
        wait_block(b - 2, slot)

    lpos, lo_mask, hi_mask = _local_positions(sel_ref[...], loff_ref, b, nblk)
    p1 = _pick(lpos, lo_mask).astype(jnp.int32)
    p2 = _pick(lpos, hi_mask).astype(jnp.int32)
    r = lax.broadcasted_iota(jnp.int32, (rows, TOK_BLOCK), 0)
    perm = jnp.where((r == p1) | (r == p2), 1.0, 0.0).astype(BF16)
    buf_ref[slot] = _dot(perm, hn_ref[...])

    def start_one(lrow, grow):
        pltpu.make_async_copy(buf_ref.at[slot, pl.ds(lrow, SEG_ALIGN)], xs_ref.at[pl.ds(grow, SEG_ALIGN)],
                              sem.at[slot]).start()

    _segment_copies(pad_ref, loff_ref, goff_ref, b, nblk, ne, start_one)

    @pl.when(b == nblk - 1)
    def _():
        zero_ref[...] = jnp.zeros_like(zero_ref)
        n_span = ne + 1
        for sp in range(n_span):
            def start_zero(i, carry, sp=sp):
                grow = pl.multiple_of(tail_ref[sp] + i * SEG_ALIGN, SEG_ALIGN)
                pltpu.make_async_copy(zero_ref, xs_ref.at[pl.ds(grow, SEG_ALIGN)], tail_sem).start()
                return carry

            lax.fori_loop(0, tail_ref[n_span + sp], start_zero, 0)

        for sp in range(n_span):
            def wait_zero(i, carry):
                pltpu.make_async_copy(zero_ref, xs_ref.at[pl.ds(0, SEG_ALIGN)], tail_sem).wait()
                return carry

            lax.fori_loop(0, tail_ref[n_span + sp], wait_zero, 0)

        wait_block(b, slot)
        if nblk > 1:
            wait_block(b - 1, 1 - slot)


def moe_dispatch(plan, hn, sel_t, *, n_tiles):
    pad, loff, goff, tail = plan[:4]
    n, d = hn.shape
    ne = sel_t.shape[0]
    nblk = n // TOK_BLOCK
    rows = _loc_rows(ne)
    grid_spec = pltpu.PrefetchScalarGridSpec(
        num_scalar_prefetch=4,
        grid=(nblk,),
        in_specs=[pl.BlockSpec((TOK_BLOCK, d), lambda b, *_: (b, 0)),
                  pl.BlockSpec((ne, TOK_BLOCK), lambda b, *_: (0, b))],
        out_specs=pl.BlockSpec(memory_space=pltpu.HBM),
        scratch_shapes=[pltpu.VMEM((2, rows, d), F32), pltpu.VMEM((SEG_ALIGN, d), F32),
                        pltpu.SemaphoreType.DMA((2,)), pltpu.SemaphoreType.DMA(())],
    )
    return pl.pallas_call(
        partial(_dispatch_kernel, nblk=nblk, ne=ne),
        out_shape=jax.ShapeDtypeStruct((n_tiles * MOE_TM, d), F32),
        grid_spec=grid_spec,
        compiler_params=_params("arbitrary"),
        name="moe_dispatch",
    )(pad, loff, goff, tail, hn, sel_t)


def _expert_ffn_kernel(texp_ref, teff_ref, tvalid_ref, x_ref, wg_ref, wu_ref, wd_ref, out_ref, xb_ref):
    i, f = pl.program_id(0), pl.program_id(1)

    @pl.when(tvalid_ref[i] == 1)
    def _():
        @pl.when(f == 0)
        def _():
            xb_ref[...] = x_ref[...].astype(BF16)

        xb = xb_ref[...]
        y = _dot(_silu_mul(_dot(xb, wg_ref[0]), _dot(xb, wu_ref[0])).astype(BF16), wd_ref[0])

        @pl.when(f == 0)
        def _():
            out_ref[...] = y

        @pl.when(f > 0)
        def _():
            out_ref[...] += y

    @pl.when((tvalid_ref[i] == 0) & (f == 0))
    def _():
        out_ref[...] = jnp.zeros_like(out_ref)


def expert_ffn(plan, xs, w_gu, w_down, *, f_chunk=1792):
    tile_expert, tile_eff, tile_valid = plan[4:]
    n_rows, d = xs.shape
    d_exp = w_down.shape[1]
    nf = d_exp // f_chunk
    assert nf * f_chunk == d_exp and f_chunk % LANES == 0
    n_tiles = n_rows // MOE_TM
    f_eff = lambda i, f, tv: jnp.where(tv[i] == 1, f, nf - 1)
    grid_spec = pltpu.PrefetchScalarGridSpec(
        num_scalar_prefetch=3,
        grid=(n_tiles, nf),
        in_specs=[pl.BlockSpec((MOE_TM, d), lambda i, f, te, tf, tv: (tf[i], 0)),
                  pl.BlockSpec((1, d, f_chunk), lambda i, f, te, tf, tv: (te[i], 0, f_eff(i, f, tv))),
                  pl.BlockSpec((1, d, f_chunk), lambda i, f, te, tf, tv: (te[i], 0, nf + f_eff(i, f, tv))),
                  pl.BlockSpec((1, f_chunk, d), lambda i, f, te, tf, tv: (te[i], f_eff(i, f, tv), 0))],
        out_specs=pl.BlockSpec((MOE_TM, d), lambda i, f, te, tf, tv: (i, 0)),
        scratch_shapes=[pltpu.VMEM((MOE_TM, d), BF16)],
    )
    return pl.pallas_call(
        _expert_ffn_kernel,
        out_shape=jax.ShapeDtypeStruct((n_rows, d), F32),
        grid_spec=grid_spec,
        compiler_params=_params("arbitrary", "arbitrary"),
        name="expert_ffn",
    )(tile_expert, tile_eff, tile_valid, xs, w_gu, w_gu, w_down)


def _combine_kernel(pad_ref, loff_ref, goff_ref, ys_ref, x_ref, sel_ref, comb_ref, g_ref, out_ref, buf_ref, sem,
                    *, nblk, ne):
    b = pl.program_id(0)
    slot = b % 2
    rows = buf_ref.shape[1]

    def fetch_block(blk, s):
        def start_one(lrow, grow):
            pltpu.make_async_copy(ys_ref.at[pl.ds(grow, SEG_ALIGN)], buf_ref.at[s, pl.ds(lrow, SEG_ALIGN)],
                                  sem.at[s]).start()

        _segment_copies(pad_ref, loff_ref, goff_ref, blk, nblk, ne, start_one)

    @pl.when(b == 0)
    def _():
        buf_ref[...] = jnp.zeros_like(buf_ref)
        fetch_block(0, 0)

    @pl.when(b + 1 < nblk)
    def _():
        fetch_block(b + 1, 1 - slot)

    tot = pl.multiple_of(_block_rows(pad_ref, b, nblk, ne), SEG_ALIGN)
    pltpu.make_async_copy(ys_ref.at[pl.ds(0, tot)], buf_ref.at[slot, pl.ds(0, tot)], sem.at[slot]).wait()

    lpos, lo_mask, hi_mask = _local_positions(sel_ref[...], loff_ref, b, nblk)
    comb = comb_ref[...]
    p1, p2 = _pick(lpos, lo_mask), _pick(lpos, hi_mask)
    w1, w2 = _pick(comb, lo_mask), _pick(comb, hi_mask)
    r = lax.broadcasted_iota(jnp.int32, (rows, TOK_BLOCK), 0)
    w_rows = jnp.sum(jnp.where(r == p1.astype(jnp.int32), w1, 0.0) + jnp.where(r == p2.astype(jnp.int32), w2, 0.0),
                     axis=1, keepdims=True)
    local = (buf_ref[slot] * w_rows).astype(BF16)
    hi1, hi2 = jnp.floor(p1 * (1.0 / 32.0)), jnp.floor(p2 * (1.0 / 32.0))
    info = jnp.concatenate([hi1, p1 - 32.0 * hi1, hi2, p2 - 32.0 * hi2, jnp.zeros((4, TOK_BLOCK), F32)], axis=0)
    eye = jnp.where(lax.broadcasted_iota(jnp.int32, (TOK_BLOCK, TOK_BLOCK), 0)
                    == lax.broadcasted_iota(jnp.int32, (TOK_BLOCK, TOK_BLOCK), 1), 1.0, 0.0).astype(BF16)
    cols = _dot_nt(eye, info.astype(BF16))
    c1 = (32.0 * cols[:, 0:1] + cols[:, 1:2]).astype(jnp.int32)
    c2 = (32.0 * cols[:, 2:3] + cols[:, 3:4]).astype(jnp.int32)
    c = lax.broadcasted_iota(jnp.int32, (TOK_BLOCK, rows), 1)
    unsort = jnp.where((c == c1) | (c == c2), 1.0, 0.0).astype(BF16)
    y = x_ref[...] + _dot(unsort, local)
    out_ref[...] = y * _rms_scale(y) * g_ref[...]


def moe_combine(plan, ys, x, sel_t, comb_t, g_final):
    pad, loff, goff = plan[:3]
    n, d = x.shape
    ne = sel_t.shape[0]
    nblk = n // TOK_BLOCK
    rows = _loc_rows(ne)
    lane_major = pl.BlockSpec((ne, TOK_BLOCK), lambda b, *_: (0, b))
    grid_spec = pltpu.PrefetchScalarGridSpec(
        num_scalar_prefetch=3,
        grid=(nblk,),
        in_specs=[pl.BlockSpec(memory_space=pltpu.HBM),
                  pl.BlockSpec((TOK_BLOCK, d), lambda b, *_: (b, 0)), lane_major, lane_major,
                  pl.BlockSpec((1, d), lambda b, *_: (0, 0))],
        out_specs=pl.BlockSpec((TOK_BLOCK, d), lambda b, *_: (b, 0)),
        scratch_shapes=[pltpu.VMEM((2, rows, d), F32), pltpu.SemaphoreType.DMA((2,))],
    )
    return pl.pallas_call(
        partial(_combine_kernel, nblk=nblk, ne=ne),
        out_shape=jax.ShapeDtypeStruct((n, d), F32),
        grid_spec=grid_spec,
        compiler_params=_params("arbitrary"),
        name="moe_combine",
    )(pad, loff, goff, ys, x, sel_t, comb_t, g_final)


def moe_ffn(hn, sel_t, comb_t, w_gu, w_down, x, g_final):
    n = hn.shape[0]
    ne = sel_t.shape[0]
    nblk = n // TOK_BLOCK
    n_tiles = -(-(2 * n + nblk * ne * (SEG_ALIGN - 1)) // MOE_TM) + ne
    plan = _dispatch_plan(sel_t, n_tiles)
    xs = moe_dispatch(plan, hn, sel_t, n_tiles=n_tiles)
    ys = expert_ffn(plan, xs, w_gu, w_down)
    return moe_combine(plan, ys, x, sel_t, comb_t, g_final)


def _rope_tables(seq):
    half = ROT_DIM // 2
    pos = jnp.arange(seq, dtype=F32)
    inv_freq = ROPE_THETA ** (-jnp.arange(0, ROT_DIM, 2, dtype=F32) / ROT_DIM)
    ang = pos[:, None] * inv_freq[None, :]
    cos, sin = jnp.cos(ang), jnp.sin(ang)
    pad = jnp.zeros((seq, LANES - ROT_DIM), F32)
    zeros = jnp.zeros((seq, half), F32)
    cos_t = jnp.concatenate([cos, cos, pad + 1.0], axis=1)
    sin_lo = jnp.concatenate([-sin, zeros, pad], axis=1)
    sin_hi = jnp.concatenate([zeros, sin, pad], axis=1)
    return cos_t, sin_lo, sin_hi


def kernel(x, mlstm_norm, mlstm_w_in, mlstm_b_if, mlstm_out_norm, mlstm_w_out, ffn_norm, dense_w_gu,
           dense_w_down, moe_router, moe_w_gu, moe_w_down, kv_norm, kv_w, moba_norm, moba_w_q, moba_w_o,
           final_norm):
    batch, seq, d = x.shape
    n = batch * seq
    H = MLSTM_HEADS
    n_main = H * (2 * MLSTM_DQK + 2 * MLSTM_DV)
    xf = x.reshape(n, d)
    row = lambda g: g.reshape(1, -1).astype(F32)

    w_in = mlstm_w_in[0]
    proj, gates = mlstm_proj(xf, row(mlstm_norm[0]), w_in[:, :n_main].astype(BF16),
                             w_in[:, n_main:].T, mlstm_b_if[0].reshape(2 * H, 1))
    hg = mlstm_core(proj, gates, row(mlstm_out_norm[0]), batch=batch, seq=seq)
    x2, moe_gu_bf16, moe_down_bf16 = mix_ffn(xf, hg, mlstm_w_out[0].astype(BF16), row(ffn_norm[0]),
                                             dense_w_gu[0].astype(BF16), dense_w_down[0].astype(BF16),
                                             riders=(moe_w_gu[0], moe_w_down[0]))

    cos_t, sin_lo, sin_hi = _rope_tables(seq)
    q, k_aug, v, k_means = kvq_proj(x2, row(kv_norm), row(moba_norm[0]), kv_w.astype(BF16),
                                    moba_w_q[0].astype(BF16), cos_t, sin_lo, sin_hi, seq=seq)
    k_means = k_means.reshape(n // MOBA_BLOCK, ATT_KV_HEADS * HEAD_DIM)
    att = moba_attn(q, k_aug, v, k_means, batch=batch, seq=seq)
    x3, hn, sel_t, comb_t = oproj_route(x2, att, moba_w_o[0].astype(BF16), row(ffn_norm[1]), moe_router[0].T)
    out = moe_ffn(hn, sel_t, comb_t, moe_gu_bf16, moe_down_bf16, x3, row(final_norm))
    return out.reshape(batch, seq, d)
```

```python
import math
from functools import partial

import jax
import jax.numpy as jnp
from jax import lax
from jax.experimental import pallas as pl
from jax.experimental.pallas import tpu as pltpu

NORM_EPS = 1e-6

MLSTM_HEADS = 4
MLSTM_DQK = 128
MLSTM_DV = 256
MLSTM_CHUNK = 256
MLSTM_ROWS = 512

ATT_HEADS = 8
ATT_KV_HEADS = 2
HEAD_DIM = 128
MOBA_BLOCK = 256
MOBA_TOPK = 3
ROW_GROUPS = 4
ROPE_THETA = 500000.0
ROT_DIM = HEAD_DIM // 4

N_EXPERTS = 8

LANES = 128
NEG_BIG = -1e30
LOG2E = 1.4426950408889634
VMEM_LIMIT = 56 * 1024 * 1024

F32 = jnp.float32
BF16 = jnp.bfloat16

_NT = (((1,), (1,)), ((), ()))


def _dot(a, b):
    return jnp.dot(a, b, preferred_element_type=F32)


def _dot_nt(a, b, precision=None):
    return lax.dot_general(a, b, _NT, precision=precision, preferred_element_type=F32)


def _split_bf16(a):
    hi = a.astype(BF16)
    return hi, (a - hi.astype(F32)).astype(BF16)


def _dot_nt_split(w, x_hi, x_lo):
    r = w.shape[0]
    w_hi, w_lo = _split_bf16(w)
    a = _dot_nt(jnp.concatenate([w_hi, w_lo], axis=0), x_hi)
    return a[:r] + a[r:] + _dot_nt(w_hi, x_lo)


def _rms_scale(x):
    return lax.rsqrt(jnp.mean(x * x, axis=-1, keepdims=True) + NORM_EPS)


def _sigmoid(x):
    return 1.0 / (1.0 + jnp.exp(-x))


def _params(*sem):
    return pltpu.CompilerParams(dimension_semantics=sem, vmem_limit_bytes=VMEM_LIMIT)


def _const_spec(shape):
    nd = len(shape)
    return pl.BlockSpec(shape, lambda *_: (0,) * nd, pipeline_mode=pl.Buffered(1))


def _mlstm_proj_kernel(x_ref, g_ref, w_ref, wg_ref, b_ref, proj_ref, gates_ref, *, col_chunk):
    x = x_ref[...]
    xn = x * _rms_scale(x) * g_ref[...]
    xb, x_lo = _split_bf16(xn)
    for c in range(w_ref.shape[1] // col_chunk):
        sl = slice(c * col_chunk, (c + 1) * col_chunk)
        proj_ref[:, sl] = _dot(xb, w_ref[:, sl]).astype(BF16)
    gates_ref[...] = _dot_nt_split(wg_ref[...], xb, x_lo) + b_ref[...]


def mlstm_proj(x, gain, w_main, w_gate_t, bias, *, tm=1024, col_chunk=512):
    n, d = x.shape
    p = w_main.shape[1]
    return pl.pallas_call(
        partial(_mlstm_proj_kernel, col_chunk=col_chunk),
        out_shape=(jax.ShapeDtypeStruct((n, p), BF16), jax.ShapeDtypeStruct((8, n), F32)),
        grid=(n // tm,),
        in_specs=[pl.BlockSpec((tm, d), lambda i: (i, 0)),
                  _const_spec((1, d)), _const_spec((d, p)), _const_spec((8, d)), _const_spec((8, 1))],
        out_specs=(pl.BlockSpec((tm, p), lambda i: (i, 0)), pl.BlockSpec((8, tm), lambda i: (0, i))),
        compiler_params=_params("parallel"),
        name="mlstm_proj",
    )(x, gain, w_main, w_gate_t, bias)


def _mlstm_core_kernel(q_ref, k_ref, v_ref, o_ref, gates_ref, gain_ref, out_ref, state_ref, m_ref):
    L = MLSTM_CHUNK
    H, DK, DV = MLSTM_HEADS, MLSTM_DQK, MLSTM_DV
    scale = DK ** -0.5

    @pl.when(pl.program_id(1) == 0)
    def _():
        state_ref[...] = jnp.zeros_like(state_ref)
        m_ref[...] = jnp.zeros_like(m_ref)

    row = lax.broadcasted_iota(jnp.int32, (L, L), 0)
    col = lax.broadcasted_iota(jnp.int32, (L, L), 1)
    causal = col <= row
    upper_incl = (row <= col).astype(F32)
    ones_blk = jnp.ones((L, LANES), BF16)

    hs = range(H)
    cs = range(q_ref.shape[0] // L)
    units = [(h, c) for c in cs for h in hs]
    rs = lambda c: slice(c * L, (c + 1) * L)
    qs = {(h, c): q_ref[rs(c), h * DK:(h + 1) * DK] for h, c in units}
    ks = {(h, c): k_ref[rs(c), h * DK:(h + 1) * DK] for h, c in units}
    v_augs = {(h, c): jnp.concatenate([v_ref[rs(c), h * DV:(h + 1) * DV], ones_blk], axis=1)
              for h, c in units}
    scores = {u: _dot_nt(qs[u], ks[u]) for u in units}

    log2_scale = math.log2(scale)
    gate = {}
    m_prev = {h: m_ref[h, 0:1, 0:1] for h in hs}
    for h, c in units:
        ig = gates_ref[h:h + 1, rs(c)] * LOG2E
        fg = gates_ref[H + h:H + h + 1, rs(c)]
        logf = (jnp.minimum(fg, 0.0) - jnp.log1p(jnp.exp(-jnp.abs(fg)))) * LOG2E
        bcum = jnp.dot(jnp.broadcast_to(logf, (8, L)), upper_incl,
                       precision=lax.Precision.HIGHEST, preferred_element_type=F32)[0:1, :]
        bcum_s = jnp.broadcast_to(bcum, (L, L))
        bcum_t = bcum_s.T
        dmat = jnp.where(causal, bcum_t - bcum_s + ig, -jnp.inf)
        inter = bcum_t[:, 0:1] + m_prev[h]
        m_t = jnp.maximum(inter, jnp.max(dmat, axis=-1, keepdims=True))
        m_s = m_t - log2_scale
        g_tot = bcum[:, L - 1:L]
        a_end = g_tot - bcum + ig
        m_new = jnp.maximum(g_tot + m_prev[h], jnp.max(a_end, axis=-1, keepdims=True))
        gate[h, c] = dict(m_t=m_t, inter_w=jnp.exp2(inter - m_s), e=jnp.exp2(dmat - m_s),
                          w_row=jnp.exp2(a_end - m_new), decay=jnp.exp2(g_tot + m_prev[h] - m_new))
        m_prev[h] = m_new

    ps = {u: (scores[u] * gate[u]["e"]).astype(BF16) for u in units}
    ktws = {u: (ks[u].astype(F32).T * gate[u]["w_row"]).astype(BF16) for u in units}
    states = {h: state_ref[h] for h in hs}
    nums = {}
    for c in cs:
        for h in hs:
            nums[h, c] = (_dot(ps[h, c], v_augs[h, c])
                          + gate[h, c]["inter_w"] * _dot(qs[h, c], states[h].astype(BF16)))
        for h in hs:
            states[h] = gate[h, c]["decay"] * states[h] + _dot(ktws[h, c], v_augs[h, c])
    for h in hs:
        state_ref[h] = states[h]
        m_ref[h] = jnp.broadcast_to(m_prev[h], m_ref.shape[1:])
    for h, c in units:
        nq = nums[h, c][:, DV:DV + 1]
        den = jnp.maximum(jnp.abs(nq), jnp.exp2(-gate[h, c]["m_t"]))
        hval = nums[h, c][:, :DV] / den
        hn = hval * _rms_scale(hval) * gain_ref[:, h * DV:(h + 1) * DV]
        og = o_ref[rs(c), h * DV:(h + 1) * DV].astype(F32)
        out_ref[rs(c), h * DV:(h + 1) * DV] = (hn * _sigmoid(og)).astype(BF16)


def mlstm_core(proj, gates, out_gain, *, batch, seq):
    n = proj.shape[0]
    L = MLSTM_ROWS
    assert L % MLSTM_CHUNK == 0
    nc = seq // L
    H, DK, DV = MLSTM_HEADS, MLSTM_DQK, MLSTM_DV
    qw, vw = H * DK, H * DV
    assert 2 * qw == vw
    rows = lambda b, c: b * nc + c
    return pl.pallas_call(
        _mlstm_core_kernel,
        out_shape=jax.ShapeDtypeStruct((n, vw), BF16),
        grid=(batch, nc),
        in_specs=[pl.BlockSpec((L, qw), lambda b, c: (rows(b, c), 0)),
                  pl.BlockSpec((L, qw), lambda b, c: (rows(b, c), 1)),
                  pl.BlockSpec((L, vw), lambda b, c: (rows(b, c), 1)),
                  pl.BlockSpec((L, vw), lambda b, c: (rows(b, c), 2)),
                  pl.BlockSpec((8, L), lambda b, c: (0, rows(b, c))),
                  _const_spec((1, vw))],
        out_specs=pl.BlockSpec((L, vw), lambda b, c: (rows(b, c), 0)),
        scratch_shapes=[pltpu.VMEM((H, DK, DV + LANES), F32), pltpu.VMEM((H, 8, LANES), F32)],
        compiler_params=_params("parallel", "arbitrary"),
        name="mlstm_core",
    )(proj, proj, proj, proj, gates, out_gain)


def _silu_mul(g, u):
    return g * _sigmoid(g) * u


def _mix_ffn_kernel(x_ref, h_ref, wo_ref, g_ref, wgu_ref, wd_ref, *rest, f_chunk, n_riders):
    rider_in, out_ref, rider_out = rest[:n_riders], rest[n_riders], rest[n_riders + 1:]
    d_ff = wd_ref.shape[0]
    x1 = x_ref[...] + _dot(h_ref[...], wo_ref[...])
    hn = (x1 * _rms_scale(x1) * g_ref[...]).astype(BF16)
    acc = jnp.zeros_like(x1)
    for c in range(d_ff // f_chunk):
        lo = c * f_chunk
        g = _dot(hn, wgu_ref[:, lo:lo + f_chunk])
        u = _dot(hn, wgu_ref[:, d_ff + lo:d_ff + lo + f_chunk])
        acc = acc + _dot(_silu_mul(g, u).astype(BF16), wd_ref[lo:lo + f_chunk, :])
    out_ref[...] = x1 + acc
    for src, dst in zip(rider_in, rider_out):
        dst[...] = src[...].astype(BF16)


def mix_ffn(x, h, w_out, gain, w_gu, w_down, riders=(), *, tm=512, f_chunk=256):
    n, d = x.shape
    d_ff = w_down.shape[0]
    steps = n // tm
    assert d_ff % f_chunk == 0
    slabs = []
    for w in riders:
        rows = w.size // (steps * w.shape[-1])
        assert rows * steps * w.shape[-1] == w.size and rows % 16 == 0 and w.shape[-2] % rows == 0
        slabs.append(w.reshape(steps, rows, w.shape[-1]))
    slab_spec = lambda s: pl.BlockSpec((1,) + s.shape[1:], lambda i: (i, 0, 0))
    outs = pl.pallas_call(
        partial(_mix_ffn_kernel, f_chunk=f_chunk, n_riders=len(slabs)),
        out_shape=(jax.ShapeDtypeStruct((n, d), F32),) + tuple(jax.ShapeDtypeStruct(s.shape, BF16) for s in slabs),
        grid=(steps,),
        in_specs=[pl.BlockSpec((tm, d), lambda i: (i, 0)), pl.BlockSpec((tm, h.shape[1]), lambda i: (i, 0)),
                  _const_spec(w_out.shape), _const_spec((1, d)), _const_spec(w_gu.shape),
                  _const_spec(w_down.shape)] + [slab_spec(s) for s in slabs],
        out_specs=(pl.BlockSpec((tm, d), lambda i: (i, 0)),) + tuple(slab_spec(s) for s in slabs),
        compiler_params=_params("parallel"),
        name="mix_ffn",
    )(x, h, w_out, gain, w_gu, w_down, *slabs)
    return (outs[0],) + tuple(o.reshape(w.shape) for o, w in zip(outs[1:], riders))


def _rope(t, cos, sin_lo, sin_hi):
    half = ROT_DIM // 2
    return t * cos + pltpu.roll(t, LANES - half, 1) * sin_lo + pltpu.roll(t, half, 1) * sin_hi


def _kvq_kernel(x_ref, gkv_ref, gq_ref, wkv_ref, wq_ref, cos_ref, slo_ref, shi_ref,
                q_ref, k_ref, v_ref, km_ref, *, blocks_per_seq):
    tm = x_ref.shape[0]
    nsub = tm // MOBA_BLOCK
    G, DH = ATT_KV_HEADS, HEAD_DIM
    x = x_ref[...]
    xs = x * _rms_scale(x)
    cos, slo, shi = cos_ref[...], slo_ref[...], shi_ref[...]

    kv = _dot((xs * gkv_ref[...]).astype(BF16), wkv_ref[...])
    for g in range(G):
        v_ref[:, 2 * g * DH:(2 * g + 1) * DH] = kv[:, (G + g) * DH:(G + g + 1) * DH].astype(BF16)
        v_ref[:, (2 * g + 1) * DH:(2 * g + 2) * DH] = jnp.ones((tm, DH), BF16)
    first_blk = (pl.program_id(0) * nsub) % blocks_per_seq
    blk = first_blk + lax.broadcasted_iota(jnp.int32, (tm, LANES), 0) // MOBA_BLOCK
    ind = (lax.broadcasted_iota(jnp.int32, (tm, LANES), 1) == blk).astype(BF16)
    for g in range(G):
        kg = _rope(kv[:, g * DH:(g + 1) * DH], cos, slo, shi)
        k_ref[:, 2 * g * DH:(2 * g + 1) * DH] = kg.astype(BF16)
        k_ref[:, (2 * g + 1) * DH:(2 * g + 2) * DH] = ind
        for s in range(nsub):
            km_ref[0, s:s + 1, g * DH:(g + 1) * DH] = jnp.mean(
                kg[s * MOBA_BLOCK:(s + 1) * MOBA_BLOCK, :], axis=0, keepdims=True)

    q = _dot((xs * gq_ref[...]).astype(BF16), wq_ref[...])
    for h in range(ATT_HEADS):
        q_ref[:, h * DH:(h + 1) * DH] = _rope(q[:, h * DH:(h + 1) * DH], cos, slo, shi).astype(BF16)


def kvq_proj(x, g_kv, g_q, w_kv, w_q, cos, sin_lo, sin_hi, *, seq, tm=1024):
    n, d = x.shape
    G, DH = ATT_KV_HEADS, HEAD_DIM
    nsub = tm // MOBA_BLOCK
    tiles_per_seq = seq // tm
    tab = pl.BlockSpec((tm, LANES), lambda i: (i % tiles_per_seq, 0))
    return pl.pallas_call(
        partial(_kvq_kernel, blocks_per_seq=seq // MOBA_BLOCK),
        out_shape=(jax.ShapeDtypeStruct((n, ATT_HEADS * DH), BF16),
                   jax.ShapeDtypeStruct((n, 2 * G * DH), BF16),
                   jax.ShapeDtypeStruct((n, 2 * G * DH), BF16),
                   jax.ShapeDtypeStruct((n // tm, nsub, G * DH), F32)),
        grid=(n // tm,),
        in_specs=[pl.BlockSpec((tm, d), lambda i: (i, 0)), _const_spec((1, d)), _const_spec((1, d)),
                  _const_spec(w_kv.shape), _const_spec(w_q.shape), tab, tab, tab],
        out_specs=(pl.BlockSpec((tm, ATT_HEADS * DH), lambda i: (i, 0)),
                   pl.BlockSpec((tm, 2 * G * DH), lambda i: (i, 0)),
                   pl.BlockSpec((tm, 2 * G * DH), lambda i: (i, 0)),
                   pl.BlockSpec((1, nsub, G * DH), lambda i: (i, 0, 0))),
        compiler_params=_params("parallel"),
        name="kvq_proj",
    )(x, g_kv, g_q, w_kv, w_q, cos, sin_lo, sin_hi)


def _moba_kernel(q_ref, k_ref, v_ref, km_ref, out_ref, m_ref, acc_ref):
    QB, DH = MOBA_BLOCK, HEAD_DIM
    hpg = ATT_HEADS // ATT_KV_HEADS
    R = hpg * QB
    scale = DH ** -0.5
    j = pl.program_id(2)

    q = jnp.concatenate([q_ref[:, h * DH:(h + 1) * DH] for h in range(hpg)], axis=0)

    exp_scale = scale * LOG2E
    wide = lambda a: jnp.concatenate([a] * (QB // LANES), axis=1)
    groups = [slice(i * R // ROW_GROUPS, (i + 1) * R // ROW_GROUPS) for i in range(ROW_GROUPS)]

    k_own = k_ref[pl.ds(pl.multiple_of(j * QB, QB), QB), :]
    v_own = v_ref[pl.ds(pl.multiple_of(j * QB, QB), QB), :]
    own_scores = [_dot_nt(q[rs], k_own[:, :DH]) for rs in groups]

    nb = km_ref.shape[0]
    km_hi, km_lo = _split_bf16(km_ref[...])
    g2 = _dot_nt(jnp.concatenate([km_hi, km_lo], axis=0), q)
    gate = g2[:nb] + g2[nb:]
    blk = lax.broadcasted_iota(jnp.int32, (nb, R), 0)
    beats = jnp.zeros((nb, R), jnp.int32)
    for mb in range(nb):
        gm = gate[mb:mb + 1, :]
        ahead = (gm > gate) | ((gm == gate) & (mb < blk))
        beats = beats + jnp.where(ahead, 1, 0) * (mb < j).astype(jnp.int32)
    selected = jnp.where((blk < j) & (beats < MOBA_TOPK), 1.0, 0.0).astype(BF16)
    eye = jnp.where(lax.broadcasted_iota(jnp.int32, (nb, LANES), 0)
                    == lax.broadcasted_iota(jnp.int32, (nb, LANES), 1), 1.0, 0.0).astype(BF16)
    sel_rows = lax.dot_general(selected, eye, (((0,), (0,)), ((), ())), preferred_element_type=F32)
    pen = ((1.0 - sel_rows) * NEG_BIG).astype(BF16)
    q_aug = jnp.concatenate([q, pen], axis=1)

    m0s, p0s = [], []
    for rs, s in zip(groups, own_scores):
        rows = rs.stop - rs.start
        t_pos = (lax.broadcasted_iota(jnp.int32, (rows, QB), 0) + rs.start) % QB
        s_pos = lax.broadcasted_iota(jnp.int32, (rows, QB), 1)
        s = jnp.where(s_pos <= t_pos, s, -jnp.inf)
        m0 = jnp.max(s, axis=-1, keepdims=True)
        m0s.append(m0)
        p0s.append(jnp.exp2((s - m0) * exp_scale).astype(BF16))
    for rs, m0, p0 in zip(groups, m0s, p0s):
        m_ref[rs, :] = jnp.broadcast_to(m0, (rs.stop - rs.start, LANES))
        acc_ref[rs, :] = _dot(p0, v_own)

    def attend(first_blk, n_blk):
        start = pl.multiple_of(first_blk * QB, QB)
        k_n = k_ref[pl.ds(start, n_blk * QB), :]
        v_n = v_ref[pl.ds(start, n_blk * QB), :]
        ss = [_dot_nt(q_aug[rs], k_n) for rs in groups]
        m_olds = [m_ref[rs, :] for rs in groups]
        m_news = [jnp.maximum(mo, jnp.max(s, axis=-1, keepdims=True)) for mo, s in zip(m_olds, ss)]
        ps = [jnp.exp2((s - jnp.concatenate([mn] * (n_blk * QB // LANES), axis=1)) * exp_scale).astype(BF16)
              for s, mn in zip(ss, m_news)]
        pvs = [_dot(p, v_n) for p in ps]
        for rs, mo, mn, pv in zip(groups, m_olds, m_news, pvs):
            alpha = jnp.exp2((mo - mn) * exp_scale)
            acc_ref[rs, :] = wide(alpha) * acc_ref[rs, :] + pv
            m_ref[rs, :] = mn

    def body(i, carry):
        attend(2 * i, 2)
        return carry

    n_pairs = lax.shift_right_logical(j, 1)
    lax.fori_loop(0, n_pairs, body, 0)

    @pl.when(j % 2 == 1)
    def _():
        attend(j - 1, 1)

    acc = acc_ref[...]
    out = acc[:, :DH] / acc[:, DH:]
    for h in range(hpg):
        out_ref[:, h * DH:(h + 1) * DH] = out[h * QB:(h + 1) * QB, :].astype(BF16)


def moba_attn(q, k_aug, v, k_means, *, batch, seq):
    n = q.shape[0]
    G, DH, QB = ATT_KV_HEADS, HEAD_DIM, MOBA_BLOCK
    hpg = ATT_HEADS // G
    nb = seq // QB
    assert DH == LANES and nb <= LANES
    R = hpg * QB
    return pl.pallas_call(
        _moba_kernel,
        out_shape=jax.ShapeDtypeStruct((n, ATT_HEADS * DH), BF16),
        grid=(batch, G, nb),
        in_specs=[pl.BlockSpec((QB, hpg * DH), lambda b, g, j: (b * nb + j, g)),
                  pl.BlockSpec((seq, 2 * DH), lambda b, g, j: (b, g)),
                  pl.BlockSpec((seq, 2 * DH), lambda b, g, j: (b, g)),
                  pl.BlockSpec((nb, DH), lambda b, g, j: (b, g))],
        out_specs=pl.BlockSpec((QB, hpg * DH), lambda b, g, j: (b * nb + j, g)),
        scratch_shapes=[pltpu.VMEM((R, LANES), F32), pltpu.VMEM((R, 2 * DH), F32)],
        compiler_params=_params("parallel", "parallel", "arbitrary"),
        name="moba_attn",
    )(q, k_aug, v, k_means)


def _oproj_route_kernel(x_ref, a_ref, wo_ref, g_ref, r_ref, x_out_ref, hn_ref, sel_ref, comb_ref):
    x3 = x_ref[...] + _dot(a_ref[...], wo_ref[...])
    x_out_ref[...] = x3
    hn = x3 * _rms_scale(x3) * g_ref[...]
    hn_hi, hn_lo = _split_bf16(hn)
    hn_ref[...] = hn_hi
    logits = _dot_nt_split(r_ref[...], hn_hi, hn_lo)
    ne = logits.shape[0]
    idx = lax.broadcasted_iota(jnp.int32, logits.shape, 0)
    m1 = jnp.max(logits, axis=0, keepdims=True)
    i1 = jnp.min(jnp.where(logits == m1, idx, ne), axis=0, keepdims=True)
    first = idx == i1
    rest = jnp.where(first, -jnp.inf, logits)
    m2 = jnp.max(rest, axis=0, keepdims=True)
    i2 = jnp.min(jnp.where(rest == m2, idx, ne), axis=0, keepdims=True)
    second = idx == i2
    e = jnp.exp(m2 - m1)
    sel_ref[...] = jnp.where(first | second, 1.0, 0.0)
    comb_ref[...] = jnp.where(first, 1.0 / (1.0 + e), 0.0) + jnp.where(second, e / (1.0 + e), 0.0)


def oproj_route(x, att, w_o, gain, router_t, *, tm=1024):
    n, d = x.shape
    ne = router_t.shape[0]
    row = lambda w: pl.BlockSpec((tm, w), lambda i: (i, 0))
    lane_major = pl.BlockSpec((ne, tm), lambda i: (0, i))
    return pl.pallas_call(
        _oproj_route_kernel,
        out_shape=(jax.ShapeDtypeStruct((n, d), F32), jax.ShapeDtypeStruct((n, d), BF16),
                   jax.ShapeDtypeStruct((ne, n), F32), jax.ShapeDtypeStruct((ne, n), F32)),
        grid=(n // tm,),
        in_specs=[row(d), row(att.shape[1]), _const_spec(w_o.shape), _const_spec((1, d)),
                  _const_spec(router_t.shape)],
        out_specs=(row(d), row(d), lane_major, lane_major),
        compiler_params=_params("parallel"),
        name="oproj_route",
    )(x, att, w_o, gain, router_t)


TOK_BLOCK = 256
SEG_ALIGN = 8
BIG_CHUNK = 32
MOE_TM = 512

def _loc_rows(ne):
    need = 2 * TOK_BLOCK + ne * (SEG_ALIGN - 1)
    return -(-need // LANES) * LANES


def _dispatch_plan(sel_t, n_tiles):
    ne, n = sel_t.shape
    nblk = n // TOK_BLOCK
    i32 = jnp.int32
    cnt = sel_t.reshape(ne, nblk, TOK_BLOCK).sum(-1).astype(i32)
    pad = (cnt + SEG_ALIGN - 1) // SEG_ALIGN * SEG_ALIGN
    loff = jnp.cumsum(pad, axis=0) - pad
    tot = pad.sum(axis=1)
    reg = (tot + MOE_TM - 1) // MOE_TM * MOE_TM
    reg_start = jnp.cumsum(reg) - reg
    goff = reg_start[:, None] + jnp.cumsum(pad, axis=1) - pad
    tiles_end = jnp.cumsum(reg // MOE_TM)
    n_valid = tiles_end[-1]
    t = jnp.arange(n_tiles, dtype=i32)
    t_eff = jnp.minimum(t, n_valid - 1)
    tile_expert = (tiles_end[None, :] <= t_eff[:, None]).sum(axis=1).astype(i32)
    tile_valid = (t < n_valid).astype(i32)
    used = reg.sum()
    tail = jnp.concatenate([reg_start + tot, used[None], (reg - tot) // SEG_ALIGN,
                            ((n_tiles * MOE_TM - used) // SEG_ALIGN)[None]]).astype(i32)
    flat = lambda a: a.reshape(-1).astype(i32)
    return flat(pad), flat(loff), flat(goff), tail, tile_expert, t_eff.astype(i32), tile_valid


def _block_rows(pad_ref, blk, nblk, ne):
    tot = pad_ref[blk]
    for e in range(1, ne):
        tot = tot + pad_ref[e * nblk + blk]
    return tot


def _segment_copies(pad_ref, loff_ref, goff_ref, blk, nblk, ne, start_one):
    per_big = BIG_CHUNK // SEG_ALIGN
    for e in range(ne):
        n_small = lax.shift_right_logical(pad_ref[e * nblk + blk], 3)
        n_big = lax.div(n_small, per_big)
        lo, go = loff_ref[e * nblk + blk], goff_ref[e * nblk + blk]

        def big(i, carry, lo=lo, go=go):
            start_one(pl.multiple_of(lo + i * BIG_CHUNK, SEG_ALIGN), pl.multiple_of(go + i * BIG_CHUNK, SEG_ALIGN),
                      BIG_CHUNK)
            return carry

        def small(i, carry, lo=lo, go=go):
            start_one(pl.multiple_of(lo + i * SEG_ALIGN, SEG_ALIGN), pl.multiple_of(go + i * SEG_ALIGN, SEG_ALIGN),
                      SEG_ALIGN)
            return carry

        lax.fori_loop(0, n_big, big, 0)
        lax.fori_loop(n_big * per_big, n_small, small, 0)


def _local_positions(sel, loff_ref, blk, nblk):
    ne, tb = sel.shape
    sub = lax.broadcasted_iota(jnp.int32, (ne, tb), 0)
    row = lax.broadcasted_iota(jnp.int32, (tb, tb), 0)
    col = lax.broadcasted_iota(jnp.int32, (tb, tb), 1)
    before = (row < col).astype(BF16)
    rank = _dot(sel.astype(BF16), before)
    sub1 = lax.broadcasted_iota(jnp.int32, (ne, 1), 0)
    loff = jnp.zeros((ne, 1), F32)
    for e in range(ne):
        loff = jnp.where(sub1 == e, loff_ref[e * nblk + blk].astype(F32), loff)
    chosen = sel > 0.0
    e_lo = jnp.min(jnp.where(chosen, sub, ne), axis=0, keepdims=True)
    e_hi = jnp.max(jnp.where(chosen, sub, -1), axis=0, keepdims=True)
    return loff + rank, sub == e_lo, sub == e_hi


def _pick(a, mask):
    return jnp.sum(jnp.where(mask, a, 0.0), axis=0, keepdims=True)


def _dispatch_kernel(pad_ref, loff_ref, goff_ref, tail_ref, hn_ref, sel_ref, xs_ref, buf_ref, zero_ref,
                     sem, tail_sem, *, nblk, ne):
    b = pl.program_id(0)
    slot = b % 2
    rows = buf_ref.shape[1]

    def wait_block(blk, s):
        tot = pl.multiple_of(_block_rows(pad_ref, blk, nblk, ne), SEG_ALIGN)
        pltpu.make_async_copy(buf_ref.at[s, pl.ds(0, tot)], xs_ref.at[pl.ds(0, tot)], sem.at[s]).wait()

    @pl.when(b >= 2)
    def _():
        wait_block(b - 2, slot)

    lpos, lo_mask, hi_mask = _local_positions(sel_ref[...], loff_ref, b, nblk)
    p1 = _pick(lpos, lo_mask).astype(jnp.int32)
    p2 = _pick(lpos, hi_mask).astype(jnp.int32)
    r = lax.broadcasted_iota(jnp.int32, (rows, TOK_BLOCK), 0)
    perm = jnp.where((r == p1) | (r == p2), 1.0, 0.0).astype(BF16)
    buf_ref[slot] = _dot(perm, hn_ref[...])

    def start_one(lrow, grow, n):
        pltpu.make_async_copy(buf_ref.at[slot, pl.ds(lrow, n)], xs_ref.at[pl.ds(grow, n)], sem.at[slot]).start()

    _segment_copies(pad_ref, loff_ref, goff_ref, b, nblk, ne, start_one)

    @pl.when(b == nblk - 1)
    def _():
        zero_ref[...] = jnp.zeros_like(zero_ref)
        n_span = ne + 1
        for sp in range(n_span):
            def start_zero(i, carry, sp=sp):
                grow = pl.multiple_of(tail_ref[sp] + i * SEG_ALIGN, SEG_ALIGN)
                pltpu.make_async_copy(zero_ref, xs_ref.at[pl.ds(grow, SEG_ALIGN)], tail_sem).start()
                return carry

            lax.fori_loop(0, tail_ref[n_span + sp], start_zero, 0)

        for sp in range(n_span):
            def wait_zero(i, carry):
                pltpu.make_async_copy(zero_ref, xs_ref.at[pl.ds(0, SEG_ALIGN)], tail_sem).wait()
                return carry

            lax.fori_loop(0, tail_ref[n_span + sp], wait_zero, 0)

        wait_block(b, slot)
        if nblk > 1:
            wait_block(b - 1, 1 - slot)


def moe_dispatch(plan, hn, sel_t, *, n_tiles):
    pad, loff, goff, tail = plan[:4]
    n, d = hn.shape
    ne = sel_t.shape[0]
    nblk = n // TOK_BLOCK
    rows = _loc_rows(ne)
    grid_spec = pltpu.PrefetchScalarGridSpec(
        num_scalar_prefetch=4,
        grid=(nblk,),
        in_specs=[pl.BlockSpec((TOK_BLOCK, d), lambda b, *_: (b, 0)),
                  pl.BlockSpec((ne, TOK_BLOCK), lambda b, *_: (0, b))],
        out_specs=pl.BlockSpec(memory_space=pltpu.HBM),
        scratch_shapes=[pltpu.VMEM((2, rows, d), F32), pltpu.VMEM((SEG_ALIGN, d), F32),
                        pltpu.SemaphoreType.DMA((2,)), pltpu.SemaphoreType.DMA(())],
    )
    return pl.pallas_call(
        partial(_dispatch_kernel, nblk=nblk, ne=ne),
        out_shape=jax.ShapeDtypeStruct((n_tiles * MOE_TM, d), F32),
        grid_spec=grid_spec,
        compiler_params=_params("arbitrary"),
        name="moe_dispatch",
    )(pad, loff, goff, tail, hn, sel_t)


def _expert_ffn_kernel(texp_ref, teff_ref, tvalid_ref, x_ref, wg_ref, wu_ref, wd_ref, out_ref, xb_ref):
    i, f = pl.program_id(0), pl.program_id(1)

    @pl.when(tvalid_ref[i] == 1)
    def _():
        @pl.when(f == 0)
        def _():
            xb_ref[...] = x_ref[...].astype(BF16)

        xb = xb_ref[...]
        y = _dot(_silu_mul(_dot(xb, wg_ref[0]), _dot(xb, wu_ref[0])).astype(BF16), wd_ref[0])

        @pl.when(f == 0)
        def _():
            out_ref[...] = y

        @pl.when(f > 0)
        def _():
            out_ref[...] += y

    @pl.when((tvalid_ref[i] == 0) & (f == 0))
    def _():
        out_ref[...] = jnp.zeros_like(out_ref)


def expert_ffn(plan, xs, w_gu, w_down, *, f_chunk=1792):
    tile_expert, tile_eff, tile_valid = plan[4:]
    n_rows, d = xs.shape
    d_exp = w_down.shape[1]
    nf = d_exp // f_chunk
    assert nf * f_chunk == d_exp and f_chunk % LANES == 0
    n_tiles = n_rows // MOE_TM
    f_eff = lambda i, f, tv: jnp.where(tv[i] == 1, f, nf - 1)
    grid_spec = pltpu.PrefetchScalarGridSpec(
        num_scalar_prefetch=3,
        grid=(n_tiles, nf),
        in_specs=[pl.BlockSpec((MOE_TM, d), lambda i, f, te, tf, tv: (tf[i], 0)),
                  pl.BlockSpec((1, d, f_chunk), lambda i, f, te, tf, tv: (te[i], 0, f_eff(i, f, tv))),
                  pl.BlockSpec((1, d, f_chunk), lambda i, f, te, tf, tv: (te[i], 0, nf + f_eff(i, f, tv))),
                  pl.BlockSpec((1, f_chunk, d), lambda i, f, te, tf, tv: (te[i], f_eff(i, f, tv), 0))],
        out_specs=pl.BlockSpec((MOE_TM, d), lambda i, f, te, tf, tv: (i, 0)),
        scratch_shapes=[pltpu.VMEM((MOE_TM, d), BF16)],
    )
    return pl.pallas_call(
        _expert_ffn_kernel,
        out_shape=jax.ShapeDtypeStruct((n_rows, d), F32),
        grid_spec=grid_spec,
        compiler_params=_params("arbitrary", "arbitrary"),
        name="expert_ffn",
    )(tile_expert, tile_eff, tile_valid, xs, w_gu, w_gu, w_down)


def _combine_kernel(pad_ref, loff_ref, goff_ref, ys_ref, x_ref, sel_ref, comb_ref, g_ref, out_ref, buf_ref, sem,
                    *, nblk, ne):
    b = pl.program_id(0)
    slot = b % 2
    rows = buf_ref.shape[1]

    def fetch_block(blk, s):
        def start_one(lrow, grow, n):
            pltpu.make_async_copy(ys_ref.at[pl.ds(grow, n)], buf_ref.at[s, pl.ds(lrow, n)], sem.at[s]).start()

        _segment_copies(pad_ref, loff_ref, goff_ref, blk, nblk, ne, start_one)

    @pl.when(b == 0)
    def _():
        buf_ref[...] = jnp.zeros_like(buf_ref)
        fetch_block(0, 0)

    @pl.when(b + 1 < nblk)
    def _():
        fetch_block(b + 1, 1 - slot)

    tot = pl.multiple_of(_block_rows(pad_ref, b, nblk, ne), SEG_ALIGN)
    pltpu.make_async_copy(ys_ref.at[pl.ds(0, tot)], buf_ref.at[slot, pl.ds(0, tot)], sem.at[slot]).wait()

    lpos, lo_mask, hi_mask = _local_positions(sel_ref[...], loff_ref, b, nblk)
    comb = comb_ref[...]
    p1, p2 = _pick(lpos, lo_mask), _pick(lpos, hi_mask)
    w1, w2 = _pick(comb, lo_mask), _pick(comb, hi_mask)
    r = lax.broadcasted_iota(jnp.int32, (rows, TOK_BLOCK), 0)
    w_rows = jnp.sum(jnp.where(r == p1.astype(jnp.int32), w1, 0.0) + jnp.where(r == p2.astype(jnp.int32), w2, 0.0),
                     axis=1, keepdims=True)
    local = (buf_ref[slot] * w_rows).astype(BF16)
    hi1, hi2 = jnp.floor(p1 * (1.0 / 32.0)), jnp.floor(p2 * (1.0 / 32.0))
    info = jnp.concatenate([hi1, p1 - 32.0 * hi1, hi2, p2 - 32.0 * hi2, jnp.zeros((4, TOK_BLOCK), F32)], axis=0)
    eye = jnp.where(lax.broadcasted_iota(jnp.int32, (TOK_BLOCK, TOK_BLOCK), 0)
                    == lax.broadcasted_iota(jnp.int32, (TOK_BLOCK, TOK_BLOCK), 1), 1.0, 0.0).astype(BF16)
    cols = _dot_nt(eye, info.astype(BF16))
    c1 = (32.0 * cols[:, 0:1] + cols[:, 1:2]).astype(jnp.int32)
    c2 = (32.0 * cols[:, 2:3] + cols[:, 3:4]).astype(jnp.int32)
    c = lax.broadcasted_iota(jnp.int32, (TOK_BLOCK, rows), 1)
    unsort = jnp.where((c == c1) | (c == c2), 1.0, 0.0).astype(BF16)
    y = x_ref[...] + _dot(unsort, local)
    out_ref[...] = y * _rms_scale(y) * g_ref[...]


def moe_combine(plan, ys, x, sel_t, comb_t, g_final):
    pad, loff, goff = plan[:3]
    n, d = x.shape
    ne = sel_t.shape[0]
    nblk = n // TOK_BLOCK
    rows = _loc_rows(ne)
    lane_major = pl.BlockSpec((ne, TOK_BLOCK), lambda b, *_: (0, b))
    grid_spec = pltpu.PrefetchScalarGridSpec(
        num_scalar_prefetch=3,
        grid=(nblk,),
        in_specs=[pl.BlockSpec(memory_space=pltpu.HBM),
                  pl.BlockSpec((TOK_BLOCK, d), lambda b, *_: (b, 0)), lane_major, lane_major,
                  pl.BlockSpec((1, d), lambda b, *_: (0, 0))],
        out_specs=pl.BlockSpec((TOK_BLOCK, d), lambda b, *_: (b, 0)),
        scratch_shapes=[pltpu.VMEM((2, rows, d), F32), pltpu.SemaphoreType.DMA((2,))],
    )
    return pl.pallas_call(
        partial(_combine_kernel, nblk=nblk, ne=ne),
        out_shape=jax.ShapeDtypeStruct((n, d), F32),
        grid_spec=grid_spec,
        compiler_params=_params("arbitrary"),
        name="moe_combine",
    )(pad, loff, goff, ys, x, sel_t, comb_t, g_final)


def moe_ffn(hn, sel_t, comb_t, w_gu, w_down, x, g_final):
    n = hn.shape[0]
    ne = sel_t.shape[0]
    nblk = n // TOK_BLOCK
    n_tiles = -(-(2 * n + nblk * ne * (SEG_ALIGN - 1)) // MOE_TM) + ne
    plan = _dispatch_plan(sel_t, n_tiles)
    xs = moe_dispatch(plan, hn, sel_t, n_tiles=n_tiles)
    ys = expert_ffn(plan, xs, w_gu, w_down)
    return moe_combine(plan, ys, x, sel_t, comb_t, g_final)


def _rope_tables(seq):
    half = ROT_DIM // 2
    pos = jnp.arange(seq, dtype=F32)
    inv_freq = ROPE_THETA ** (-jnp.arange(0, ROT_DIM, 2, dtype=F32) / ROT_DIM)
    ang = pos[:, None] * inv_freq[None, :]
    cos, sin = jnp.cos(ang), jnp.sin(ang)
    pad = jnp.zeros((seq, LANES - ROT_DIM), F32)
    zeros = jnp.zeros((seq, half), F32)
    cos_t = jnp.concatenate([cos, cos, pad + 1.0], axis=1)
    sin_lo = jnp.concatenate([-sin, zeros, pad], axis=1)
    sin_hi = jnp.concatenate([zeros, sin, pad], axis=1)
    return cos_t, sin_lo, sin_hi


def kernel(x, mlstm_norm, mlstm_w_in, mlstm_b_if, mlstm_out_norm, mlstm_w_out, ffn_norm, dense_w_gu,
           dense_w_down, moe_router, moe_w_gu, moe_w_down, kv_norm, kv_w, moba_norm, moba_w_q, moba_w_o,
           final_norm):
    batch, seq, d = x.shape
    n = batch * seq
    H = MLSTM_HEADS
    n_main = H * (2 * MLSTM_DQK + 2 * MLSTM_DV)
    xf = x.reshape(n, d)
    row = lambda g: g.reshape(1, -1).astype(F32)

    w_in = mlstm_w_in[0]
    proj, gates = mlstm_proj(xf, row(mlstm_norm[0]), w_in[:, :n_main].astype(BF16),
                             w_in[:, n_main:].T, mlstm_b_if[0].reshape(2 * H, 1))
    hg = mlstm_core(proj, gates, row(mlstm_out_norm[0]), batch=batch, seq=seq)
    x2, moe_gu_bf16, moe_down_bf16 = mix_ffn(xf, hg, mlstm_w_out[0].astype(BF16), row(ffn_norm[0]),
                                             dense_w_gu[0].astype(BF16), dense_w_down[0].astype(BF16),
                                             riders=(moe_w_gu[0], moe_w_down[0]))

    cos_t, sin_lo, sin_hi = _rope_tables(seq)
    q, k_aug, v, k_means = kvq_proj(x2, row(kv_norm), row(moba_norm[0]), kv_w.astype(BF16),
                                    moba_w_q[0].astype(BF16), cos_t, sin_lo, sin_hi, seq=seq)
    k_means = k_means.reshape(n // MOBA_BLOCK, ATT_KV_HEADS * HEAD_DIM)
    att = moba_attn(q, k_aug, v, k_means, batch=batch, seq=seq)
    x3, hn, sel_t, comb_t = oproj_route(x2, att, moba_w_o[0].astype(BF16), row(ffn_norm[1]), moe_router[0].T)
    out = moe_ffn(hn, sel_t, comb_t, moe_gu_bf16, moe_down_bf16, x3, row(final_norm))
    return out.reshape(batch, seq, d)
```

```python
import math
from functools import partial

import jax
import jax.numpy as jnp
from jax import lax
from jax.experimental import pallas as pl
from jax.experimental.pallas import tpu as pltpu

NORM_EPS = 1e-6

MLSTM_HEADS = 4
MLSTM_DQK = 128
MLSTM_DV = 256
MLSTM_CHUNK = 256
MLSTM_ROWS = 512

ATT_HEADS = 8
ATT_KV_HEADS = 2
HEAD_DIM = 128
MOBA_BLOCK = 256
MOBA_TOPK = 3
ROW_GROUPS = 4
ROPE_THETA = 500000.0
ROT_DIM = HEAD_DIM // 4

N_EXPERTS = 8

LANES = 128
NEG_BIG = -1e30
LOG2E = 1.4426950408889634
VMEM_LIMIT = 56 * 1024 * 1024

F32 = jnp.float32
BF16 = jnp.bfloat16

_NT = (((1,), (1,)), ((), ()))


def _dot(a, b):
    return jnp.dot(a, b, preferred_element_type=F32)


def _dot_nt(a, b, precision=None):
    return lax.dot_general(a, b, _NT, precision=precision, preferred_element_type=F32)


def _split_bf16(a):
    hi = a.astype(BF16)
    return hi, (a - hi.astype(F32)).astype(BF16)


def _dot_nt_split(w, x_hi, x_lo):
    r = w.shape[0]
    w_hi, w_lo = _split_bf16(w)
    a = _dot_nt(jnp.concatenate([w_hi, w_lo], axis=0), x_hi)
    return a[:r] + a[r:] + _dot_nt(w_hi, x_lo)


def _rms_scale(x):
    return lax.rsqrt(jnp.mean(x * x, axis=-1, keepdims=True) + NORM_EPS)


def _sigmoid(x):
    return 1.0 / (1.0 + jnp.exp(-x))


def _params(*sem):
    return pltpu.CompilerParams(dimension_semantics=sem, vmem_limit_bytes=VMEM_LIMIT)


def _const_spec(shape):
    nd = len(shape)
    return pl.BlockSpec(shape, lambda *_: (0,) * nd, pipeline_mode=pl.Buffered(1))


def _mlstm_proj_kernel(x_ref, g_ref, w_ref, wg_ref, b_ref, proj_ref, gates_ref, *, col_chunk):
    x = x_ref[...]
    xn = x * _rms_scale(x) * g_ref[...]
    xb, x_lo = _split_bf16(xn)
    for c in range(w_ref.shape[1] // col_chunk):
        sl = slice(c * col_chunk, (c + 1) * col_chunk)
        proj_ref[:, sl] = _dot(xb, w_ref[:, sl]).astype(BF16)
    gates_ref[...] = _dot_nt_split(wg_ref[...], xb, x_lo) + b_ref[...]


def mlstm_proj(x, gain, w_main, w_gate_t, bias, *, tm=1024, col_chunk=512):
    n, d = x.shape
    p = w_main.shape[1]
    return pl.pallas_call(
        partial(_mlstm_proj_kernel, col_chunk=col_chunk),
        out_shape=(jax.ShapeDtypeStruct((n, p), BF16), jax.ShapeDtypeStruct((8, n), F32)),
        grid=(n // tm,),
        in_specs=[pl.BlockSpec((tm, d), lambda i: (i, 0)),
                  _const_spec((1, d)), _const_spec((d, p)), _const_spec((8, d)), _const_spec((8, 1))],
        out_specs=(pl.BlockSpec((tm, p), lambda i: (i, 0)), pl.BlockSpec((8, tm), lambda i: (0, i))),
        compiler_params=_params("parallel"),
        name="mlstm_proj",
    )(x, gain, w_main, w_gate_t, bias)


def _mlstm_core_kernel(q_ref, k_ref, v_ref, o_ref, gates_ref, gain_ref, out_ref, state_ref, m_ref):
    L = MLSTM_CHUNK
    H, DK, DV = MLSTM_HEADS, MLSTM_DQK, MLSTM_DV
    scale = DK ** -0.5

    @pl.when(pl.program_id(1) == 0)
    def _():
        state_ref[...] = jnp.zeros_like(state_ref)
        m_ref[...] = jnp.zeros_like(m_ref)

    row = lax.broadcasted_iota(jnp.int32, (L, L), 0)
    col = lax.broadcasted_iota(jnp.int32, (L, L), 1)
    causal = col <= row
    upper_incl = (row <= col).astype(F32)
    ones_blk = jnp.ones((L, LANES), BF16)

    hs = range(H)
    cs = range(q_ref.shape[0] // L)
    units = [(h, c) for c in cs for h in hs]
    rs = lambda c: slice(c * L, (c + 1) * L)
    qs = {(h, c): q_ref[rs(c), h * DK:(h + 1) * DK] for h, c in units}
    ks = {(h, c): k_ref[rs(c), h * DK:(h + 1) * DK] for h, c in units}
    v_augs = {(h, c): jnp.concatenate([v_ref[rs(c), h * DV:(h + 1) * DV], ones_blk], axis=1)
              for h, c in units}
    scores = {u: _dot_nt(qs[u], ks[u]) for u in units}

    log2_scale = math.log2(scale)
    gate = {}
    m_prev = {h: m_ref[h, 0:1, 0:1] for h in hs}
    for h, c in units:
        ig = gates_ref[h:h + 1, rs(c)] * LOG2E
        fg = gates_ref[H + h:H + h + 1, rs(c)]
        logf = (jnp.minimum(fg, 0.0) - jnp.log1p(jnp.exp(-jnp.abs(fg)))) * LOG2E
        bcum = jnp.dot(jnp.broadcast_to(logf, (8, L)), upper_incl,
                       precision=lax.Precision.HIGHEST, preferred_element_type=F32)[0:1, :]
        bcum_s = jnp.broadcast_to(bcum, (L, L))
        bcum_t = bcum_s.T
        dmat = jnp.where(causal, bcum_t - bcum_s + ig, -jnp.inf)
        inter = bcum_t[:, 0:1] + m_prev[h]
        m_t = jnp.maximum(inter, jnp.max(dmat, axis=-1, keepdims=True))
        m_s = m_t - log2_scale
        g_tot = bcum[:, L - 1:L]
        a_end = g_tot - bcum + ig
        m_new = jnp.maximum(g_tot + m_prev[h], jnp.max(a_end, axis=-1, keepdims=True))
        gate[h, c] = dict(m_t=m_t, inter_w=jnp.exp2(inter - m_s), e=jnp.exp2(dmat - m_s),
                          w_row=jnp.exp2(a_end - m_new), decay=jnp.exp2(g_tot + m_prev[h] - m_new))
        m_prev[h] = m_new

    ps = {u: (scores[u] * gate[u]["e"]).astype(BF16) for u in units}
    ktws = {u: (ks[u].astype(F32).T * gate[u]["w_row"]).astype(BF16) for u in units}
    states = {h: state_ref[h] for h in hs}
    nums = {}
    for c in cs:
        for h in hs:
            nums[h, c] = (_dot(ps[h, c], v_augs[h, c])
                          + gate[h, c]["inter_w"] * _dot(qs[h, c], states[h].astype(BF16)))
        for h in hs:
            states[h] = gate[h, c]["decay"] * states[h] + _dot(ktws[h, c], v_augs[h, c])
    for h in hs:
        state_ref[h] = states[h]
        m_ref[h] = jnp.broadcast_to(m_prev[h], m_ref.shape[1:])
    for h, c in units:
        nq = nums[h, c][:, DV:DV + 1]
        den = jnp.maximum(jnp.abs(nq), jnp.exp2(-gate[h, c]["m_t"]))
        hval = nums[h, c][:, :DV] / den
        hn = hval * _rms_scale(hval) * gain_ref[:, h * DV:(h + 1) * DV]
        og = o_ref[rs(c), h * DV:(h + 1) * DV].astype(F32)
        out_ref[rs(c), h * DV:(h + 1) * DV] = (hn * _sigmoid(og)).astype(BF16)


def mlstm_core(proj, gates, out_gain, *, batch, seq):
    n = proj.shape[0]
    L = MLSTM_ROWS
    assert L % MLSTM_CHUNK == 0
    nc = seq // L
    H, DK, DV = MLSTM_HEADS, MLSTM_DQK, MLSTM_DV
    qw, vw = H * DK, H * DV
    assert 2 * qw == vw
    rows = lambda b, c: b * nc + c
    return pl.pallas_call(
        _mlstm_core_kernel,
        out_shape=jax.ShapeDtypeStruct((n, vw), BF16),
        grid=(batch, nc),
        in_specs=[pl.BlockSpec((L, qw), lambda b, c: (rows(b, c), 0)),
                  pl.BlockSpec((L, qw), lambda b, c: (rows(b, c), 1)),
                  pl.BlockSpec((L, vw), lambda b, c: (rows(b, c), 1)),
                  pl.BlockSpec((L, vw), lambda b, c: (rows(b, c), 2)),
                  pl.BlockSpec((8, L), lambda b, c: (0, rows(b, c))),
                  _const_spec((1, vw))],
        out_specs=pl.BlockSpec((L, vw), lambda b, c: (rows(b, c), 0)),
        scratch_shapes=[pltpu.VMEM((H, DK, DV + LANES), F32), pltpu.VMEM((H, 8, LANES), F32)],
        compiler_params=_params("parallel", "arbitrary"),
        name="mlstm_core",
    )(proj, proj, proj, proj, gates, out_gain)


def _silu_mul(g, u):
    return g * _sigmoid(g) * u


def _mix_ffn_kernel(x_ref, h_ref, wo_ref, g_ref, wgu_ref, wd_ref, *rest, f_chunk, n_riders):
    rider_in, out_ref, rider_out = rest[:n_riders], rest[n_riders], rest[n_riders + 1:]
    d_ff = wd_ref.shape[0]
    x1 = x_ref[...] + _dot(h_ref[...], wo_ref[...])
    hn = (x1 * _rms_scale(x1) * g_ref[...]).astype(BF16)
    acc = jnp.zeros_like(x1)
    for c in range(d_ff // f_chunk):
        lo = c * f_chunk
        g = _dot(hn, wgu_ref[:, lo:lo + f_chunk])
        u = _dot(hn, wgu_ref[:, d_ff + lo:d_ff + lo + f_chunk])
        acc = acc + _dot(_silu_mul(g, u).astype(BF16), wd_ref[lo:lo + f_chunk, :])
    out_ref[...] = x1 + acc
    for src, dst in zip(rider_in, rider_out):
        dst[...] = src[...].astype(BF16)


def mix_ffn(x, h, w_out, gain, w_gu, w_down, riders=(), *, tm=512, f_chunk=256):
    n, d = x.shape
    d_ff = w_down.shape[0]
    steps = n // tm
    assert d_ff % f_chunk == 0
    slabs = []
    for w in riders:
        rows = w.size // (steps * w.shape[-1])
        assert rows * steps * w.shape[-1] == w.size and rows % 16 == 0 and w.shape[-2] % rows == 0
        slabs.append(w.reshape(steps, rows, w.shape[-1]))
    slab_spec = lambda s: pl.BlockSpec((1,) + s.shape[1:], lambda i: (i, 0, 0))
    outs = pl.pallas_call(
        partial(_mix_ffn_kernel, f_chunk=f_chunk, n_riders=len(slabs)),
        out_shape=(jax.ShapeDtypeStruct((n, d), F32),) + tuple(jax.ShapeDtypeStruct(s.shape, BF16) for s in slabs),
        grid=(steps,),
        in_specs=[pl.BlockSpec((tm, d), lambda i: (i, 0)), pl.BlockSpec((tm, h.shape[1]), lambda i: (i, 0)),
                  _const_spec(w_out.shape), _const_spec((1, d)), _const_spec(w_gu.shape),
                  _const_spec(w_down.shape)] + [slab_spec(s) for s in slabs],
        out_specs=(pl.BlockSpec((tm, d), lambda i: (i, 0)),) + tuple(slab_spec(s) for s in slabs),
        compiler_params=_params("parallel"),
        name="mix_ffn",
    )(x, h, w_out, gain, w_gu, w_down, *slabs)
    return (outs[0],) + tuple(o.reshape(w.shape) for o, w in zip(outs[1:], riders))


def _rope(t, cos, sin_lo, sin_hi):
    half = ROT_DIM // 2
    return t * cos + pltpu.roll(t, LANES - half, 1) * sin_lo + pltpu.roll(t, half, 1) * sin_hi


def _kvq_kernel(x_ref, gkv_ref, gq_ref, wkv_ref, wq_ref, cos_ref, slo_ref, shi_ref,
                q_ref, k_ref, v_ref, km_ref, *, blocks_per_seq):
    tm = x_ref.shape[0]
    nsub = tm // MOBA_BLOCK
    G, DH = ATT_KV_HEADS, HEAD_DIM
    x = x_ref[...]
    xs = x * _rms_scale(x)
    cos, slo, shi = cos_ref[...], slo_ref[...], shi_ref[...]

    kv = _dot((xs * gkv_ref[...]).astype(BF16), wkv_ref[...])
    for g in range(G):
        v_ref[:, 2 * g * DH:(2 * g + 1) * DH] = kv[:, (G + g) * DH:(G + g + 1) * DH].astype(BF16)
        v_ref[:, (2 * g + 1) * DH:(2 * g + 2) * DH] = jnp.ones((tm, DH), BF16)
    first_blk = (pl.program_id(0) * nsub) % blocks_per_seq
    blk = first_blk + lax.broadcasted_iota(jnp.int32, (tm, LANES), 0) // MOBA_BLOCK
    ind = (lax.broadcasted_iota(jnp.int32, (tm, LANES), 1) == blk).astype(BF16)
    for g in range(G):
        kg = _rope(kv[:, g * DH:(g + 1) * DH], cos, slo, shi)
        k_ref[:, 2 * g * DH:(2 * g + 1) * DH] = kg.astype(BF16)
        k_ref[:, (2 * g + 1) * DH:(2 * g + 2) * DH] = ind
        for s in range(nsub):
            km_ref[0, s:s + 1, g * DH:(g + 1) * DH] = jnp.mean(
                kg[s * MOBA_BLOCK:(s + 1) * MOBA_BLOCK, :], axis=0, keepdims=True)

    q = _dot((xs * gq_ref[...]).astype(BF16), wq_ref[...])
    for h in range(ATT_HEADS):
        q_ref[:, h * DH:(h + 1) * DH] = _rope(q[:, h * DH:(h + 1) * DH], cos, slo, shi).astype(BF16)


def kvq_proj(x, g_kv, g_q, w_kv, w_q, cos, sin_lo, sin_hi, *, seq, tm=1024):
    n, d = x.shape
    G, DH = ATT_KV_HEADS, HEAD_DIM
    nsub = tm // MOBA_BLOCK
    tiles_per_seq = seq // tm
    tab = pl.BlockSpec((tm, LANES), lambda i: (i % tiles_per_seq, 0))
    return pl.pallas_call(
        partial(_kvq_kernel, blocks_per_seq=seq // MOBA_BLOCK),
        out_shape=(jax.ShapeDtypeStruct((n, ATT_HEADS * DH), BF16),
                   jax.ShapeDtypeStruct((n, 2 * G * DH), BF16),
                   jax.ShapeDtypeStruct((n, 2 * G * DH), BF16),
                   jax.ShapeDtypeStruct((n // tm, nsub, G * DH), F32)),
        grid=(n // tm,),
        in_specs=[pl.BlockSpec((tm, d), lambda i: (i, 0)), _const_spec((1, d)), _const_spec((1, d)),
                  _const_spec(w_kv.shape), _const_spec(w_q.shape), tab, tab, tab],
        out_specs=(pl.BlockSpec((tm, ATT_HEADS * DH), lambda i: (i, 0)),
                   pl.BlockSpec((tm, 2 * G * DH), lambda i: (i, 0)),
                   pl.BlockSpec((tm, 2 * G * DH), lambda i: (i, 0)),
                   pl.BlockSpec((1, nsub, G * DH), lambda i: (i, 0, 0))),
        compiler_params=_params("parallel"),
        name="kvq_proj",
    )(x, g_kv, g_q, w_kv, w_q, cos, sin_lo, sin_hi)


def _moba_kernel(q_ref, k_ref, v_ref, km_ref, out_ref, m_ref, acc_ref):
    QB, DH = MOBA_BLOCK, HEAD_DIM
    hpg = ATT_HEADS // ATT_KV_HEADS
    R = hpg * QB
    scale = DH ** -0.5
    j = pl.program_id(2)

    q = jnp.concatenate([q_ref[:, h * DH:(h + 1) * DH] for h in range(hpg)], axis=0)

    exp_scale = scale * LOG2E
    wide = lambda a: jnp.concatenate([a] * (QB // LANES), axis=1)
    groups = [slice(i * R // ROW_GROUPS, (i + 1) * R // ROW_GROUPS) for i in range(ROW_GROUPS)]

    k_own = k_ref[pl.ds(pl.multiple_of(j * QB, QB), QB), :]
    v_own = v_ref[pl.ds(pl.multiple_of(j * QB, QB), QB), :]
    own_scores = [_dot_nt(q[rs], k_own[:, :DH]) for rs in groups]

    nb = km_ref.shape[0]
    km_hi, km_lo = _split_bf16(km_ref[...])
    g2 = _dot_nt(jnp.concatenate([km_hi, km_lo], axis=0), q)
    gate = g2[:nb] + g2[nb:]
    blk = lax.broadcasted_iota(jnp.int32, (nb, R), 0)
    beats = jnp.zeros((nb, R), jnp.int32)
    for mb in range(nb):
        gm = gate[mb:mb + 1, :]
        ahead = (gm > gate) | ((gm == gate) & (mb < blk))
        beats = beats + jnp.where(ahead, 1, 0) * (mb < j).astype(jnp.int32)
    selected = jnp.where((blk < j) & (beats < MOBA_TOPK), 1.0, 0.0).astype(BF16)
    eye = jnp.where(lax.broadcasted_iota(jnp.int32, (nb, LANES), 0)
                    == lax.broadcasted_iota(jnp.int32, (nb, LANES), 1), 1.0, 0.0).astype(BF16)
    sel_rows = lax.dot_general(selected, eye, (((0,), (0,)), ((), ())), preferred_element_type=F32)
    pen = ((1.0 - sel_rows) * NEG_BIG).astype(BF16)
    q_aug = jnp.concatenate([q, pen], axis=1)

    m0s, p0s = [], []
    for rs, s in zip(groups, own_scores):
        rows = rs.stop - rs.start
        t_pos = (lax.broadcasted_iota(jnp.int32, (rows, QB), 0) + rs.start) % QB
        s_pos = lax.broadcasted_iota(jnp.int32, (rows, QB), 1)
        s = jnp.where(s_pos <= t_pos, s, -jnp.inf)
        m0 = jnp.max(s, axis=-1, keepdims=True)
        m0s.append(m0)
        p0s.append(jnp.exp2((s - m0) * exp_scale).astype(BF16))
    for rs, m0, p0 in zip(groups, m0s, p0s):
        m_ref[rs, :] = jnp.broadcast_to(m0, (rs.stop - rs.start, LANES))
        acc_ref[rs, :] = _dot(p0, v_own)

    def attend(first_blk, n_blk):
        start = pl.multiple_of(first_blk * QB, QB)
        k_n = k_ref[pl.ds(start, n_blk * QB), :]
        v_n = v_ref[pl.ds(start, n_blk * QB), :]
        ss = [_dot_nt(q_aug[rs], k_n) for rs in groups]
        m_olds = [m_ref[rs, :] for rs in groups]
        m_news = [jnp.maximum(mo, jnp.max(s, axis=-1, keepdims=True)) for mo, s in zip(m_olds, ss)]
        ps = [jnp.exp2((s - jnp.concatenate([mn] * (n_blk * QB // LANES), axis=1)) * exp_scale).astype(BF16)
              for s, mn in zip(ss, m_news)]
        pvs = [_dot(p, v_n) for p in ps]
        for rs, mo, mn, pv in zip(groups, m_olds, m_news, pvs):
            alpha = jnp.exp2((mo - mn) * exp_scale)
            acc_ref[rs, :] = wide(alpha) * acc_ref[rs, :] + pv
            m_ref[rs, :] = mn

    def body(i, carry):
        attend(2 * i, 2)
        return carry

    n_pairs = lax.shift_right_logical(j, 1)
    lax.fori_loop(0, n_pairs, body, 0)

    @pl.when(j % 2 == 1)
    def _():
        attend(j - 1, 1)

    acc = acc_ref[...]
    out = acc[:, :DH] / acc[:, DH:]
    for h in range(hpg):
        out_ref[:, h * DH:(h + 1) * DH] = out[h * QB:(h + 1) * QB, :].astype(BF16)


def moba_attn(q, k_aug, v, k_means, *, batch, seq):
    n = q.shape[0]
    G, DH, QB = ATT_KV_HEADS, HEAD_DIM, MOBA_BLOCK
    hpg = ATT_HEADS // G
    nb = seq // QB
    assert DH == LANES and nb <= LANES
    R = hpg * QB
    return pl.pallas_call(
        _moba_kernel,
        out_shape=jax.ShapeDtypeStruct((n, ATT_HEADS * DH), BF16),
        grid=(batch, G, nb),
        in_specs=[pl.BlockSpec((QB, hpg * DH), lambda b, g, j: (b * nb + j, g)),
                  pl.BlockSpec((seq, 2 * DH), lambda b, g, j: (b, g)),
                  pl.BlockSpec((seq, 2 * DH), lambda b, g, j: (b, g)),
                  pl.BlockSpec((nb, DH), lambda b, g, j: (b, g))],
        out_specs=pl.BlockSpec((QB, hpg * DH), lambda b, g, j: (b * nb + j, g)),
        scratch_shapes=[pltpu.VMEM((R, LANES), F32), pltpu.VMEM((R, 2 * DH), F32)],
        compiler_params=_params("parallel", "parallel", "arbitrary"),
        name="moba_attn",
    )(q, k_aug, v, k_means)


def _oproj_route_kernel(x_ref, a_ref, wo_ref, g_ref, r_ref, x_out_ref, hn_ref, sel_ref, comb_ref):
    x3 = x_ref[...] + _dot(a_ref[...], wo_ref[...])
    x_out_ref[...] = x3
    hn = x3 * _rms_scale(x3) * g_ref[...]
    hn_hi, hn_lo = _split_bf16(hn)
    hn_ref[...] = hn_hi
    logits = _dot_nt_split(r_ref[...], hn_hi, hn_lo)
    ne = logits.shape[0]
    idx = lax.broadcasted_iota(jnp.int32, logits.shape, 0)
    m1 = jnp.max(logits, axis=0, keepdims=True)
    i1 = jnp.min(jnp.where(logits == m1, idx, ne), axis=0, keepdims=True)
    first = idx == i1
    rest = jnp.where(first, -jnp.inf, logits)
    m2 = jnp.max(rest, axis=0, keepdims=True)
    i2 = jnp.min(jnp.where(rest == m2, idx, ne), axis=0, keepdims=True)
    second = idx == i2
    e = jnp.exp(m2 - m1)
    sel_ref[...] = jnp.where(first | second, 1.0, 0.0)
    comb_ref[...] = jnp.where(first, 1.0 / (1.0 + e), 0.0) + jnp.where(second, e / (1.0 + e), 0.0)


def oproj_route(x, att, w_o, gain, router_t, *, tm=1024):
    n, d = x.shape
    ne = router_t.shape[0]
    row = lambda w: pl.BlockSpec((tm, w), lambda i: (i, 0))
    lane_major = pl.BlockSpec((ne, tm), lambda i: (0, i))
    return pl.pallas_call(
        _oproj_route_kernel,
        out_shape=(jax.ShapeDtypeStruct((n, d), F32), jax.ShapeDtypeStruct((n, d), BF16),
                   jax.ShapeDtypeStruct((ne, n), F32), jax.ShapeDtypeStruct((ne, n), F32)),
        grid=(n // tm,),
        in_specs=[row(d), row(att.shape[1]), _const_spec(w_o.shape), _const_spec((1, d)),
                  _const_spec(router_t.shape)],
        out_specs=(row(d), row(d), lane_major, lane_major),
        compiler_params=_params("parallel"),
        name="oproj_route",
    )(x, att, w_o, gain, router_t)


TOK_BLOCK = 256
SEG_ALIGN = 8
BIG_CHUNK = 32
MOE_TM = 512

def _loc_rows(ne):
    need = 2 * TOK_BLOCK + ne * (SEG_ALIGN - 1)
    return -(-need // LANES) * LANES


def _dispatch_plan(sel_t, n_tiles):
    ne, n = sel_t.shape
    nblk = n // TOK_BLOCK
    i32 = jnp.int32
    cnt = sel_t.reshape(ne, nblk, TOK_BLOCK).sum(-1).astype(i32)
    pad = (cnt + SEG_ALIGN - 1) // SEG_ALIGN * SEG_ALIGN
    loff = jnp.cumsum(pad, axis=0) - pad
    tot = pad.sum(axis=1)
    reg = (tot + MOE_TM - 1) // MOE_TM * MOE_TM
    reg_start = jnp.cumsum(reg) - reg
    goff = reg_start[:, None] + jnp.cumsum(pad, axis=1) - pad
    tiles_end = jnp.cumsum(reg // MOE_TM)
    n_valid = tiles_end[-1]
    t = jnp.arange(n_tiles, dtype=i32)
    t_eff = jnp.minimum(t, n_valid - 1)
    tile_expert = (tiles_end[None, :] <= t_eff[:, None]).sum(axis=1).astype(i32)
    tile_valid = (t < n_valid).astype(i32)
    used = reg.sum()
    tail = jnp.concatenate([reg_start + tot, used[None], (reg - tot) // SEG_ALIGN,
                            ((n_tiles * MOE_TM - used) // SEG_ALIGN)[None]]).astype(i32)
    flat = lambda a: a.reshape(-1).astype(i32)
    return flat(pad), flat(loff), flat(goff), tail, tile_expert, t_eff.astype(i32), tile_valid


def _block_rows(pad_ref, blk, nblk, ne):
    tot = pad_ref[blk]
    for e in range(1, ne):
        tot = tot + pad_ref[e * nblk + blk]
    return tot


def _segment_copies(pad_ref, loff_ref, goff_ref, blk, nblk, ne, start_one):
    per_big = BIG_CHUNK // SEG_ALIGN
    for e in range(ne):
        n_small = lax.shift_right_logical(pad_ref[e * nblk + blk], 3)
        n_big = lax.div(n_small, per_big)
        lo, go = loff_ref[e * nblk + blk], goff_ref[e * nblk + blk]

        def big(i, carry, lo=lo, go=go):
            start_one(pl.multiple_of(lo + i * BIG_CHUNK, SEG_ALIGN), pl.multiple_of(go + i * BIG_CHUNK, SEG_ALIGN),
                      BIG_CHUNK)
            return carry

        def small(i, carry, lo=lo, go=go):
            start_one(pl.multiple_of(lo + i * SEG_ALIGN, SEG_ALIGN), pl.multiple_of(go + i * SEG_ALIGN, SEG_ALIGN),
                      SEG_ALIGN)
            return carry

        lax.fori_loop(0, n_big, big, 0)
        lax.fori_loop(n_big * per_big, n_small, small, 0)


def _local_positions(sel, loff_ref, blk, nblk):
    ne, tb = sel.shape
    sub = lax.broadcasted_iota(jnp.int32, (ne, tb), 0)
    row = lax.broadcasted_iota(jnp.int32, (tb, tb), 0)
    col = lax.broadcasted_iota(jnp.int32, (tb, tb), 1)
    before = (row < col).astype(BF16)
    rank = _dot(sel.astype(BF16), before)
    sub1 = lax.broadcasted_iota(jnp.int32, (ne, 1), 0)
    loff = jnp.zeros((ne, 1), F32)
    for e in range(ne):
        loff = jnp.where(sub1 == e, loff_ref[e * nblk + blk].astype(F32), loff)
    chosen = sel > 0.0
    e_lo = jnp.min(jnp.where(chosen, sub, ne), axis=0, keepdims=True)
    e_hi = jnp.max(jnp.where(chosen, sub, -1), axis=0, keepdims=True)
    return loff + rank, sub == e_lo, sub == e_hi


def _pick(a, mask):
    return jnp.sum(jnp.where(mask, a, 0.0), axis=0, keepdims=True)


def _dispatch_kernel(pad_ref, loff_ref, goff_ref, tail_ref, hn_ref, sel_ref, xs_ref, buf_ref, zero_ref,
                     sem, tail_sem, *, nblk, ne):
    b = pl.program_id(0)
    slot = b % 2
    rows = buf_ref.shape[1]

    def wait_block(blk, s):
        tot = pl.multiple_of(_block_rows(pad_ref, blk, nblk, ne), SEG_ALIGN)
        pltpu.make_async_copy(buf_ref.at[s, pl.ds(0, tot)], xs_ref.at[pl.ds(0, tot)], sem.at[s]).wait()

    @pl.when(b >= 2)
    def _():
        wait_block(b - 2, slot)

    lpos, lo_mask, hi_mask = _local_positions(sel_ref[...], loff_ref, b, nblk)
    p1 = _pick(lpos, lo_mask).astype(jnp.int32)
    p2 = _pick(lpos, hi_mask).astype(jnp.int32)
    r = lax.broadcasted_iota(jnp.int32, (rows, TOK_BLOCK), 0)
    perm = jnp.where((r == p1) | (r == p2), 1.0, 0.0).astype(BF16)
    buf_ref[slot] = _dot(perm, hn_ref[...])

    def start_one(lrow, grow, n):
        pltpu.make_async_copy(buf_ref.at[slot, pl.ds(lrow, n)], xs_ref.at[pl.ds(grow, n)], sem.at[slot]).start()

    _segment_copies(pad_ref, loff_ref, goff_ref, b, nblk, ne, start_one)

    @pl.when(b == nblk - 1)
    def _():
        zero_ref[...] = jnp.zeros_like(zero_ref)
        n_span = ne + 1
        for sp in range(n_span):
            def start_zero(i, carry, sp=sp):
                grow = pl.multiple_of(tail_ref[sp] + i * SEG_ALIGN, SEG_ALIGN)
                pltpu.make_async_copy(zero_ref, xs_ref.at[pl.ds(grow, SEG_ALIGN)], tail_sem).start()
                return carry

            lax.fori_loop(0, tail_ref[n_span + sp], start_zero, 0)

        for sp in range(n_span):
            def wait_zero(i, carry):
                pltpu.make_async_copy(zero_ref, xs_ref.at[pl.ds(0, SEG_ALIGN)], tail_sem).wait()
                return carry

            lax.fori_loop(0, tail_ref[n_span + sp], wait_zero, 0)

        wait_block(b, slot)
        if nblk > 1:
            wait_block(b - 1, 1 - slot)


def moe_dispatch(plan, hn, sel_t, *, n_tiles):
    pad, loff, goff, tail = plan[:4]
    n, d = hn.shape
    ne = sel_t.shape[0]
    nblk = n // TOK_BLOCK
    rows = _loc_rows(ne)
    grid_spec = pltpu.PrefetchScalarGridSpec(
        num_scalar_prefetch=4,
        grid=(nblk,),
        in_specs=[pl.BlockSpec((TOK_BLOCK, d), lambda b, *_: (b, 0)),
                  pl.BlockSpec((ne, TOK_BLOCK), lambda b, *_: (0, b))],
        out_specs=pl.BlockSpec(memory_space=pltpu.HBM),
        scratch_shapes=[pltpu.VMEM((2, rows, d), F32), pltpu.VMEM((SEG_ALIGN, d), F32),
                        pltpu.SemaphoreType.DMA((2,)), pltpu.SemaphoreType.DMA(())],
    )
    return pl.pallas_call(
        partial(_dispatch_kernel, nblk=nblk, ne=ne),
        out_shape=jax.ShapeDtypeStruct((n_tiles * MOE_TM, d), F32),
        grid_spec=grid_spec,
        compiler_params=_params("arbitrary"),
        name="moe_dispatch",
    )(pad, loff, goff, tail, hn, sel_t)


def _expert_ffn_kernel(texp_ref, teff_ref, tvalid_ref, x_ref, wg_ref, wu_ref, wd_ref, out_ref, *, f_chunk):
    i = pl.program_id(0)

    @pl.when(tvalid_ref[i] == 1)
    def _():
        xb = x_ref[...].astype(BF16)
        y = None
        for lo in range(0, wd_ref.shape[1], f_chunk):
            act = _silu_mul(_dot(xb, wg_ref[0, :, lo:lo + f_chunk]), _dot(xb, wu_ref[0, :, lo:lo + f_chunk]))
            part = _dot(act.astype(BF16), wd_ref[0, lo:lo + f_chunk, :])
            y = part if y is None else y + part
        out_ref[...] = y

    @pl.when(tvalid_ref[i] == 0)
    def _():
        out_ref[...] = jnp.zeros_like(out_ref)


def expert_ffn(plan, xs, w_gu, w_down, *, f_chunk=1792):
    tile_expert, tile_eff, tile_valid = plan[4:]
    n_rows, d = xs.shape
    d_exp = w_down.shape[1]
    assert d_exp % f_chunk == 0 and f_chunk % LANES == 0
    n_tiles = n_rows // MOE_TM
    grid_spec = pltpu.PrefetchScalarGridSpec(
        num_scalar_prefetch=3,
        grid=(n_tiles,),
        in_specs=[pl.BlockSpec((MOE_TM, d), lambda i, te, tf, tv: (tf[i], 0)),
                  pl.BlockSpec((1, d, d_exp), lambda i, te, tf, tv: (te[i], 0, 0)),
                  pl.BlockSpec((1, d, d_exp), lambda i, te, tf, tv: (te[i], 0, 1)),
                  pl.BlockSpec((1, d_exp, d), lambda i, te, tf, tv: (te[i], 0, 0), pipeline_mode=pl.Buffered(1))],
        out_specs=pl.BlockSpec((MOE_TM, d), lambda i, te, tf, tv: (i, 0)),
    )
    return pl.pallas_call(
        partial(_expert_ffn_kernel, f_chunk=f_chunk),
        out_shape=jax.ShapeDtypeStruct((n_rows, d), F32),
        grid_spec=grid_spec,
        compiler_params=_params("arbitrary"),
        name="expert_ffn",
    )(tile_expert, tile_eff, tile_valid, xs, w_gu, w_gu, w_down)


def _combine_kernel(pad_ref, loff_ref, goff_ref, ys_ref, x_ref, sel_ref, comb_ref, g_ref, out_ref, buf_ref, sem,
                    *, nblk, ne):
    b = pl.program_id(0)
    slot = b % 2
    rows = buf_ref.shape[1]

    def fetch_block(blk, s):
        def start_one(lrow, grow, n):
            pltpu.make_async_copy(ys_ref.at[pl.ds(grow, n)], buf_ref.at[s, pl.ds(lrow, n)], sem.at[s]).start()

        _segment_copies(pad_ref, loff_ref, goff_ref, blk, nblk, ne, start_one)

    @pl.when(b == 0)
    def _():
        buf_ref[...] = jnp.zeros_like(buf_ref)
        fetch_block(0, 0)

    @pl.when(b + 1 < nblk)
    def _():
        fetch_block(b + 1, 1 - slot)

    tot = pl.multiple_of(_block_rows(pad_ref, b, nblk, ne), SEG_ALIGN)
    pltpu.make_async_copy(ys_ref.at[pl.ds(0, tot)], buf_ref.at[slot, pl.ds(0, tot)], sem.at[slot]).wait()

    lpos, lo_mask, hi_mask = _local_positions(sel_ref[...], loff_ref, b, nblk)
    comb = comb_ref[...]
    p1, p2 = _pick(lpos, lo_mask), _pick(lpos, hi_mask)
    w1, w2 = _pick(comb, lo_mask), _pick(comb, hi_mask)
    r = lax.broadcasted_iota(jnp.int32, (rows, TOK_BLOCK), 0)
    w_rows = jnp.sum(jnp.where(r == p1.astype(jnp.int32), w1, 0.0) + jnp.where(r == p2.astype(jnp.int32), w2, 0.0),
                     axis=1, keepdims=True)
    local = (buf_ref[slot] * w_rows).astype(BF16)
    hi1, hi2 = jnp.floor(p1 * (1.0 / 32.0)), jnp.floor(p2 * (1.0 / 32.0))
    info = jnp.concatenate([hi1, p1 - 32.0 * hi1, hi2, p2 - 32.0 * hi2, jnp.zeros((4, TOK_BLOCK), F32)], axis=0)
    eye = jnp.where(lax.broadcasted_iota(jnp.int32, (TOK_BLOCK, TOK_BLOCK), 0)
                    == lax.broadcasted_iota(jnp.int32, (TOK_BLOCK, TOK_BLOCK), 1), 1.0, 0.0).astype(BF16)
    cols = _dot_nt(eye, info.astype(BF16))
    c1 = (32.0 * cols[:, 0:1] + cols[:, 1:2]).astype(jnp.int32)
    c2 = (32.0 * cols[:, 2:3] + cols[:, 3:4]).astype(jnp.int32)
    c = lax.broadcasted_iota(jnp.int32, (TOK_BLOCK, rows), 1)
    unsort = jnp.where((c == c1) | (c == c2), 1.0, 0.0).astype(BF16)
    y = x_ref[...] + _dot(unsort, local)
    out_ref[...] = y * _rms_scale(y) * g_ref[...]


def moe_combine(plan, ys, x, sel_t, comb_t, g_final):
    pad, loff, goff = plan[:3]
    n, d = x.shape
    ne = sel_t.shape[0]
    nblk = n // TOK_BLOCK
    rows = _loc_rows(ne)
    lane_major = pl.BlockSpec((ne, TOK_BLOCK), lambda b, *_: (0, b))
    grid_spec = pltpu.PrefetchScalarGridSpec(
        num_scalar_prefetch=3,
        grid=(nblk,),
        in_specs=[pl.BlockSpec(memory_space=pltpu.HBM),
                  pl.BlockSpec((TOK_BLOCK, d), lambda b, *_: (b, 0)), lane_major, lane_major,
                  pl.BlockSpec((1, d), lambda b, *_: (0, 0))],
        out_specs=pl.BlockSpec((TOK_BLOCK, d), lambda b, *_: (b, 0)),
        scratch_shapes=[pltpu.VMEM((2, rows, d), F32), pltpu.SemaphoreType.DMA((2,))],
    )
    return pl.pallas_call(
        partial(_combine_kernel, nblk=nblk, ne=ne),
        out_shape=jax.ShapeDtypeStruct((n, d), F32),
        grid_spec=grid_spec,
        compiler_params=_params("arbitrary"),
        name="moe_combine",
    )(pad, loff, goff, ys, x, sel_t, comb_t, g_final)


def moe_ffn(hn, sel_t, comb_t, w_gu, w_down, x, g_final):
    n = hn.shape[0]
    ne = sel_t.shape[0]
    nblk = n // TOK_BLOCK
    n_tiles = -(-(2 * n + nblk * ne * (SEG_ALIGN - 1)) // MOE_TM) + ne
    plan = _dispatch_plan(sel_t, n_tiles)
    xs = moe_dispatch(plan, hn, sel_t, n_tiles=n_tiles)
    ys = expert_ffn(plan, xs, w_gu, w_down)
    return moe_combine(plan, ys, x, sel_t, comb_t, g_final)


def _rope_tables(seq):
    half = ROT_DIM // 2
    pos = jnp.arange(seq, dtype=F32)
    inv_freq = ROPE_THETA ** (-jnp.arange(0, ROT_DIM, 2, dtype=F32) / ROT_DIM)
    ang = pos[:, None] * inv_freq[None, :]
    cos, sin = jnp.cos(ang), jnp.sin(ang)
    pad = jnp.zeros((seq, LANES - ROT_DIM), F32)
    zeros = jnp.zeros((seq, half), F32)
    cos_t = jnp.concatenate([cos, cos, pad + 1.0], axis=1)
    sin_lo = jnp.concatenate([-sin, zeros, pad], axis=1)
    sin_hi = jnp.concatenate([zeros, sin, pad], axis=1)
    return cos_t, sin_lo, sin_hi


def kernel(x, mlstm_norm, mlstm_w_in, mlstm_b_if, mlstm_out_norm, mlstm_w_out, ffn_norm, dense_w_gu,
           dense_w_down, moe_router, moe_w_gu, moe_w_down, kv_norm, kv_w, moba_norm, moba_w_q, moba_w_o,
           final_norm):
    batch, seq, d = x.shape
    n = batch * seq
    H = MLSTM_HEADS
    n_main = H * (2 * MLSTM_DQK + 2 * MLSTM_DV)
    xf = x.reshape(n, d)
    row = lambda g: g.reshape(1, -1).astype(F32)

    w_in = mlstm_w_in[0]
    proj, gates = mlstm_proj(xf, row(mlstm_norm[0]), w_in[:, :n_main].astype(BF16),
                             w_in[:, n_main:].T, mlstm_b_if[0].reshape(2 * H, 1))
    hg = mlstm_core(proj, gates, row(mlstm_out_norm[0]), batch=batch, seq=seq)
    x2, moe_gu_bf16, moe_down_bf16 = mix_ffn(xf, hg, mlstm_w_out[0].astype(BF16), row(ffn_norm[0]),
                                             dense_w_gu[0].astype(BF16), dense_w_down[0].astype(BF16),
                                             riders=(moe_w_gu[0], moe_w_down[0]))

    cos_t, sin_lo, sin_hi = _rope_tables(seq)
    q, k_aug, v, k_means = kvq_proj(x2, row(kv_norm), row(moba_norm[0]), kv_w.astype(BF16),
                                    moba_w_q[0].astype(BF16), cos_t, sin_lo, sin_hi, seq=seq)
    k_means = k_means.reshape(n // MOBA_BLOCK, ATT_KV_HEADS * HEAD_DIM)
    att = moba_attn(q, k_aug, v, k_means, batch=batch, seq=seq)
    x3, hn, sel_t, comb_t = oproj_route(x2, att, moba_w_o[0].astype(BF16), row(ffn_norm[1]), moe_router[0].T)
    out = moe_ffn(hn, sel_t, comb_t, moe_gu_bf16, moe_down_bf16, x3, row(final_norm))
    return out.reshape(batch, seq, d)
```

```python
import math
from functools import partial

import jax
import jax.numpy as jnp
from jax import lax
from jax.experimental import pallas as pl
from jax.experimental.pallas import tpu as pltpu

NORM_EPS = 1e-6

MLSTM_HEADS = 4
MLSTM_DQK = 128
MLSTM_DV = 256
MLSTM_CHUNK = 256
MLSTM_ROWS = 512

ATT_HEADS = 8
ATT_KV_HEADS = 2
HEAD_DIM = 128
MOBA_BLOCK = 256
MOBA_TOPK = 3
ROW_GROUPS = 4
ROPE_THETA = 500000.0
ROT_DIM = HEAD_DIM // 4

N_EXPERTS = 8

LANES = 128
NEG_BIG = -1e30
LOG2E = 1.4426950408889634
VMEM_LIMIT = 56 * 1024 * 1024

F32 = jnp.float32
BF16 = jnp.bfloat16

_NT = (((1,), (1,)), ((), ()))


def _dot(a, b):
    return jnp.dot(a, b, preferred_element_type=F32)


def _dot_nt(a, b, precision=None):
    return lax.dot_general(a, b, _NT, precision=precision, preferred_element_type=F32)


def _split_bf16(a):
    hi = a.astype(BF16)
    return hi, (a - hi.astype(F32)).astype(BF16)


def _dot_nt_split(w, x_hi, x_lo):
    r = w.shape[0]
    w_hi, w_lo = _split_bf16(w)
    a = _dot_nt(jnp.concatenate([w_hi, w_lo], axis=0), x_hi)
    return a[:r] + a[r:] + _dot_nt(w_hi, x_lo)


def _rms_scale(x):
    return lax.rsqrt(jnp.mean(x * x, axis=-1, keepdims=True) + NORM_EPS)


def _sigmoid(x):
    return 1.0 / (1.0 + jnp.exp(-x))


def _params(*sem):
    return pltpu.CompilerParams(dimension_semantics=sem, vmem_limit_bytes=VMEM_LIMIT)


def _const_spec(shape):
    nd = len(shape)
    return pl.BlockSpec(shape, lambda *_: (0,) * nd, pipeline_mode=pl.Buffered(1))


def _mlstm_proj_kernel(x_ref, g_ref, w_ref, wg_ref, b_ref, proj_ref, gates_ref, *, col_chunk):
    x = x_ref[...]
    xn = x * _rms_scale(x) * g_ref[...]
    xb, x_lo = _split_bf16(xn)
    for c in range(w_ref.shape[1] // col_chunk):
        sl = slice(c * col_chunk, (c + 1) * col_chunk)
        proj_ref[:, sl] = _dot(xb, w_ref[:, sl]).astype(BF16)
    gates_ref[...] = _dot_nt_split(wg_ref[...], xb, x_lo) + b_ref[...]


def mlstm_proj(x, gain, w_main, w_gate_t, bias, *, tm=1024, col_chunk=512):
    n, d = x.shape
    p = w_main.shape[1]
    return pl.pallas_call(
        partial(_mlstm_proj_kernel, col_chunk=col_chunk),
        out_shape=(jax.ShapeDtypeStruct((n, p), BF16), jax.ShapeDtypeStruct((8, n), F32)),
        grid=(n // tm,),
        in_specs=[pl.BlockSpec((tm, d), lambda i: (i, 0)),
                  _const_spec((1, d)), _const_spec((d, p)), _const_spec((8, d)), _const_spec((8, 1))],
        out_specs=(pl.BlockSpec((tm, p), lambda i: (i, 0)), pl.BlockSpec((8, tm), lambda i: (0, i))),
        compiler_params=_params("parallel"),
        name="mlstm_proj",
    )(x, gain, w_main, w_gate_t, bias)


def _mlstm_core_kernel(q_ref, k_ref, v_ref, o_ref, gates_ref, gain_ref, out_ref, state_ref, m_ref):
    L = MLSTM_CHUNK
    H, DK, DV = MLSTM_HEADS, MLSTM_DQK, MLSTM_DV
    scale = DK ** -0.5

    @pl.when(pl.program_id(1) == 0)
    def _():
        state_ref[...] = jnp.zeros_like(state_ref)
        m_ref[...] = jnp.zeros_like(m_ref)

    row = lax.broadcasted_iota(jnp.int32, (L, L), 0)
    col = lax.broadcasted_iota(jnp.int32, (L, L), 1)
    causal = col <= row
    upper_incl = (row <= col).astype(F32)
    ones_blk = jnp.ones((L, LANES), BF16)

    hs = range(H)
    cs = range(q_ref.shape[0] // L)
    units = [(h, c) for c in cs for h in hs]
    rs = lambda c: slice(c * L, (c + 1) * L)
    qs = {(h, c): q_ref[rs(c), h * DK:(h + 1) * DK] for h, c in units}
    ks = {(h, c): k_ref[rs(c), h * DK:(h + 1) * DK] for h, c in units}
    v_augs = {(h, c): jnp.concatenate([v_ref[rs(c), h * DV:(h + 1) * DV], ones_blk], axis=1)
              for h, c in units}
    scores = {u: _dot_nt(qs[u], ks[u]) for u in units}

    log2_scale = math.log2(scale)
    gate = {}
    m_prev = {h: m_ref[h, 0:1, 0:1] for h in hs}
    for h, c in units:
        ig = gates_ref[h:h + 1, rs(c)] * LOG2E
        fg = gates_ref[H + h:H + h + 1, rs(c)]
        logf = (jnp.minimum(fg, 0.0) - jnp.log1p(jnp.exp(-jnp.abs(fg)))) * LOG2E
        bcum = jnp.dot(jnp.broadcast_to(logf, (8, L)), upper_incl,
                       precision=lax.Precision.HIGHEST, preferred_element_type=F32)[0:1, :]
        bcum_s = jnp.broadcast_to(bcum, (L, L))
        bcum_t = bcum_s.T
        dmat = jnp.where(causal, bcum_t - bcum_s + ig, -jnp.inf)
        inter = bcum_t[:, 0:1] + m_prev[h]
        m_t = jnp.maximum(inter, jnp.max(dmat, axis=-1, keepdims=True))
        m_s = m_t - log2_scale
        g_tot = bcum[:, L - 1:L]
        a_end = g_tot - bcum + ig
        m_new = jnp.maximum(g_tot + m_prev[h], jnp.max(a_end, axis=-1, keepdims=True))
        gate[h, c] = dict(m_t=m_t, inter_w=jnp.exp2(inter - m_s), e=jnp.exp2(dmat - m_s),
                          w_row=jnp.exp2(a_end - m_new), decay=jnp.exp2(g_tot + m_prev[h] - m_new))
        m_prev[h] = m_new

    ps = {u: (scores[u] * gate[u]["e"]).astype(BF16) for u in units}
    ktws = {u: (ks[u].astype(F32).T * gate[u]["w_row"]).astype(BF16) for u in units}
    states = {h: state_ref[h] for h in hs}
    nums = {}
    for c in cs:
        for h in hs:
            nums[h, c] = (_dot(ps[h, c], v_augs[h, c])
                          + gate[h, c]["inter_w"] * _dot(qs[h, c], states[h].astype(BF16)))
        for h in hs:
            states[h] = gate[h, c]["decay"] * states[h] + _dot(ktws[h, c], v_augs[h, c])
    for h in hs:
        state_ref[h] = states[h]
        m_ref[h] = jnp.broadcast_to(m_prev[h], m_ref.shape[1:])
    for h, c in units:
        nq = nums[h, c][:, DV:DV + 1]
        den = jnp.maximum(jnp.abs(nq), jnp.exp2(-gate[h, c]["m_t"]))
        hval = nums[h, c][:, :DV] / den
        hn = hval * _rms_scale(hval) * gain_ref[:, h * DV:(h + 1) * DV]
        og = o_ref[rs(c), h * DV:(h + 1) * DV].astype(F32)
        out_ref[rs(c), h * DV:(h + 1) * DV] = (hn * _sigmoid(og)).astype(BF16)


def mlstm_core(proj, gates, out_gain, *, batch, seq):
    n = proj.shape[0]
    L = MLSTM_ROWS
    assert L % MLSTM_CHUNK == 0
    nc = seq // L
    H, DK, DV = MLSTM_HEADS, MLSTM_DQK, MLSTM_DV
    qw, vw = H * DK, H * DV
    assert 2 * qw == vw
    rows = lambda b, c: b * nc + c
    return pl.pallas_call(
        _mlstm_core_kernel,
        out_shape=jax.ShapeDtypeStruct((n, vw), BF16),
        grid=(batch, nc),
        in_specs=[pl.BlockSpec((L, qw), lambda b, c: (rows(b, c), 0)),
                  pl.BlockSpec((L, qw), lambda b, c: (rows(b, c), 1)),
                  pl.BlockSpec((L, vw), lambda b, c: (rows(b, c), 1)),
                  pl.BlockSpec((L, vw), lambda b, c: (rows(b, c), 2)),
                  pl.BlockSpec((8, L), lambda b, c: (0, rows(b, c))),
                  _const_spec((1, vw))],
        out_specs=pl.BlockSpec((L, vw), lambda b, c: (rows(b, c), 0)),
        scratch_shapes=[pltpu.VMEM((H, DK, DV + LANES), F32), pltpu.VMEM((H, 8, LANES), F32)],
        compiler_params=_params("parallel", "arbitrary"),
        name="mlstm_core",
    )(proj, proj, proj, proj, gates, out_gain)


def _silu_mul(g, u):
    return g * _sigmoid(g) * u


def _mix_ffn_kernel(x_ref, h_ref, wo_ref, g_ref, wgu_ref, wd_ref, *rest, f_chunk, n_riders):
    rider_in, out_ref, rider_out = rest[:n_riders], rest[n_riders], rest[n_riders + 1:]
    d_ff = wd_ref.shape[0]
    x1 = x_ref[...] + _dot(h_ref[...], wo_ref[...])
    hn = (x1 * _rms_scale(x1) * g_ref[...]).astype(BF16)
    acc = jnp.zeros_like(x1)
    for c in range(d_ff // f_chunk):
        lo = c * f_chunk
        g = _dot(hn, wgu_ref[:, lo:lo + f_chunk])
        u = _dot(hn, wgu_ref[:, d_ff + lo:d_ff + lo + f_chunk])
        acc = acc + _dot(_silu_mul(g, u).astype(BF16), wd_ref[lo:lo + f_chunk, :])
    out_ref[...] = x1 + acc
    for src, dst in zip(rider_in, rider_out):
        dst[...] = src[...].astype(BF16)


def mix_ffn(x, h, w_out, gain, w_gu, w_down, riders=(), *, tm=512, f_chunk=256):
    n, d = x.shape
    d_ff = w_down.shape[0]
    steps = n // tm
    assert d_ff % f_chunk == 0
    slabs = []
    for w in riders:
        rows = w.size // (steps * w.shape[-1])
        assert rows * steps * w.shape[-1] == w.size and rows % 16 == 0 and w.shape[-2] % rows == 0
        slabs.append(w.reshape(steps, rows, w.shape[-1]))
    slab_spec = lambda s: pl.BlockSpec((1,) + s.shape[1:], lambda i: (i, 0, 0))
    outs = pl.pallas_call(
        partial(_mix_ffn_kernel, f_chunk=f_chunk, n_riders=len(slabs)),
        out_shape=(jax.ShapeDtypeStruct((n, d), F32),) + tuple(jax.ShapeDtypeStruct(s.shape, BF16) for s in slabs),
        grid=(steps,),
        in_specs=[pl.BlockSpec((tm, d), lambda i: (i, 0)), pl.BlockSpec((tm, h.shape[1]), lambda i: (i, 0)),
                  _const_spec(w_out.shape), _const_spec((1, d)), _const_spec(w_gu.shape),
                  _const_spec(w_down.shape)] + [slab_spec(s) for s in slabs],
        out_specs=(pl.BlockSpec((tm, d), lambda i: (i, 0)),) + tuple(slab_spec(s) for s in slabs),
        compiler_params=_params("parallel"),
        name="mix_ffn",
    )(x, h, w_out, gain, w_gu, w_down, *slabs)
    return (outs[0],) + tuple(o.reshape(w.shape) for o, w in zip(outs[1:], riders))


def _rope(t, cos, sin_lo, sin_hi):
    half = ROT_DIM // 2
    return t * cos + pltpu.roll(t, LANES - half, 1) * sin_lo + pltpu.roll(t, half, 1) * sin_hi


def _kvq_kernel(x_ref, gkv_ref, gq_ref, wkv_ref, wq_ref, cos_ref, slo_ref, shi_ref,
                q_ref, k_ref, v_ref, km_ref, *, blocks_per_seq):
    tm = x_ref.shape[0]
    nsub = tm // MOBA_BLOCK
    G, DH = ATT_KV_HEADS, HEAD_DIM
    x = x_ref[...]
    xs = x * _rms_scale(x)
    cos, slo, shi = cos_ref[...], slo_ref[...], shi_ref[...]

    q = _dot((xs * gq_ref[...]).astype(BF16), wq_ref[...])
    kv = _dot((xs * gkv_ref[...]).astype(BF16), wkv_ref[...])
    for h in range(ATT_HEADS):
        q_ref[:, h * DH:(h + 1) * DH] = _rope(q[:, h * DH:(h + 1) * DH], cos, slo, shi).astype(BF16)
    for g in range(G):
        v_ref[:, 2 * g * DH:(2 * g + 1) * DH] = kv[:, (G + g) * DH:(G + g + 1) * DH].astype(BF16)
        v_ref[:, (2 * g + 1) * DH:(2 * g + 2) * DH] = jnp.ones((tm, DH), BF16)
    first_blk = (pl.program_id(0) * nsub) % blocks_per_seq
    blk = first_blk + lax.broadcasted_iota(jnp.int32, (tm, LANES), 0) // MOBA_BLOCK
    ind = (lax.broadcasted_iota(jnp.int32, (tm, LANES), 1) == blk).astype(BF16)
    for g in range(G):
        kg = _rope(kv[:, g * DH:(g + 1) * DH], cos, slo, shi)
        k_ref[:, 2 * g * DH:(2 * g + 1) * DH] = kg.astype(BF16)
        k_ref[:, (2 * g + 1) * DH:(2 * g + 2) * DH] = ind
        for s in range(nsub):
            km_ref[0, s:s + 1, g * DH:(g + 1) * DH] = jnp.mean(
                kg[s * MOBA_BLOCK:(s + 1) * MOBA_BLOCK, :], axis=0, keepdims=True)


def kvq_proj(x, g_kv, g_q, w_kv, w_q, cos, sin_lo, sin_hi, *, seq, tm=1024):
    n, d = x.shape
    G, DH = ATT_KV_HEADS, HEAD_DIM
    nsub = tm // MOBA_BLOCK
    tiles_per_seq = seq // tm
    tab = pl.BlockSpec((tm, LANES), lambda i: (i % tiles_per_seq, 0))
    return pl.pallas_call(
        partial(_kvq_kernel, blocks_per_seq=seq // MOBA_BLOCK),
        out_shape=(jax.ShapeDtypeStruct((n, ATT_HEADS * DH), BF16),
                   jax.ShapeDtypeStruct((n, 2 * G * DH), BF16),
                   jax.ShapeDtypeStruct((n, 2 * G * DH), BF16),
                   jax.ShapeDtypeStruct((n // tm, nsub, G * DH), F32)),
        grid=(n // tm,),
        in_specs=[pl.BlockSpec((tm, d), lambda i: (i, 0)), _const_spec((1, d)), _const_spec((1, d)),
                  _const_spec(w_kv.shape), _const_spec(w_q.shape), tab, tab, tab],
        out_specs=(pl.BlockSpec((tm, ATT_HEADS * DH), lambda i: (i, 0)),
                   pl.BlockSpec((tm, 2 * G * DH), lambda i: (i, 0)),
                   pl.BlockSpec((tm, 2 * G * DH), lambda i: (i, 0)),
                   pl.BlockSpec((1, nsub, G * DH), lambda i: (i, 0, 0))),
        compiler_params=_params("parallel"),
        name="kvq_proj",
    )(x, g_kv, g_q, w_kv, w_q, cos, sin_lo, sin_hi)


def _moba_kernel(q_ref, k_ref, v_ref, km_ref, out_ref, m_ref, acc_ref):
    QB, DH = MOBA_BLOCK, HEAD_DIM
    G = ATT_KV_HEADS
    hpg = ATT_HEADS // G
    R = hpg * QB
    scale = DH ** -0.5
    j = pl.program_id(1)
    nb = km_ref.shape[0]

    exp_scale = scale * LOG2E
    wide = lambda a: jnp.concatenate([a] * (QB // LANES), axis=1)
    row_groups = [slice(i * R // ROW_GROUPS, (i + 1) * R // ROW_GROUPS) for i in range(ROW_GROUPS)]
    units = [(g, rs, slice(g * R + rs.start, g * R + rs.stop)) for g in range(G) for rs in row_groups]
    kcols = lambda g: slice(2 * g * DH, (2 * g + 2) * DH)
    own = pl.ds(pl.multiple_of(j * QB, QB), QB)

    qs = [jnp.concatenate([q_ref[:, (g * hpg + h) * DH:(g * hpg + h + 1) * DH] for h in range(hpg)], axis=0)
          for g in range(G)]

    own_scores = [_dot_nt(qs[g][rs], k_ref[own, 2 * g * DH:(2 * g + 1) * DH]) for g, rs, _ in units]

    blk = lax.broadcasted_iota(jnp.int32, (nb, R), 0)
    eye = jnp.where(lax.broadcasted_iota(jnp.int32, (nb, LANES), 0)
                    == lax.broadcasted_iota(jnp.int32, (nb, LANES), 1), 1.0, 0.0).astype(BF16)
    q_augs = []
    for g in range(G):
        km_hi, km_lo = _split_bf16(km_ref[:, g * DH:(g + 1) * DH])
        g2 = _dot_nt(jnp.concatenate([km_hi, km_lo], axis=0), qs[g])
        gate = g2[:nb] + g2[nb:]
        beats = jnp.zeros((nb, R), jnp.int32)
        for mb in range(nb):
            gm = gate[mb:mb + 1, :]
            ahead = (gm > gate) | ((gm == gate) & (mb < blk))
            beats = beats + jnp.where(ahead, 1, 0) * (mb < j).astype(jnp.int32)
        selected = jnp.where((blk < j) & (beats < MOBA_TOPK), 1.0, 0.0).astype(BF16)
        sel_rows = lax.dot_general(selected, eye, (((0,), (0,)), ((), ())), preferred_element_type=F32)
        pen = ((1.0 - sel_rows) * NEG_BIG).astype(BF16)
        q_augs.append(jnp.concatenate([qs[g], pen], axis=1))

    m0s, p0s = [], []
    for (g, rs, _), s in zip(units, own_scores):
        rows = rs.stop - rs.start
        t_pos = (lax.broadcasted_iota(jnp.int32, (rows, QB), 0) + rs.start) % QB
        s_pos = lax.broadcasted_iota(jnp.int32, (rows, QB), 1)
        s = jnp.where(s_pos <= t_pos, s, -jnp.inf)
        m0 = jnp.max(s, axis=-1, keepdims=True)
        m0s.append(m0)
        p0s.append(jnp.exp2((s - m0) * exp_scale).astype(BF16))
    for (g, rs, ar), m0, p0 in zip(units, m0s, p0s):
        m_ref[ar, :] = jnp.broadcast_to(m0, (rs.stop - rs.start, LANES))
        acc_ref[ar, :] = _dot(p0, v_ref[own, kcols(g)])

    def attend(first_blk, n_blk):
        keys = pl.ds(pl.multiple_of(first_blk * QB, QB), n_blk * QB)
        ss = [_dot_nt(q_augs[g][rs], k_ref[keys, kcols(g)]) for g, rs, _ in units]
        m_olds = [m_ref[ar, :] for _, _, ar in units]
        m_news = [jnp.maximum(mo, jnp.max(s, axis=-1, keepdims=True)) for mo, s in zip(m_olds, ss)]
        ps = [jnp.exp2((s - jnp.concatenate([mn] * (n_blk * QB // LANES), axis=1)) * exp_scale).astype(BF16)
              for s, mn in zip(ss, m_news)]
        pvs = [_dot(p, v_ref[keys, kcols(g)]) for p, (g, _, _) in zip(ps, units)]
        for (_, _, ar), mo, mn, pv in zip(units, m_olds, m_news, pvs):
            alpha = jnp.exp2((mo - mn) * exp_scale)
            acc_ref[ar, :] = wide(alpha) * acc_ref[ar, :] + pv
            m_ref[ar, :] = mn

    def body(i, carry):
        attend(2 * i, 2)
        return carry

    n_pairs = lax.shift_right_logical(j, 1)
    lax.fori_loop(0, n_pairs, body, 0)

    @pl.when(j % 2 == 1)
    def _():
        attend(j - 1, 1)

    for h in range(ATT_HEADS):
        acc = acc_ref[h * QB:(h + 1) * QB, :]
        out_ref[:, h * DH:(h + 1) * DH] = (acc[:, :DH] / acc[:, DH:]).astype(BF16)


def moba_attn(q, k_aug, v, k_means, *, batch, seq):
    n = q.shape[0]
    G, DH, QB = ATT_KV_HEADS, HEAD_DIM, MOBA_BLOCK
    nb = seq // QB
    assert DH == LANES and nb <= LANES
    rows = ATT_HEADS * QB
    return pl.pallas_call(
        _moba_kernel,
        out_shape=jax.ShapeDtypeStruct((n, ATT_HEADS * DH), BF16),
        grid=(batch, nb),
        in_specs=[pl.BlockSpec((QB, ATT_HEADS * DH), lambda b, j: (b * nb + j, 0)),
                  pl.BlockSpec((seq, 2 * G * DH), lambda b, j: (b, 0)),
                  pl.BlockSpec((seq, 2 * G * DH), lambda b, j: (b, 0)),
                  pl.BlockSpec((nb, G * DH), lambda b, j: (b, 0))],
        out_specs=pl.BlockSpec((QB, ATT_HEADS * DH), lambda b, j: (b * nb + j, 0)),
        scratch_shapes=[pltpu.VMEM((rows, LANES), F32), pltpu.VMEM((rows, 2 * DH), F32)],
        compiler_params=_params("parallel", "arbitrary"),
        name="moba_attn",
    )(q, k_aug, v, k_means)


def _oproj_route_kernel(x_ref, a_ref, wo_ref, g_ref, r_ref, x_out_ref, hn_ref, sel_ref, comb_ref):
    x3 = x_ref[...] + _dot(a_ref[...], wo_ref[...])
    x_out_ref[...] = x3
    hn = x3 * _rms_scale(x3) * g_ref[...]
    hn_hi, hn_lo = _split_bf16(hn)
    hn_ref[...] = hn_hi
    logits = _dot_nt_split(r_ref[...], hn_hi, hn_lo)
    ne = logits.shape[0]
    idx = lax.broadcasted_iota(jnp.int32, logits.shape, 0)
    m1 = jnp.max(logits, axis=0, keepdims=True)
    i1 = jnp.min(jnp.where(logits == m1, idx, ne), axis=0, keepdims=True)
    first = idx == i1
    rest = jnp.where(first, -jnp.inf, logits)
    m2 = jnp.max(rest, axis=0, keepdims=True)
    i2 = jnp.min(jnp.where(rest == m2, idx, ne), axis=0, keepdims=True)
    second = idx == i2
    e = jnp.exp(m2 - m1)
    sel_ref[...] = jnp.where(first | second, 1.0, 0.0)
    comb_ref[...] = jnp.where(first, 1.0 / (1.0 + e), 0.0) + jnp.where(second, e / (1.0 + e), 0.0)


def oproj_route(x, att, w_o, gain, router_t, *, tm=1024):
    n, d = x.shape
    ne = router_t.shape[0]
    row = lambda w: pl.BlockSpec((tm, w), lambda i: (i, 0))
    lane_major = pl.BlockSpec((ne, tm), lambda i: (0, i))
    return pl.pallas_call(
        _oproj_route_kernel,
        out_shape=(jax.ShapeDtypeStruct((n, d), F32), jax.ShapeDtypeStruct((n, d), BF16),
                   jax.ShapeDtypeStruct((ne, n), F32), jax.ShapeDtypeStruct((ne, n), F32)),
        grid=(n // tm,),
        in_specs=[row(d), row(att.shape[1]), _const_spec(w_o.shape), _const_spec((1, d)),
                  _const_spec(router_t.shape)],
        out_specs=(row(d), row(d), lane_major, lane_major),
        compiler_params=_params("parallel"),
        name="oproj_route",
    )(x, att, w_o, gain, router_t)


TOK_BLOCK = 256
SEG_ALIGN = 8
BIG_CHUNK = 32
MOE_TM = 512

def _loc_rows(ne):
    need = 2 * TOK_BLOCK + ne * (SEG_ALIGN - 1)
    return -(-need // LANES) * LANES


def _dispatch_plan(sel_t, n_tiles):
    ne, n = sel_t.shape
    nblk = n // TOK_BLOCK
    i32 = jnp.int32
    cnt = sel_t.reshape(ne, nblk, TOK_BLOCK).sum(-1).astype(i32)
    pad = (cnt + SEG_ALIGN - 1) // SEG_ALIGN * SEG_ALIGN
    loff = jnp.cumsum(pad, axis=0) - pad
    tot = pad.sum(axis=1)
    reg = (tot + MOE_TM - 1) // MOE_TM * MOE_TM
    reg_start = jnp.cumsum(reg) - reg
    goff = reg_start[:, None] + jnp.cumsum(pad, axis=1) - pad
    tiles_end = jnp.cumsum(reg // MOE_TM)
    n_valid = tiles_end[-1]
    t = jnp.arange(n_tiles, dtype=i32)
    t_eff = jnp.minimum(t, n_valid - 1)
    tile_expert = (tiles_end[None, :] <= t_eff[:, None]).sum(axis=1).astype(i32)
    tile_valid = (t < n_valid).astype(i32)
    used = reg.sum()
    tail = jnp.concatenate([reg_start + tot, used[None], (reg - tot) // SEG_ALIGN,
                            ((n_tiles * MOE_TM - used) // SEG_ALIGN)[None]]).astype(i32)
    flat = lambda a: a.reshape(-1).astype(i32)
    return flat(pad), flat(loff), flat(goff), tail, tile_expert, t_eff.astype(i32), tile_valid


def _block_rows(pad_ref, blk, nblk, ne):
    tot = pad_ref[blk]
    for e in range(1, ne):
        tot = tot + pad_ref[e * nblk + blk]
    return tot


def _segment_copies(pad_ref, loff_ref, goff_ref, blk, nblk, ne, start_one):
    per_big = BIG_CHUNK // SEG_ALIGN
    for e in range(ne):
        n_small = lax.shift_right_logical(pad_ref[e * nblk + blk], 3)
        n_big = lax.div(n_small, per_big)
        lo, go = loff_ref[e * nblk + blk], goff_ref[e * nblk + blk]

        def big(i, carry, lo=lo, go=go):
            start_one(pl.multiple_of(lo + i * BIG_CHUNK, SEG_ALIGN), pl.multiple_of(go + i * BIG_CHUNK, SEG_ALIGN),
                      BIG_CHUNK)
            return carry

        def small(i, carry, lo=lo, go=go):
            start_one(pl.multiple_of(lo + i * SEG_ALIGN, SEG_ALIGN), pl.multiple_of(go + i * SEG_ALIGN, SEG_ALIGN),
                      SEG_ALIGN)
            return carry

        lax.fori_loop(0, n_big, big, 0)
        lax.fori_loop(n_big * per_big, n_small, small, 0)


def _local_positions(sel, loff_ref, blk, nblk):
    ne, tb = sel.shape
    sub = lax.broadcasted_iota(jnp.int32, (ne, tb), 0)
    row = lax.broadcasted_iota(jnp.int32, (tb, tb), 0)
    col = lax.broadcasted_iota(jnp.int32, (tb, tb), 1)
    before = (row < col).astype(BF16)
    rank = _dot(sel.astype(BF16), before)
    sub1 = lax.broadcasted_iota(jnp.int32, (ne, 1), 0)
    loff = jnp.zeros((ne, 1), F32)
    for e in range(ne):
        loff = jnp.where(sub1 == e, loff_ref[e * nblk + blk].astype(F32), loff)
    chosen = sel > 0.0
    e_lo = jnp.min(jnp.where(chosen, sub, ne), axis=0, keepdims=True)
    e_hi = jnp.max(jnp.where(chosen, sub, -1), axis=0, keepdims=True)
    return loff + rank, sub == e_lo, sub == e_hi


def _pick(a, mask):
    return jnp.sum(jnp.where(mask, a, 0.0), axis=0, keepdims=True)


def _dispatch_kernel(pad_ref, loff_ref, goff_ref, tail_ref, hn_ref, sel_ref, xs_ref, buf_ref, zero_ref,
                     sem, tail_sem, *, nblk, ne):
    b = pl.program_id(0)
    slot = b % 2
    rows = buf_ref.shape[1]

    def wait_block(blk, s):
        tot = pl.multiple_of(_block_rows(pad_ref, blk, nblk, ne), SEG_ALIGN)
        pltpu.make_async_copy(buf_ref.at[s, pl.ds(0, tot)], xs_ref.at[pl.ds(0, tot)], sem.at[s]).wait()

    @pl.when(b >= 2)
    def _():
        wait_block(b - 2, slot)

    lpos, lo_mask, hi_mask = _local_positions(sel_ref[...], loff_ref, b, nblk)
    p1 = _pick(lpos, lo_mask).astype(jnp.int32)
    p2 = _pick(lpos, hi_mask).astype(jnp.int32)
    r = lax.broadcasted_iota(jnp.int32, (rows, TOK_BLOCK), 0)
    perm = jnp.where((r == p1) | (r == p2), 1.0, 0.0).astype(BF16)
    buf_ref[slot] = _dot(perm, hn_ref[...])

    def start_one(lrow, grow, n):
        pltpu.make_async_copy(buf_ref.at[slot, pl.ds(lrow, n)], xs_ref.at[pl.ds(grow, n)], sem.at[slot]).start()

    _segment_copies(pad_ref, loff_ref, goff_ref, b, nblk, ne, start_one)

    @pl.when(b == nblk - 1)
    def _():
        zero_ref[...] = jnp.zeros_like(zero_ref)
        n_span = ne + 1
        for sp in range(n_span):
            def start_zero(i, carry, sp=sp):
                grow = pl.multiple_of(tail_ref[sp] + i * SEG_ALIGN, SEG_ALIGN)
                pltpu.make_async_copy(zero_ref, xs_ref.at[pl.ds(grow, SEG_ALIGN)], tail_sem).start()
                return carry

            lax.fori_loop(0, tail_ref[n_span + sp], start_zero, 0)

        for sp in range(n_span):
            def wait_zero(i, carry):
                pltpu.make_async_copy(zero_ref, xs_ref.at[pl.ds(0, SEG_ALIGN)], tail_sem).wait()
                return carry

            lax.fori_loop(0, tail_ref[n_span + sp], wait_zero, 0)

        wait_block(b, slot)
        if nblk > 1:
            wait_block(b - 1, 1 - slot)


def moe_dispatch(plan, hn, sel_t, *, n_tiles):
    pad, loff, goff, tail = plan[:4]
    n, d = hn.shape
    ne = sel_t.shape[0]
    nblk = n // TOK_BLOCK
    rows = _loc_rows(ne)
    grid_spec = pltpu.PrefetchScalarGridSpec(
        num_scalar_prefetch=4,
        grid=(nblk,),
        in_specs=[pl.BlockSpec((TOK_BLOCK, d), lambda b, *_: (b, 0)),
                  pl.BlockSpec((ne, TOK_BLOCK), lambda b, *_: (0, b))],
        out_specs=pl.BlockSpec(memory_space=pltpu.HBM),
        scratch_shapes=[pltpu.VMEM((2, rows, d), F32), pltpu.VMEM((SEG_ALIGN, d), F32),
                        pltpu.SemaphoreType.DMA((2,)), pltpu.SemaphoreType.DMA(())],
    )
    return pl.pallas_call(
        partial(_dispatch_kernel, nblk=nblk, ne=ne),
        out_shape=jax.ShapeDtypeStruct((n_tiles * MOE_TM, d), F32),
        grid_spec=grid_spec,
        compiler_params=_params("arbitrary"),
        name="moe_dispatch",
    )(pad, loff, goff, tail, hn, sel_t)


def _expert_ffn_kernel(texp_ref, teff_ref, tvalid_ref, x_ref, wg_ref, wu_ref, wd_ref, out_ref, *, f_chunk):
    i = pl.program_id(0)

    @pl.when(tvalid_ref[i] == 1)
    def _():
        xb = x_ref[...].astype(BF16)
        y = None
        for lo in range(0, wd_ref.shape[1], f_chunk):
            act = _silu_mul(_dot(xb, wg_ref[0, :, lo:lo + f_chunk]), _dot(xb, wu_ref[0, :, lo:lo + f_chunk]))
            part = _dot(act.astype(BF16), wd_ref[0, lo:lo + f_chunk, :])
            y = part if y is None else y + part
        out_ref[...] = y

    @pl.when(tvalid_ref[i] == 0)
    def _():
        out_ref[...] = jnp.zeros_like(out_ref)


def expert_ffn(plan, xs, w_gu, w_down, *, f_chunk=1792):
    tile_expert, tile_eff, tile_valid = plan[4:]
    n_rows, d = xs.shape
    d_exp = w_down.shape[1]
    assert d_exp % f_chunk == 0 and f_chunk % LANES == 0
    n_tiles = n_rows // MOE_TM
    grid_spec = pltpu.PrefetchScalarGridSpec(
        num_scalar_prefetch=3,
        grid=(n_tiles,),
        in_specs=[pl.BlockSpec((MOE_TM, d), lambda i, te, tf, tv: (tf[i], 0)),
                  pl.BlockSpec((1, d, d_exp), lambda i, te, tf, tv: (te[i], 0, 0)),
                  pl.BlockSpec((1, d, d_exp), lambda i, te, tf, tv: (te[i], 0, 1)),
                  pl.BlockSpec((1, d_exp, d), lambda i, te, tf, tv: (te[i], 0, 0), pipeline_mode=pl.Buffered(1))],
        out_specs=pl.BlockSpec((MOE_TM, d), lambda i, te, tf, tv: (i, 0)),
    )
    return pl.pallas_call(
        partial(_expert_ffn_kernel, f_chunk=f_chunk),
        out_shape=jax.ShapeDtypeStruct((n_rows, d), F32),
        grid_spec=grid_spec,
        compiler_params=_params("arbitrary"),
        name="expert_ffn",
    )(tile_expert, tile_eff, tile_valid, xs, w_gu, w_gu, w_down)


def _combine_kernel(pad_ref, loff_ref, goff_ref, ys_ref, x_ref, sel_ref, comb_ref, g_ref, out_ref, buf_ref, sem,
                    *, nblk, ne):
    b = pl.program_id(0)
    slot = b % 2
    rows = buf_ref.shape[1]

    def fetch_block(blk, s):
        def start_one(lrow, grow, n):
            pltpu.make_async_copy(ys_ref.at[pl.ds(grow, n)], buf_ref.at[s, pl.ds(lrow, n)], sem.at[s]).start()

        _segment_copies(pad_ref, loff_ref, goff_ref, blk, nblk, ne, start_one)

    @pl.when(b == 0)
    def _():
        buf_ref[...] = jnp.zeros_like(buf_ref)
        fetch_block(0, 0)

    @pl.when(b + 1 < nblk)
    def _():
        fetch_block(b + 1, 1 - slot)

    tot = pl.multiple_of(_block_rows(pad_ref, b, nblk, ne), SEG_ALIGN)
    pltpu.make_async_copy(ys_ref.at[pl.ds(0, tot)], buf_ref.at[slot, pl.ds(0, tot)], sem.at[slot]).wait()

    lpos, lo_mask, hi_mask = _local_positions(sel_ref[...], loff_ref, b, nblk)
    comb = comb_ref[...]
    p1, p2 = _pick(lpos, lo_mask), _pick(lpos, hi_mask)
    w1, w2 = _pick(comb, lo_mask), _pick(comb, hi_mask)
    r = lax.broadcasted_iota(jnp.int32, (rows, TOK_BLOCK), 0)
    w_rows = jnp.sum(jnp.where(r == p1.astype(jnp.int32), w1, 0.0) + jnp.where(r == p2.astype(jnp.int32), w2, 0.0),
                     axis=1, keepdims=True)
    local = (buf_ref[slot] * w_rows).astype(BF16)
    hi1, hi2 = jnp.floor(p1 * (1.0 / 32.0)), jnp.floor(p2 * (1.0 / 32.0))
    info = jnp.concatenate([hi1, p1 - 32.0 * hi1, hi2, p2 - 32.0 * hi2, jnp.zeros((4, TOK_BLOCK), F32)], axis=0)
    eye = jnp.where(lax.broadcasted_iota(jnp.int32, (TOK_BLOCK, TOK_BLOCK), 0)
                    == lax.broadcasted_iota(jnp.int32, (TOK_BLOCK, TOK_BLOCK), 1), 1.0, 0.0).astype(BF16)
    cols = _dot_nt(eye, info.astype(BF16))
    c1 = (32.0 * cols[:, 0:1] + cols[:, 1:2]).astype(jnp.int32)
    c2 = (32.0 * cols[:, 2:3] + cols[:, 3:4]).astype(jnp.int32)
    c = lax.broadcasted_iota(jnp.int32, (TOK_BLOCK, rows), 1)
    unsort = jnp.where((c == c1) | (c == c2), 1.0, 0.0).astype(BF16)
    y = x_ref[...] + _dot(unsort, local)
    out_ref[...] = y * _rms_scale(y) * g_ref[...]


def moe_combine(plan, ys, x, sel_t, comb_t, g_final):
    pad, loff, goff = plan[:3]
    n, d = x.shape
    ne = sel_t.shape[0]
    nblk = n // TOK_BLOCK
    rows = _loc_rows(ne)
    lane_major = pl.BlockSpec((ne, TOK_BLOCK), lambda b, *_: (0, b))
    grid_spec = pltpu.PrefetchScalarGridSpec(
        num_scalar_prefetch=3,
        grid=(nblk,),
        in_specs=[pl.BlockSpec(memory_space=pltpu.HBM),
                  pl.BlockSpec((TOK_BLOCK, d), lambda b, *_: (b, 0)), lane_major, lane_major,
                  pl.BlockSpec((1, d), lambda b, *_: (0, 0))],
        out_specs=pl.BlockSpec((TOK_BLOCK, d), lambda b, *_: (b, 0)),
        scratch_shapes=[pltpu.VMEM((2, rows, d), F32), pltpu.SemaphoreType.DMA((2,))],
    )
    return pl.pallas_call(
        partial(_combine_kernel, nblk=nblk, ne=ne),
        out_shape=jax.ShapeDtypeStruct((n, d), F32),
        grid_spec=grid_spec,
        compiler_params=_params("arbitrary"),
        name="moe_combine",
    )(pad, loff, goff, ys, x, sel_t, comb_t, g_final)


def moe_ffn(hn, sel_t, comb_t, w_gu, w_down, x, g_final):
    n = hn.shape[0]
    ne = sel_t.shape[0]
    nblk = n // TOK_BLOCK
    n_tiles = -(-(2 * n + nblk * ne * (SEG_ALIGN - 1)) // MOE_TM) + ne
    plan = _dispatch_plan(sel_t, n_tiles)
    xs = moe_dispatch(plan, hn, sel_t, n_tiles=n_tiles)
    ys = expert_ffn(plan, xs, w_gu, w_down)
    return moe_combine(plan, ys, x, sel_t, comb_t, g_final)


def _rope_tables(seq):
    half = ROT_DIM // 2
    pos = jnp.arange(seq, dtype=F32)
    inv_freq = ROPE_THETA ** (-jnp.arange(0, ROT_DIM, 2, dtype=F32) / ROT_DIM)
    ang = pos[:, None] * inv_freq[None, :]
    cos, sin = jnp.cos(ang), jnp.sin(ang)
    pad = jnp.zeros((seq, LANES - ROT_DIM), F32)
    zeros = jnp.zeros((seq, half), F32)
    cos_t = jnp.concatenate([cos, cos, pad + 1.0], axis=1)
    sin_lo = jnp.concatenate([-sin, zeros, pad], axis=1)
    sin_hi = jnp.concatenate([zeros, sin, pad], axis=1)
    return cos_t, sin_lo, sin_hi


def kernel(x, mlstm_norm, mlstm_w_in, mlstm_b_if, mlstm_out_norm, mlstm_w_out, ffn_norm, dense_w_gu,
           dense_w_down, moe_router, moe_w_gu, moe_w_down, kv_norm, kv_w, moba_norm, moba_w_q, moba_w_o,
           final_norm):
    batch, seq, d = x.shape
    n = batch * seq
    H = MLSTM_HEADS
    n_main = H * (2 * MLSTM_DQK + 2 * MLSTM_DV)
    xf = x.reshape(n, d)
    row = lambda g: g.reshape(1, -1).astype(F32)

    w_in = mlstm_w_in[0]
    proj, gates = mlstm_proj(xf, row(mlstm_norm[0]), w_in[:, :n_main].astype(BF16),
                             w_in[:, n_main:].T, mlstm_b_if[0].reshape(2 * H, 1))
    hg = mlstm_core(proj, gates, row(mlstm_out_norm[0]), batch=batch, seq=seq)
    x2, moe_gu_bf16, moe_down_bf16 = mix_ffn(xf, hg, mlstm_w_out[0].astype(BF16), row(ffn_norm[0]),
                                             dense_w_gu[0].astype(BF16), dense_w_down[0].astype(BF16),
                                             riders=(moe_w_gu[0], moe_w_down[0]))

    cos_t, sin_lo, sin_hi = _rope_tables(seq)
    q, k_aug, v, k_means = kvq_proj(x2, row(kv_norm), row(moba_norm[0]), kv_w.astype(BF16),
                                    moba_w_q[0].astype(BF16), cos_t, sin_lo, sin_hi, seq=seq)
    k_means = k_means.reshape(n // MOBA_BLOCK, ATT_KV_HEADS * HEAD_DIM)
    att = moba_attn(q, k_aug, v, k_means, batch=batch, seq=seq)
    x3, hn, sel_t, comb_t = oproj_route(x2, att, moba_w_o[0].astype(BF16), row(ffn_norm[1]), moe_router[0].T)
    out = moe_ffn(hn, sel_t, comb_t, moe_gu_bf16, moe_down_bf16, x3, row(final_norm))
    return out.reshape(batch, seq, d)
```

```python
import math
from functools import partial

import jax
import jax.numpy as jnp
from jax import lax
from jax.experimental import pallas as pl
from jax.experimental.pallas import tpu as pltpu

NORM_EPS = 1e-6

MLSTM_HEADS = 4
MLSTM_DQK = 128
MLSTM_DV = 256
MLSTM_CHUNK = 256
MLSTM_ROWS = 512

ATT_HEADS = 8
ATT_KV_HEADS = 2
HEAD_DIM = 128
MOBA_BLOCK = 256
MOBA_TOPK = 3
ROW_GROUPS = 4
ROPE_THETA = 500000.0
ROT_DIM = HEAD_DIM // 4

N_EXPERTS = 8

LANES = 128
NEG_BIG = -1e30
LOG2E = 1.4426950408889634
VMEM_LIMIT = 56 * 1024 * 1024

F32 = jnp.float32
BF16 = jnp.bfloat16

_NT = (((1,), (1,)), ((), ()))


def _dot(a, b):
    return jnp.dot(a, b, preferred_element_type=F32)


def _dot_nt(a, b, precision=None):
    return lax.dot_general(a, b, _NT, precision=precision, preferred_element_type=F32)


def _split_bf16(a):
    hi = a.astype(BF16)
    return hi, (a - hi.astype(F32)).astype(BF16)


def _dot_nt_split(w, x_hi, x_lo):
    r = w.shape[0]
    w_hi, w_lo = _split_bf16(w)
    a = _dot_nt(jnp.concatenate([w_hi, w_lo], axis=0), x_hi)
    return a[:r] + a[r:] + _dot_nt(w_hi, x_lo)


def _rms_scale(x):
    return lax.rsqrt(jnp.mean(x * x, axis=-1, keepdims=True) + NORM_EPS)


def _sigmoid(x):
    return 1.0 / (1.0 + jnp.exp(-x))


def _params(*sem):
    return pltpu.CompilerParams(dimension_semantics=sem, vmem_limit_bytes=VMEM_LIMIT)


def _const_spec(shape):
    nd = len(shape)
    return pl.BlockSpec(shape, lambda *_: (0,) * nd, pipeline_mode=pl.Buffered(1))


def _rider_slabs(riders, steps):
    slabs = []
    for w in riders:
        rows = w.size // (steps * w.shape[-1])
        assert rows * steps * w.shape[-1] == w.size and rows % 16 == 0 and w.shape[-2] % rows == 0
        slabs.append(w.reshape(steps, rows, w.shape[-1]))
    specs = [pl.BlockSpec((1,) + s.shape[1:], lambda i: (i, 0, 0)) for s in slabs]
    shapes = tuple(jax.ShapeDtypeStruct(s.shape, BF16) for s in slabs)
    return slabs, specs, shapes


def _cast_riders(rider_in, rider_out):
    for src, dst in zip(rider_in, rider_out):
        dst[...] = src[...].astype(BF16)


def _mlstm_proj_kernel(x_ref, g_ref, w_ref, wg_ref, b_ref, *rest, col_chunk, n_riders):
    rider_in, (proj_ref, gates_ref), rider_out = rest[:n_riders], rest[n_riders:n_riders + 2], rest[n_riders + 2:]
    x = x_ref[...]
    xn = x * _rms_scale(x) * g_ref[...]
    xb, x_lo = _split_bf16(xn)
    for c in range(w_ref.shape[1] // col_chunk):
        sl = slice(c * col_chunk, (c + 1) * col_chunk)
        proj_ref[:, sl] = _dot(xb, w_ref[:, sl]).astype(BF16)
    gates_ref[...] = _dot_nt_split(wg_ref[...], xb, x_lo) + b_ref[...]
    _cast_riders(rider_in, rider_out)


def mlstm_proj(x, gain, w_main, w_gate_t, bias, riders=(), *, tm=1024, col_chunk=512):
    n, d = x.shape
    p = w_main.shape[1]
    steps = n // tm
    slabs, slab_specs, slab_shapes = _rider_slabs(riders, steps)
    outs = pl.pallas_call(
        partial(_mlstm_proj_kernel, col_chunk=col_chunk, n_riders=len(slabs)),
        out_shape=(jax.ShapeDtypeStruct((n, p), BF16), jax.ShapeDtypeStruct((8, n), F32)) + slab_shapes,
        grid=(steps,),
        in_specs=[pl.BlockSpec((tm, d), lambda i: (i, 0)),
                  _const_spec((1, d)), _const_spec((d, p)), _const_spec((8, d)), _const_spec((8, 1))] + slab_specs,
        out_specs=(pl.BlockSpec((tm, p), lambda i: (i, 0)), pl.BlockSpec((8, tm), lambda i: (0, i)))
        + tuple(slab_specs),
        compiler_params=_params("parallel"),
        name="mlstm_proj",
    )(x, gain, w_main, w_gate_t, bias, *slabs)
    return outs[:2] + tuple(o.reshape(w.shape) for o, w in zip(outs[2:], riders))


def _mlstm_core_kernel(q_ref, k_ref, v_ref, o_ref, gates_ref, gain_ref, out_ref, state_ref, m_ref):
    L = MLSTM_CHUNK
    H, DK, DV = MLSTM_HEADS, MLSTM_DQK, MLSTM_DV
    scale = DK ** -0.5

    @pl.when(pl.program_id(1) == 0)
    def _():
        state_ref[...] = jnp.zeros_like(state_ref)
        m_ref[...] = jnp.zeros_like(m_ref)

    row = lax.broadcasted_iota(jnp.int32, (L, L), 0)
    col = lax.broadcasted_iota(jnp.int32, (L, L), 1)
    causal = col <= row
    upper_incl = (row <= col).astype(F32)
    ones_blk = jnp.ones((L, LANES), BF16)

    hs = range(H)
    cs = range(q_ref.shape[0] // L)
    units = [(h, c) for c in cs for h in hs]
    rs = lambda c: slice(c * L, (c + 1) * L)
    qs = {(h, c): q_ref[rs(c), h * DK:(h + 1) * DK] for h, c in units}
    ks = {(h, c): k_ref[rs(c), h * DK:(h + 1) * DK] for h, c in units}
    v_augs = {(h, c): jnp.concatenate([v_ref[rs(c), h * DV:(h + 1) * DV], ones_blk], axis=1)
              for h, c in units}
    scores = {u: _dot_nt(qs[u], ks[u]) for u in units}

    log2_scale = math.log2(scale)
    gate = {}
    m_prev = {h: m_ref[h, 0:1, 0:1] for h in hs}
    for h, c in units:
        ig = gates_ref[h:h + 1, rs(c)] * LOG2E
        fg = gates_ref[H + h:H + h + 1, rs(c)]
        logf = (jnp.minimum(fg, 0.0) - jnp.log1p(jnp.exp(-jnp.abs(fg)))) * LOG2E
        bcum = jnp.dot(jnp.broadcast_to(logf, (8, L)), upper_incl,
                       precision=lax.Precision.HIGHEST, preferred_element_type=F32)[0:1, :]
        bcum_s = jnp.broadcast_to(bcum, (L, L))
        bcum_t = bcum_s.T
        dmat = jnp.where(causal, bcum_t - bcum_s + ig, -jnp.inf)
        inter = bcum_t[:, 0:1] + m_prev[h]
        m_t = jnp.maximum(inter, jnp.max(dmat, axis=-1, keepdims=True))
        m_s = m_t - log2_scale
        g_tot = bcum[:, L - 1:L]
        a_end = g_tot - bcum + ig
        m_new = jnp.maximum(g_tot + m_prev[h], jnp.max(a_end, axis=-1, keepdims=True))
        gate[h, c] = dict(m_t=m_t, inter_w=jnp.exp2(inter - m_s), e=jnp.exp2(dmat - m_s),
                          w_row=jnp.exp2(a_end - m_new), decay=jnp.exp2(g_tot + m_prev[h] - m_new))
        m_prev[h] = m_new

    ps = {u: (scores[u] * gate[u]["e"]).astype(BF16) for u in units}
    ktws = {u: (ks[u].astype(F32).T * gate[u]["w_row"]).astype(BF16) for u in units}
    states = {h: state_ref[h] for h in hs}
    nums = {}
    for c in cs:
        for h in hs:
            nums[h, c] = (_dot(ps[h, c], v_augs[h, c])
                          + gate[h, c]["inter_w"] * _dot(qs[h, c], states[h].astype(BF16)))
        for h in hs:
            states[h] = gate[h, c]["decay"] * states[h] + _dot(ktws[h, c], v_augs[h, c])
    for h in hs:
        state_ref[h] = states[h]
        m_ref[h] = jnp.broadcast_to(m_prev[h], m_ref.shape[1:])
    for h, c in units:
        nq = nums[h, c][:, DV:DV + 1]
        den = jnp.maximum(jnp.abs(nq), jnp.exp2(-gate[h, c]["m_t"]))
        hval = nums[h, c][:, :DV] / den
        hn = hval * _rms_scale(hval) * gain_ref[:, h * DV:(h + 1) * DV]
        og = o_ref[rs(c), h * DV:(h + 1) * DV].astype(F32)
        out_ref[rs(c), h * DV:(h + 1) * DV] = (hn * _sigmoid(og)).astype(BF16)


def mlstm_core(proj, gates, out_gain, *, batch, seq):
    n = proj.shape[0]
    L = MLSTM_ROWS
    assert L % MLSTM_CHUNK == 0
    nc = seq // L
    H, DK, DV = MLSTM_HEADS, MLSTM_DQK, MLSTM_DV
    qw, vw = H * DK, H * DV
    assert 2 * qw == vw
    rows = lambda b, c: b * nc + c
    return pl.pallas_call(
        _mlstm_core_kernel,
        out_shape=jax.ShapeDtypeStruct((n, vw), BF16),
        grid=(batch, nc),
        in_specs=[pl.BlockSpec((L, qw), lambda b, c: (rows(b, c), 0)),
                  pl.BlockSpec((L, qw), lambda b, c: (rows(b, c), 1)),
                  pl.BlockSpec((L, vw), lambda b, c: (rows(b, c), 1)),
                  pl.BlockSpec((L, vw), lambda b, c: (rows(b, c), 2)),
                  pl.BlockSpec((8, L), lambda b, c: (0, rows(b, c))),
                  _const_spec((1, vw))],
        out_specs=pl.BlockSpec((L, vw), lambda b, c: (rows(b, c), 0)),
        scratch_shapes=[pltpu.VMEM((H, DK, DV + LANES), F32), pltpu.VMEM((H, 8, LANES), F32)],
        compiler_params=_params("parallel", "arbitrary"),
        name="mlstm_core",
    )(proj, proj, proj, proj, gates, out_gain)


def _silu_mul(g, u):
    return g * _sigmoid(g) * u


def _mix_ffn_kernel(x_ref, h_ref, wo_ref, g_ref, wgu_ref, wd_ref, *rest, f_chunk, n_riders):
    rider_in, out_ref, rider_out = rest[:n_riders], rest[n_riders], rest[n_riders + 1:]
    d_ff = wd_ref.shape[0]
    x1 = x_ref[...] + _dot(h_ref[...], wo_ref[...])
    hn = (x1 * _rms_scale(x1) * g_ref[...]).astype(BF16)
    acc = jnp.zeros_like(x1)
    for c in range(d_ff // f_chunk):
        lo = c * f_chunk
        g = _dot(hn, wgu_ref[:, lo:lo + f_chunk])
        u = _dot(hn, wgu_ref[:, d_ff + lo:d_ff + lo + f_chunk])
        acc = acc + _dot(_silu_mul(g, u).astype(BF16), wd_ref[lo:lo + f_chunk, :])
    out_ref[...] = x1 + acc
    _cast_riders(rider_in, rider_out)


def mix_ffn(x, h, w_out, gain, w_gu, w_down, riders=(), *, tm=512, f_chunk=256):
    n, d = x.shape
    d_ff = w_down.shape[0]
    steps = n // tm
    assert d_ff % f_chunk == 0
    slabs, slab_specs, slab_shapes = _rider_slabs(riders, steps)
    outs = pl.pallas_call(
        partial(_mix_ffn_kernel, f_chunk=f_chunk, n_riders=len(slabs)),
        out_shape=(jax.ShapeDtypeStruct((n, d), F32),) + slab_shapes,
        grid=(steps,),
        in_specs=[pl.BlockSpec((tm, d), lambda i: (i, 0)), pl.BlockSpec((tm, h.shape[1]), lambda i: (i, 0)),
                  _const_spec(w_out.shape), _const_spec((1, d)), _const_spec(w_gu.shape),
                  _const_spec(w_down.shape)] + slab_specs,
        out_specs=(pl.BlockSpec((tm, d), lambda i: (i, 0)),) + tuple(slab_specs),
        compiler_params=_params("parallel"),
        name="mix_ffn",
    )(x, h, w_out, gain, w_gu, w_down, *slabs)
    return (outs[0],) + tuple(o.reshape(w.shape) for o, w in zip(outs[1:], riders))


def _rope(t, cos, sin_lo, sin_hi):
    half = ROT_DIM // 2
    return t * cos + pltpu.roll(t, LANES - half, 1) * sin_lo + pltpu.roll(t, half, 1) * sin_hi


def _kvq_kernel(x_ref, gkv_ref, gq_ref, wkv_ref, wq_ref, cos_ref, slo_ref, shi_ref,
                q_ref, k_ref, v_ref, km_ref, *, blocks_per_seq):
    tm = x_ref.shape[0]
    nsub = tm // MOBA_BLOCK
    G, DH = ATT_KV_HEADS, HEAD_DIM
    x = x_ref[...]
    xs = x * _rms_scale(x)
    cos, slo, shi = cos_ref[...], slo_ref[...], shi_ref[...]

    q = _dot((xs * gq_ref[...]).astype(BF16), wq_ref[...])
    kv = _dot((xs * gkv_ref[...]).astype(BF16), wkv_ref[...])
    for h in range(ATT_HEADS):
        q_ref[:, h * DH:(h + 1) * DH] = _rope(q[:, h * DH:(h + 1) * DH], cos, slo, shi).astype(BF16)
    for g in range(G):
        v_ref[:, 2 * g * DH:(2 * g + 1) * DH] = kv[:, (G + g) * DH:(G + g + 1) * DH].astype(BF16)
        v_ref[:, (2 * g + 1) * DH:(2 * g + 2) * DH] = jnp.ones((tm, DH), BF16)
    first_blk = (pl.program_id(0) * nsub) % blocks_per_seq
    blk = first_blk + lax.broadcasted_iota(jnp.int32, (tm, LANES), 0) // MOBA_BLOCK
    ind = (lax.broadcasted_iota(jnp.int32, (tm, LANES), 1) == blk).astype(BF16)
    for g in range(G):
        kg = _rope(kv[:, g * DH:(g + 1) * DH], cos, slo, shi)
        k_ref[:, 2 * g * DH:(2 * g + 1) * DH] = kg.astype(BF16)
        k_ref[:, (2 * g + 1) * DH:(2 * g + 2) * DH] = ind
        for s in range(nsub):
            km_ref[0, s:s + 1, g * DH:(g + 1) * DH] = jnp.mean(
                kg[s * MOBA_BLOCK:(s + 1) * MOBA_BLOCK, :], axis=0, keepdims=True)


def kvq_proj(x, g_kv, g_q, w_kv, w_q, cos, sin_lo, sin_hi, *, seq, tm=1024):
    n, d = x.shape
    G, DH = ATT_KV_HEADS, HEAD_DIM
    nsub = tm // MOBA_BLOCK
    tiles_per_seq = seq // tm
    tab = pl.BlockSpec((tm, LANES), lambda i: (i % tiles_per_seq, 0))
    return pl.pallas_call(
        partial(_kvq_kernel, blocks_per_seq=seq // MOBA_BLOCK),
        out_shape=(jax.ShapeDtypeStruct((n, ATT_HEADS * DH), BF16),
                   jax.ShapeDtypeStruct((n, 2 * G * DH), BF16),
                   jax.ShapeDtypeStruct((n, 2 * G * DH), BF16),
                   jax.ShapeDtypeStruct((n // tm, nsub, G * DH), F32)),
        grid=(n // tm,),
        in_specs=[pl.BlockSpec((tm, d), lambda i: (i, 0)), _const_spec((1, d)), _const_spec((1, d)),
                  _const_spec(w_kv.shape), _const_spec(w_q.shape), tab, tab, tab],
        out_specs=(pl.BlockSpec((tm, ATT_HEADS * DH), lambda i: (i, 0)),
                   pl.BlockSpec((tm, 2 * G * DH), lambda i: (i, 0)),
                   pl.BlockSpec((tm, 2 * G * DH), lambda i: (i, 0)),
                   pl.BlockSpec((1, nsub, G * DH), lambda i: (i, 0, 0))),
        compiler_params=_params("parallel"),
        name="kvq_proj",
    )(x, g_kv, g_q, w_kv, w_q, cos, sin_lo, sin_hi)


def _moba_kernel(q_ref, k_ref, v_ref, km_ref, out_ref, m_ref, acc_ref):
    QB, DH = MOBA_BLOCK, HEAD_DIM
    G = ATT_KV_HEADS
    hpg = ATT_HEADS // G
    R = hpg * QB
    scale = DH ** -0.5
    j = pl.program_id(1)
    nb = km_ref.shape[0]

    exp_scale = scale * LOG2E
    wide = lambda a: jnp.concatenate([a] * (QB // LANES), axis=1)
    row_groups = [slice(i * R // ROW_GROUPS, (i + 1) * R // ROW_GROUPS) for i in range(ROW_GROUPS)]
    units = [(g, rs, slice(g * R + rs.start, g * R + rs.stop)) for g in range(G) for rs in row_groups]
    kcols = lambda g: slice(2 * g * DH, (2 * g + 2) * DH)
    own = pl.ds(pl.multiple_of(j * QB, QB), QB)

    qs = [jnp.concatenate([q_ref[:, (g * hpg + h) * DH:(g * hpg + h + 1) * DH] for h in range(hpg)], axis=0)
          for g in range(G)]

    own_scores = [_dot_nt(qs[g][rs], k_ref[own, 2 * g * DH:(2 * g + 1) * DH]) for g, rs, _ in units]

    blk = lax.broadcasted_iota(jnp.int32, (nb, R), 0)
    eye = jnp.where(lax.broadcasted_iota(jnp.int32, (nb, LANES), 0)
                    == lax.broadcasted_iota(jnp.int32, (nb, LANES), 1), 1.0, 0.0).astype(BF16)
    q_augs = []
    for g in range(G):
        km_hi, km_lo = _split_bf16(km_ref[:, g * DH:(g + 1) * DH])
        g2 = _dot_nt(jnp.concatenate([km_hi, km_lo], axis=0), qs[g])
        gate = g2[:nb] + g2[nb:]
        beats = jnp.zeros((nb, R), jnp.int32)
        for mb in range(nb):
            gm = gate[mb:mb + 1, :]
            ahead = (gm > gate) | ((gm == gate) & (mb < blk))
            beats = beats + jnp.where(ahead, 1, 0) * (mb < j).astype(jnp.int32)
        selected = jnp.where((blk < j) & (beats < MOBA_TOPK), 1.0, 0.0).astype(BF16)
        sel_rows = lax.dot_general(selected, eye, (((0,), (0,)), ((), ())), preferred_element_type=F32)
        pen = ((1.0 - sel_rows) * NEG_BIG).astype(BF16)
        q_augs.append(jnp.concatenate([qs[g], pen], axis=1))

    m0s, p0s = [], []
    for (g, rs, _), s in zip(units, own_scores):
        rows = rs.stop - rs.start
        t_pos = (lax.broadcasted_iota(jnp.int32, (rows, QB), 0) + rs.start) % QB
        s_pos = lax.broadcasted_iota(jnp.int32, (rows, QB), 1)
        s = jnp.where(s_pos <= t_pos, s, -jnp.inf)
        m0 = jnp.max(s, axis=-1, keepdims=True)
        m0s.append(m0)
        p0s.append(jnp.exp2((s - m0) * exp_scale).astype(BF16))
    for (g, rs, ar), m0, p0 in zip(units, m0s, p0s):
        m_ref[ar, :] = jnp.broadcast_to(m0, (rs.stop - rs.start, LANES))
        acc_ref[ar, :] = _dot(p0, v_ref[own, kcols(g)])

    def attend(first_blk, n_blk):
        keys = pl.ds(pl.multiple_of(first_blk * QB, QB), n_blk * QB)
        ss = [_dot_nt(q_augs[g][rs], k_ref[keys, kcols(g)]) for g, rs, _ in units]
        m_olds = [m_ref[ar, :] for _, _, ar in units]
        m_news = [jnp.maximum(mo, jnp.max(s, axis=-1, keepdims=True)) for mo, s in zip(m_olds, ss)]
        ps = [jnp.exp2((s - jnp.concatenate([mn] * (n_blk * QB // LANES), axis=1)) * exp_scale).astype(BF16)
              for s, mn in zip(ss, m_news)]
        pvs = [_dot(p, v_ref[keys, kcols(g)]) for p, (g, _, _) in zip(ps, units)]
        for (_, _, ar), mo, mn, pv in zip(units, m_olds, m_news, pvs):
            alpha = jnp.exp2((mo - mn) * exp_scale)
            acc_ref[ar, :] = wide(alpha) * acc_ref[ar, :] + pv
            m_ref[ar, :] = mn

    def body(i, carry):
        attend(2 * i, 2)
        return carry

    n_pairs = lax.shift_right_logical(j, 1)
    lax.fori_loop(0, n_pairs, body, 0)

    @pl.when(j % 2 == 1)
    def _():
        attend(j - 1, 1)

    for h in range(ATT_HEADS):
        acc = acc_ref[h * QB:(h + 1) * QB, :]
        out_ref[:, h * DH:(h + 1) * DH] = (acc[:, :DH] / acc[:, DH:]).astype(BF16)


def moba_attn(q, k_aug, v, k_means, *, batch, seq):
    n = q.shape[0]
    G, DH, QB = ATT_KV_HEADS, HEAD_DIM, MOBA_BLOCK
    nb = seq // QB
    assert DH == LANES and nb <= LANES
    rows = ATT_HEADS * QB
    return pl.pallas_call(
        _moba_kernel,
        out_shape=jax.ShapeDtypeStruct((n, ATT_HEADS * DH), BF16),
        grid=(batch, nb),
        in_specs=[pl.BlockSpec((QB, ATT_HEADS * DH), lambda b, j: (b * nb + j, 0)),
                  pl.BlockSpec((seq, 2 * G * DH), lambda b, j: (b, 0)),
                  pl.BlockSpec((seq, 2 * G * DH), lambda b, j: (b, 0)),
                  pl.BlockSpec((nb, G * DH), lambda b, j: (b, 0))],
        out_specs=pl.BlockSpec((QB, ATT_HEADS * DH), lambda b, j: (b * nb + j, 0)),
        scratch_shapes=[pltpu.VMEM((rows, LANES), F32), pltpu.VMEM((rows, 2 * DH), F32)],
        compiler_params=_params("parallel", "arbitrary"),
        name="moba_attn",
    )(q, k_aug, v, k_means)


def _oproj_route_kernel(x_ref, a_ref, wo_ref, g_ref, r_ref, x_out_ref, hn_ref, sel_ref, comb_ref):
    x3 = x_ref[...] + _dot(a_ref[...], wo_ref[...])
    x_out_ref[...] = x3
    hn = x3 * _rms_scale(x3) * g_ref[...]
    hn_hi, hn_lo = _split_bf16(hn)
    hn_ref[...] = hn_hi
    logits = _dot_nt_split(r_ref[...], hn_hi, hn_lo)
    ne = logits.shape[0]
    idx = lax.broadcasted_iota(jnp.int32, logits.shape, 0)
    m1 = jnp.max(logits, axis=0, keepdims=True)
    i1 = jnp.min(jnp.where(logits == m1, idx, ne), axis=0, keepdims=True)
    first = idx == i1
    rest = jnp.where(first, -jnp.inf, logits)
    m2 = jnp.max(rest, axis=0, keepdims=True)
    i2 = jnp.min(jnp.where(rest == m2, idx, ne), axis=0, keepdims=True)
    second = idx == i2
    e = jnp.exp(m2 - m1)
    sel_ref[...] = jnp.where(first | second, 1.0, 0.0)
    comb_ref[...] = jnp.where(first, 1.0 / (1.0 + e), 0.0) + jnp.where(second, e / (1.0 + e), 0.0)


def oproj_route(x, att, w_o, gain, router_t, *, tm=1024):
    n, d = x.shape
    ne = router_t.shape[0]
    row = lambda w: pl.BlockSpec((tm, w), lambda i: (i, 0))
    lane_major = pl.BlockSpec((ne, tm), lambda i: (0, i))
    return pl.pallas_call(
        _oproj_route_kernel,
        out_shape=(jax.ShapeDtypeStruct((n, d), F32), jax.ShapeDtypeStruct((n, d), BF16),
                   jax.ShapeDtypeStruct((ne, n), F32), jax.ShapeDtypeStruct((ne, n), F32)),
        grid=(n // tm,),
        in_specs=[row(d), row(att.shape[1]), _const_spec(w_o.shape), _const_spec((1, d)),
                  _const_spec(router_t.shape)],
        out_specs=(row(d), row(d), lane_major, lane_major),
        compiler_params=_params("parallel"),
        name="oproj_route",
    )(x, att, w_o, gain, router_t)


TOK_BLOCK = 512
SEG_ALIGN = 8
BIG_CHUNK = 32
MOE_TM = 512

def _loc_rows(ne):
    need = 2 * TOK_BLOCK + ne * (SEG_ALIGN - 1)
    return -(-need // LANES) * LANES


def _dispatch_plan(sel_t, n_tiles):
    ne, n = sel_t.shape
    nblk = n // TOK_BLOCK
    i32 = jnp.int32
    cnt = sel_t.reshape(ne, nblk, TOK_BLOCK).sum(-1).astype(i32)
    pad = (cnt + SEG_ALIGN - 1) // SEG_ALIGN * SEG_ALIGN
    loff = jnp.cumsum(pad, axis=0) - pad
    tot = pad.sum(axis=1)
    reg = (tot + MOE_TM - 1) // MOE_TM * MOE_TM
    reg_start = jnp.cumsum(reg) - reg
    goff = reg_start[:, None] + jnp.cumsum(pad, axis=1) - pad
    tiles_end = jnp.cumsum(reg // MOE_TM)
    n_valid = tiles_end[-1]
    t = jnp.arange(n_tiles, dtype=i32)
    t_eff = jnp.minimum(t, n_valid - 1)
    tile_expert = (tiles_end[None, :] <= t_eff[:, None]).sum(axis=1).astype(i32)
    tile_valid = (t < n_valid).astype(i32)
    used = reg.sum()
    tail = jnp.concatenate([reg_start + tot, used[None], (reg - tot) // SEG_ALIGN,
                            ((n_tiles * MOE_TM - used) // SEG_ALIGN)[None]]).astype(i32)
    flat = lambda a: a.reshape(-1).astype(i32)
    return flat(pad), flat(loff), flat(goff), tail, tile_expert, t_eff.astype(i32), tile_valid


def _block_rows(pad_ref, blk, nblk, ne):
    tot = pad_ref[blk]
    for e in range(1, ne):
        tot = tot + pad_ref[e * nblk + blk]
    return tot


def _segment_copies(pad_ref, loff_ref, goff_ref, blk, nblk, ne, start_one):
    per_big = BIG_CHUNK // SEG_ALIGN
    for e in range(ne):
        n_small = lax.shift_right_logical(pad_ref[e * nblk + blk], 3)
        n_big = lax.div(n_small, per_big)
        lo, go = loff_ref[e * nblk + blk], goff_ref[e * nblk + blk]

        def big(i, carry, lo=lo, go=go):
            start_one(pl.multiple_of(lo + i * BIG_CHUNK, SEG_ALIGN), pl.multiple_of(go + i * BIG_CHUNK, SEG_ALIGN),
                      BIG_CHUNK)
            return carry

        def small(i, carry, lo=lo, go=go):
            start_one(pl.multiple_of(lo + i * SEG_ALIGN, SEG_ALIGN), pl.multiple_of(go + i * SEG_ALIGN, SEG_ALIGN),
                      SEG_ALIGN)
            return carry

        lax.fori_loop(0, n_big, big, 0)
        lax.fori_loop(n_big * per_big, n_small, small, 0)


def _local_positions(sel, loff_ref, blk, nblk):
    ne, tb = sel.shape
    sub = lax.broadcasted_iota(jnp.int32, (ne, tb), 0)
    row = lax.broadcasted_iota(jnp.int32, (tb, tb), 0)
    col = lax.broadcasted_iota(jnp.int32, (tb, tb), 1)
    before = (row < col).astype(BF16)
    rank = _dot(sel.astype(BF16), before)
    sub1 = lax.broadcasted_iota(jnp.int32, (ne, 1), 0)
    loff = jnp.zeros((ne, 1), F32)
    for e in range(ne):
        loff = jnp.where(sub1 == e, loff_ref[e * nblk + blk].astype(F32), loff)
    chosen = sel > 0.0
    e_lo = jnp.min(jnp.where(chosen, sub, ne), axis=0, keepdims=True)
    e_hi = jnp.max(jnp.where(chosen, sub, -1), axis=0, keepdims=True)
    return loff + rank, sub == e_lo, sub == e_hi


def _pick(a, mask):
    return jnp.sum(jnp.where(mask, a, 0.0), axis=0, keepdims=True)


def _dispatch_kernel(pad_ref, loff_ref, goff_ref, tail_ref, hn_ref, sel_ref, xs_ref, buf_ref, zero_ref,
                     sem, tail_sem, *, nblk, ne):
    b = pl.program_id(0)
    slot = b % 2
    rows = buf_ref.shape[1]

    def wait_block(blk, s):
        tot = pl.multiple_of(_block_rows(pad_ref, blk, nblk, ne), SEG_ALIGN)
        pltpu.make_async_copy(buf_ref.at[s, pl.ds(0, tot)], xs_ref.at[pl.ds(0, tot)], sem.at[s]).wait()

    @pl.when(b >= 2)
    def _():
        wait_block(b - 2, slot)

    lpos, lo_mask, hi_mask = _local_positions(sel_ref[...], loff_ref, b, nblk)
    p1 = _pick(lpos, lo_mask).astype(jnp.int32)
    p2 = _pick(lpos, hi_mask).astype(jnp.int32)
    r = lax.broadcasted_iota(jnp.int32, (rows, TOK_BLOCK), 0)
    perm = jnp.where((r == p1) | (r == p2), 1.0, 0.0).astype(BF16)
    buf_ref[slot] = _dot(perm, hn_ref[...])

    def start_one(lrow, grow, n):
        pltpu.make_async_copy(buf_ref.at[slot, pl.ds(lrow, n)], xs_ref.at[pl.ds(grow, n)], sem.at[slot]).start()

    _segment_copies(pad_ref, loff_ref, goff_ref, b, nblk, ne, start_one)

    @pl.when(b == nblk - 1)
    def _():
        zero_ref[...] = jnp.zeros_like(zero_ref)
        n_span = ne + 1
        for sp in range(n_span):
            def start_zero(i, carry, sp=sp):
                grow = pl.multiple_of(tail_ref[sp] + i * SEG_ALIGN, SEG_ALIGN)
                pltpu.make_async_copy(zero_ref, xs_ref.at[pl.ds(grow, SEG_ALIGN)], tail_sem).start()
                return carry

            lax.fori_loop(0, tail_ref[n_span + sp], start_zero, 0)

        for sp in range(n_span):
            def wait_zero(i, carry):
                pltpu.make_async_copy(zero_ref, xs_ref.at[pl.ds(0, SEG_ALIGN)], tail_sem).wait()
                return carry

            lax.fori_loop(0, tail_ref[n_span + sp], wait_zero, 0)

        wait_block(b, slot)
        if nblk > 1:
            wait_block(b - 1, 1 - slot)


def moe_dispatch(plan, hn, sel_t, *, n_tiles):
    pad, loff, goff, tail = plan[:4]
    n, d = hn.shape
    ne = sel_t.shape[0]
    nblk = n // TOK_BLOCK
    rows = _loc_rows(ne)
    grid_spec = pltpu.PrefetchScalarGridSpec(
        num_scalar_prefetch=4,
        grid=(nblk,),
        in_specs=[pl.BlockSpec((TOK_BLOCK, d), lambda b, *_: (b, 0)),
                  pl.BlockSpec((ne, TOK_BLOCK), lambda b, *_: (0, b))],
        out_specs=pl.BlockSpec(memory_space=pltpu.HBM),
        scratch_shapes=[pltpu.VMEM((2, rows, d), F32), pltpu.VMEM((SEG_ALIGN, d), F32),
                        pltpu.SemaphoreType.DMA((2,)), pltpu.SemaphoreType.DMA(())],
    )
    return pl.pallas_call(
        partial(_dispatch_kernel, nblk=nblk, ne=ne),
        out_shape=jax.ShapeDtypeStruct((n_tiles * MOE_TM, d), F32),
        grid_spec=grid_spec,
        compiler_params=_params("arbitrary"),
        name="moe_dispatch",
    )(pad, loff, goff, tail, hn, sel_t)


def _expert_ffn_kernel(texp_ref, teff_ref, tvalid_ref, x_ref, wg_ref, wu_ref, wd_ref, out_ref, *, f_chunk):
    i = pl.program_id(0)

    @pl.when(tvalid_ref[i] == 1)
    def _():
        xb = x_ref[...].astype(BF16)
        y = None
        for lo in range(0, wd_ref.shape[1], f_chunk):
            act = _silu_mul(_dot(xb, wg_ref[0, :, lo:lo + f_chunk]), _dot(xb, wu_ref[0, :, lo:lo + f_chunk]))
            part = _dot(act.astype(BF16), wd_ref[0, lo:lo + f_chunk, :])
            y = part if y is None else y + part
        out_ref[...] = y

    @pl.when(tvalid_ref[i] == 0)
    def _():
        out_ref[...] = jnp.zeros_like(out_ref)


def expert_ffn(plan, xs, w_gu, w_down, *, f_chunk=1792):
    tile_expert, tile_eff, tile_valid = plan[4:]
    n_rows, d = xs.shape
    d_exp = w_down.shape[1]
    assert d_exp % f_chunk == 0 and f_chunk % LANES == 0
    n_tiles = n_rows // MOE_TM
    grid_spec = pltpu.PrefetchScalarGridSpec(
        num_scalar_prefetch=3,
        grid=(n_tiles,),
        in_specs=[pl.BlockSpec((MOE_TM, d), lambda i, te, tf, tv: (tf[i], 0)),
                  pl.BlockSpec((1, d, d_exp), lambda i, te, tf, tv: (te[i], 0, 0)),
                  pl.BlockSpec((1, d, d_exp), lambda i, te, tf, tv: (te[i], 0, 1)),
                  pl.BlockSpec((1, d_exp, d), lambda i, te, tf, tv: (te[i], 0, 0), pipeline_mode=pl.Buffered(1))],
        out_specs=pl.BlockSpec((MOE_TM, d), lambda i, te, tf, tv: (i, 0)),
    )
    return pl.pallas_call(
        partial(_expert_ffn_kernel, f_chunk=f_chunk),
        out_shape=jax.ShapeDtypeStruct((n_rows, d), F32),
        grid_spec=grid_spec,
        compiler_params=_params("arbitrary"),
        name="expert_ffn",
    )(tile_expert, tile_eff, tile_valid, xs, w_gu, w_gu, w_down)


def _combine_kernel(pad_ref, loff_ref, goff_ref, ys_ref, x_ref, sel_ref, comb_ref, g_ref, out_ref, buf_ref, sem,
                    *, nblk, ne):
    b = pl.program_id(0)
    slot = b % 2
    rows = buf_ref.shape[1]

    def fetch_block(blk, s):
        def start_one(lrow, grow, n):
            pltpu.make_async_copy(ys_ref.at[pl.ds(grow, n)], buf_ref.at[s, pl.ds(lrow, n)], sem.at[s]).start()

        _segment_copies(pad_ref, loff_ref, goff_ref, blk, nblk, ne, start_one)

    @pl.when(b == 0)
    def _():
        buf_ref[...] = jnp.zeros_like(buf_ref)
        fetch_block(0, 0)

    @pl.when(b + 1 < nblk)
    def _():
        fetch_block(b + 1, 1 - slot)

    tot = pl.multiple_of(_block_rows(pad_ref, b, nblk, ne), SEG_ALIGN)
    pltpu.make_async_copy(ys_ref.at[pl.ds(0, tot)], buf_ref.at[slot, pl.ds(0, tot)], sem.at[slot]).wait()

    lpos, lo_mask, hi_mask = _local_positions(sel_ref[...], loff_ref, b, nblk)
    comb = comb_ref[...]
    p1, p2 = _pick(lpos, lo_mask), _pick(lpos, hi_mask)
    w1, w2 = _pick(comb, lo_mask), _pick(comb, hi_mask)
    r = lax.broadcasted_iota(jnp.int32, (rows, TOK_BLOCK), 0)
    w_rows = jnp.sum(jnp.where(r == p1.astype(jnp.int32), w1, 0.0) + jnp.where(r == p2.astype(jnp.int32), w2, 0.0),
                     axis=1, keepdims=True)
    local = (buf_ref[slot] * w_rows).astype(BF16)
    hi1, hi2 = jnp.floor(p1 * (1.0 / 32.0)), jnp.floor(p2 * (1.0 / 32.0))
    info = jnp.concatenate([hi1, p1 - 32.0 * hi1, hi2, p2 - 32.0 * hi2, jnp.zeros((4, TOK_BLOCK), F32)], axis=0)
    eye = jnp.where(lax.broadcasted_iota(jnp.int32, (TOK_BLOCK, TOK_BLOCK), 0)
                    == lax.broadcasted_iota(jnp.int32, (TOK_BLOCK, TOK_BLOCK), 1), 1.0, 0.0).astype(BF16)
    cols = _dot_nt(eye, info.astype(BF16))
    c1 = (32.0 * cols[:, 0:1] + cols[:, 1:2]).astype(jnp.int32)
    c2 = (32.0 * cols[:, 2:3] + cols[:, 3:4]).astype(jnp.int32)
    c = lax.broadcasted_iota(jnp.int32, (TOK_BLOCK, rows), 1)
    unsort = jnp.where((c == c1) | (c == c2), 1.0, 0.0).astype(BF16)
    y = x_ref[...] + _dot(unsort, local)
    out_ref[...] = y * _rms_scale(y) * g_ref[...]


def moe_combine(plan, ys, x, sel_t, comb_t, g_final):
    pad, loff, goff = plan[:3]
    n, d = x.shape
    ne = sel_t.shape[0]
    nblk = n // TOK_BLOCK
    rows = _loc_rows(ne)
    lane_major = pl.BlockSpec((ne, TOK_BLOCK), lambda b, *_: (0, b))
    grid_spec = pltpu.PrefetchScalarGridSpec(
        num_scalar_prefetch=3,
        grid=(nblk,),
        in_specs=[pl.BlockSpec(memory_space=pltpu.HBM),
                  pl.BlockSpec((TOK_BLOCK, d), lambda b, *_: (b, 0)), lane_major, lane_major,
                  pl.BlockSpec((1, d), lambda b, *_: (0, 0))],
        out_specs=pl.BlockSpec((TOK_BLOCK, d), lambda b, *_: (b, 0)),
        scratch_shapes=[pltpu.VMEM((2, rows, d), F32), pltpu.SemaphoreType.DMA((2,))],
    )
    return pl.pallas_call(
        partial(_combine_kernel, nblk=nblk, ne=ne),
        out_shape=jax.ShapeDtypeStruct((n, d), F32),
        grid_spec=grid_spec,
        compiler_params=_params("arbitrary"),
        name="moe_combine",
    )(pad, loff, goff, ys, x, sel_t, comb_t, g_final)


def moe_ffn(hn, sel_t, comb_t, w_gu, w_down, x, g_final):
    n = hn.shape[0]
    ne = sel_t.shape[0]
    nblk = n // TOK_BLOCK
    n_tiles = -(-(2 * n + nblk * ne * (SEG_ALIGN - 1)) // MOE_TM) + ne
    plan = _dispatch_plan(sel_t, n_tiles)
    xs = moe_dispatch(plan, hn, sel_t, n_tiles=n_tiles)
    ys = expert_ffn(plan, xs, w_gu, w_down)
    return moe_combine(plan, ys, x, sel_t, comb_t, g_final)


def _rope_tables(seq):
    half = ROT_DIM // 2
    pos = jnp.arange(seq, dtype=F32)
    inv_freq = ROPE_THETA ** (-jnp.arange(0, ROT_DIM, 2, dtype=F32) / ROT_DIM)
    ang = pos[:, None] * inv_freq[None, :]
    cos, sin = jnp.cos(ang), jnp.sin(ang)
    pad = jnp.zeros((seq, LANES - ROT_DIM), F32)
    zeros = jnp.zeros((seq, half), F32)
    cos_t = jnp.concatenate([cos, cos, pad + 1.0], axis=1)
    sin_lo = jnp.concatenate([-sin, zeros, pad], axis=1)
    sin_hi = jnp.concatenate([zeros, sin, pad], axis=1)
    return cos_t, sin_lo, sin_hi


def kernel(x, mlstm_norm, mlstm_w_in, mlstm_b_if, mlstm_out_norm, mlstm_w_out, ffn_norm, dense_w_gu,
           dense_w_down, moe_router, moe_w_gu, moe_w_down, kv_norm, kv_w, moba_norm, moba_w_q, moba_w_o,
           final_norm):
    batch, seq, d = x.shape
    n = batch * seq
    H = MLSTM_HEADS
    n_main = H * (2 * MLSTM_DQK + 2 * MLSTM_DV)
    xf = x.reshape(n, d)
    row = lambda g: g.reshape(1, -1).astype(F32)

    w_in = mlstm_w_in[0]
    proj, gates, w_out_b, dense_gu_b, kv_w_b, w_q_b, w_o_b = mlstm_proj(
        xf, row(mlstm_norm[0]), w_in[:, :n_main].astype(BF16), w_in[:, n_main:].T,
        mlstm_b_if[0].reshape(2 * H, 1),
        riders=(mlstm_w_out[0], dense_w_gu[0], kv_w, moba_w_q[0], moba_w_o[0]))
    hg = mlstm_core(proj, gates, row(mlstm_out_norm[0]), batch=batch, seq=seq)
    x2, moe_gu_bf16, moe_down_bf16 = mix_ffn(xf, hg, w_out_b, row(ffn_norm[0]), dense_gu_b,
                                             dense_w_down[0].astype(BF16), riders=(moe_w_gu[0], moe_w_down[0]))

    cos_t, sin_lo, sin_hi = _rope_tables(seq)
    q, k_aug, v, k_means = kvq_proj(x2, row(kv_norm), row(moba_norm[0]), kv_w_b, w_q_b, cos_t, sin_lo, sin_hi,
                                    seq=seq)
    k_means = k_means.reshape(n // MOBA_BLOCK, ATT_KV_HEADS * HEAD_DIM)
    att = moba_attn(q, k_aug, v, k_means, batch=batch, seq=seq)
    x3, hn, sel_t, comb_t = oproj_route(x2, att, w_o_b, row(ffn_norm[1]), moe_router[0].T)
    out = moe_ffn(hn, sel_t, comb_t, moe_gu_bf16, moe_down_bf16, x3, row(final_norm))
    return out.reshape(batch, seq, d)
```

```python
import math
from functools import partial

import jax
import jax.numpy as jnp
from jax import lax
from jax.experimental import pallas as pl
from jax.experimental.pallas import tpu as pltpu

NORM_EPS = 1e-6

MLSTM_HEADS = 4
MLSTM_DQK = 128
MLSTM_DV = 256
MLSTM_CHUNK = 256
MLSTM_ROWS = 512

ATT_HEADS = 8
ATT_KV_HEADS = 2
HEAD_DIM = 128
MOBA_BLOCK = 256
MOBA_TOPK = 3
ROW_GROUPS = 4
ROPE_THETA = 500000.0
ROT_DIM = HEAD_DIM // 4

N_EXPERTS = 8

LANES = 128
NEG_BIG = -1e30
LOG2E = 1.4426950408889634
VMEM_LIMIT = 56 * 1024 * 1024

F32 = jnp.float32
BF16 = jnp.bfloat16

_NT = (((1,), (1,)), ((), ()))


def _dot(a, b):
    return jnp.dot(a, b, preferred_element_type=F32)


def _dot_nt(a, b, precision=None):
    return lax.dot_general(a, b, _NT, precision=precision, preferred_element_type=F32)


def _split_bf16(a):
    hi = a.astype(BF16)
    return hi, (a - hi.astype(F32)).astype(BF16)


def _dot_nt_split(w, x_hi, x_lo):
    r = w.shape[0]
    w_hi, w_lo = _split_bf16(w)
    a = _dot_nt(jnp.concatenate([w_hi, w_lo], axis=0), x_hi)
    return a[:r] + a[r:] + _dot_nt(w_hi, x_lo)


def _rms_scale(x):
    return lax.rsqrt(jnp.mean(x * x, axis=-1, keepdims=True) + NORM_EPS)


def _sigmoid(x):
    return 1.0 / (1.0 + jnp.exp(-x))


def _params(*sem):
    return pltpu.CompilerParams(dimension_semantics=sem, vmem_limit_bytes=VMEM_LIMIT)


def _const_spec(shape):
    nd = len(shape)
    return pl.BlockSpec(shape, lambda *_: (0,) * nd, pipeline_mode=pl.Buffered(1))


def _rider_slabs(riders, steps):
    slabs = []
    for w in riders:
        rows = w.size // (steps * w.shape[-1])
        assert rows * steps * w.shape[-1] == w.size and rows % 16 == 0 and w.shape[-2] % rows == 0
        slabs.append(w.reshape(steps, rows, w.shape[-1]))
    specs = [pl.BlockSpec((1,) + s.shape[1:], lambda i: (i, 0, 0)) for s in slabs]
    shapes = tuple(jax.ShapeDtypeStruct(s.shape, BF16) for s in slabs)
    return slabs, specs, shapes


def _cast_riders(rider_in, rider_out):
    for src, dst in zip(rider_in, rider_out):
        dst[...] = src[...].astype(BF16)


def _mlstm_proj_kernel(x_ref, g_ref, w_ref, wg_ref, b_ref, *rest, col_chunk, n_riders):
    rider_in, (proj_ref, gates_ref), rider_out = rest[:n_riders], rest[n_riders:n_riders + 2], rest[n_riders + 2:]
    x = x_ref[...]
    xn = x * _rms_scale(x) * g_ref[...]
    xb, x_lo = _split_bf16(xn)
    for c in range(w_ref.shape[1] // col_chunk):
        sl = slice(c * col_chunk, (c + 1) * col_chunk)
        proj_ref[:, sl] = _dot(xb, w_ref[:, sl]).astype(BF16)
    gates_ref[...] = _dot_nt_split(wg_ref[...], xb, x_lo) + b_ref[...]
    _cast_riders(rider_in, rider_out)


def mlstm_proj(x, gain, w_main, w_gate_t, bias, riders=(), *, tm=1024, col_chunk=512):
    n, d = x.shape
    p = w_main.shape[1]
    steps = n // tm
    slabs, slab_specs, slab_shapes = _rider_slabs(riders, steps)
    outs = pl.pallas_call(
        partial(_mlstm_proj_kernel, col_chunk=col_chunk, n_riders=len(slabs)),
        out_shape=(jax.ShapeDtypeStruct((n, p), BF16), jax.ShapeDtypeStruct((8, n), F32)) + slab_shapes,
        grid=(steps,),
        in_specs=[pl.BlockSpec((tm, d), lambda i: (i, 0)),
                  _const_spec((1, d)), _const_spec((d, p)), _const_spec((8, d)), _const_spec((8, 1))] + slab_specs,
        out_specs=(pl.BlockSpec((tm, p), lambda i: (i, 0)), pl.BlockSpec((8, tm), lambda i: (0, i)))
        + tuple(slab_specs),
        compiler_params=_params("parallel"),
        name="mlstm_proj",
    )(x, gain, w_main, w_gate_t, bias, *slabs)
    return outs[:2] + tuple(o.reshape(w.shape) for o, w in zip(outs[2:], riders))


def _mlstm_core_kernel(q_ref, k_ref, v_ref, o_ref, gates_ref, gain_ref, out_ref, state_ref, m_ref):
    L = MLSTM_CHUNK
    H, DK, DV = MLSTM_HEADS, MLSTM_DQK, MLSTM_DV
    scale = DK ** -0.5

    @pl.when(pl.program_id(1) == 0)
    def _():
        state_ref[...] = jnp.zeros_like(state_ref)
        m_ref[...] = jnp.zeros_like(m_ref)

    row = lax.broadcasted_iota(jnp.int32, (L, L), 0)
    col = lax.broadcasted_iota(jnp.int32, (L, L), 1)
    causal = col <= row
    upper_incl = (row <= col).astype(F32)
    ones_blk = jnp.ones((L, LANES), BF16)

    hs = range(H)
    cs = range(q_ref.shape[0] // L)
    units = [(h, c) for c in cs for h in hs]
    rs = lambda c: slice(c * L, (c + 1) * L)
    qs = {(h, c): q_ref[rs(c), h * DK:(h + 1) * DK] for h, c in units}
    ks = {(h, c): k_ref[rs(c), h * DK:(h + 1) * DK] for h, c in units}
    v_augs = {(h, c): jnp.concatenate([v_ref[rs(c), h * DV:(h + 1) * DV], ones_blk], axis=1)
              for h, c in units}
    scores = {u: _dot_nt(qs[u], ks[u]) for u in units}

    log2_scale = math.log2(scale)
    gate = {}
    m_prev = {h: m_ref[h, 0:1, 0:1] for h in hs}
    for h, c in units:
        ig = gates_ref[h:h + 1, rs(c)] * LOG2E
        fg = gates_ref[H + h:H + h + 1, rs(c)]
        logf = (jnp.minimum(fg, 0.0) - jnp.log1p(jnp.exp(-jnp.abs(fg)))) * LOG2E
        bcum = jnp.dot(jnp.broadcast_to(logf, (8, L)), upper_incl,
                       precision=lax.Precision.HIGHEST, preferred_element_type=F32)[0:1, :]
        bcum_s = jnp.broadcast_to(bcum, (L, L))
        bcum_t = bcum_s.T
        dmat = jnp.where(causal, bcum_t - bcum_s + ig, -jnp.inf)
        inter = bcum_t[:, 0:1] + m_prev[h]
        m_t = jnp.maximum(inter, jnp.max(dmat, axis=-1, keepdims=True))
        m_s = m_t - log2_scale
        g_tot = bcum[:, L - 1:L]
        a_end = g_tot - bcum + ig
        m_new = jnp.maximum(g_tot + m_prev[h], jnp.max(a_end, axis=-1, keepdims=True))
        gate[h, c] = dict(m_t=m_t, inter_w=jnp.exp2(inter - m_s), e=jnp.exp2(dmat - m_s),
                          w_row=jnp.exp2(a_end - m_new), decay=jnp.exp2(g_tot + m_prev[h] - m_new))
        m_prev[h] = m_new

    ps = {u: (scores[u] * gate[u]["e"]).astype(BF16) for u in units}
    ktws = {u: (ks[u].astype(F32).T * gate[u]["w_row"]).astype(BF16) for u in units}
    states = {h: state_ref[h] for h in hs}
    nums = {}
    for c in cs:
        for h in hs:
            nums[h, c] = (_dot(ps[h, c], v_augs[h, c])
                          + gate[h, c]["inter_w"] * _dot(qs[h, c], states[h].astype(BF16)))
        for h in hs:
            states[h] = gate[h, c]["decay"] * states[h] + _dot(ktws[h, c], v_augs[h, c])
    for h in hs:
        state_ref[h] = states[h]
        m_ref[h] = jnp.broadcast_to(m_prev[h], m_ref.shape[1:])
    for h, c in units:
        nq = nums[h, c][:, DV:DV + 1]
        den = jnp.maximum(jnp.abs(nq), jnp.exp2(-gate[h, c]["m_t"]))
        hval = nums[h, c][:, :DV] / den
        hn = hval * _rms_scale(hval) * gain_ref[:, h * DV:(h + 1) * DV]
        og = o_ref[rs(c), h * DV:(h + 1) * DV].astype(F32)
        out_ref[rs(c), h * DV:(h + 1) * DV] = (hn * _sigmoid(og)).astype(BF16)


def mlstm_core(proj, gates, out_gain, *, batch, seq):
    n = proj.shape[0]
    L = MLSTM_ROWS
    assert L % MLSTM_CHUNK == 0
    nc = seq // L
    H, DK, DV = MLSTM_HEADS, MLSTM_DQK, MLSTM_DV
    qw, vw = H * DK, H * DV
    assert 2 * qw == vw
    rows = lambda b, c: b * nc + c
    return pl.pallas_call(
        _mlstm_core_kernel,
        out_shape=jax.ShapeDtypeStruct((n, vw), BF16),
        grid=(batch, nc),
        in_specs=[pl.BlockSpec((L, qw), lambda b, c: (rows(b, c), 0)),
                  pl.BlockSpec((L, qw), lambda b, c: (rows(b, c), 1)),
                  pl.BlockSpec((L, vw), lambda b, c: (rows(b, c), 1)),
                  pl.BlockSpec((L, vw), lambda b, c: (rows(b, c), 2)),
                  pl.BlockSpec((8, L), lambda b, c: (0, rows(b, c))),
                  _const_spec((1, vw))],
        out_specs=pl.BlockSpec((L, vw), lambda b, c: (rows(b, c), 0)),
        scratch_shapes=[pltpu.VMEM((H, DK, DV + LANES), F32), pltpu.VMEM((H, 8, LANES), F32)],
        compiler_params=_params("parallel", "arbitrary"),
        name="mlstm_core",
    )(proj, proj, proj, proj, gates, out_gain)


def _silu_mul(g, u):
    return g * _sigmoid(g) * u


def _mix_ffn_kernel(x_ref, h_ref, wo_ref, g_ref, wgu_ref, wd_ref, *rest, f_chunk, n_riders):
    rider_in, out_ref, rider_out = rest[:n_riders], rest[n_riders], rest[n_riders + 1:]
    d_ff = wd_ref.shape[0]
    x1 = x_ref[...] + _dot(h_ref[...], wo_ref[...])
    hn = (x1 * _rms_scale(x1) * g_ref[...]).astype(BF16)
    acc = jnp.zeros_like(x1)
    for c in range(d_ff // f_chunk):
        lo = c * f_chunk
        g = _dot(hn, wgu_ref[:, lo:lo + f_chunk])
        u = _dot(hn, wgu_ref[:, d_ff + lo:d_ff + lo + f_chunk])
        acc = acc + _dot(_silu_mul(g, u).astype(BF16), wd_ref[lo:lo + f_chunk, :])
    out_ref[...] = x1 + acc
    _cast_riders(rider_in, rider_out)


def mix_ffn(x, h, w_out, gain, w_gu, w_down, riders=(), *, tm=512, f_chunk=256):
    n, d = x.shape
    d_ff = w_down.shape[0]
    steps = n // tm
    assert d_ff % f_chunk == 0
    slabs, slab_specs, slab_shapes = _rider_slabs(riders, steps)
    outs = pl.pallas_call(
        partial(_mix_ffn_kernel, f_chunk=f_chunk, n_riders=len(slabs)),
        out_shape=(jax.ShapeDtypeStruct((n, d), F32),) + slab_shapes,
        grid=(steps,),
        in_specs=[pl.BlockSpec((tm, d), lambda i: (i, 0)), pl.BlockSpec((tm, h.shape[1]), lambda i: (i, 0)),
                  _const_spec(w_out.shape), _const_spec((1, d)), _const_spec(w_gu.shape),
                  _const_spec(w_down.shape)] + slab_specs,
        out_specs=(pl.BlockSpec((tm, d), lambda i: (i, 0)),) + tuple(slab_specs),
        compiler_params=_params("parallel"),
        name="mix_ffn",
    )(x, h, w_out, gain, w_gu, w_down, *slabs)
    return (outs[0],) + tuple(o.reshape(w.shape) for o, w in zip(outs[1:], riders))


def _rope(t, cos, sin_lo, sin_hi):
    half = ROT_DIM // 2
    return t * cos + pltpu.roll(t, LANES - half, 1) * sin_lo + pltpu.roll(t, half, 1) * sin_hi


def _kvq_kernel(x_ref, gkv_ref, gq_ref, wkv_ref, wq_ref, cos_ref, slo_ref, shi_ref,
                q_ref, k_ref, v_ref, km_ref, *, blocks_per_seq):
    tm = x_ref.shape[0]
    nsub = tm // MOBA_BLOCK
    G, DH = ATT_KV_HEADS, HEAD_DIM
    x = x_ref[...]
    xs = x * _rms_scale(x)
    cos, slo, shi = cos_ref[...], slo_ref[...], shi_ref[...]

    q = _dot((xs * gq_ref[...]).astype(BF16), wq_ref[...])
    kv = _dot((xs * gkv_ref[...]).astype(BF16), wkv_ref[...])
    for h in range(ATT_HEADS):
        q_ref[:, h * DH:(h + 1) * DH] = _rope(q[:, h * DH:(h + 1) * DH], cos, slo, shi).astype(BF16)
    for g in range(G):
        v_ref[:, 2 * g * DH:(2 * g + 1) * DH] = kv[:, (G + g) * DH:(G + g + 1) * DH].astype(BF16)
        v_ref[:, (2 * g + 1) * DH:(2 * g + 2) * DH] = jnp.ones((tm, DH), BF16)
    first_blk = (pl.program_id(0) * nsub) % blocks_per_seq
    blk = first_blk + lax.broadcasted_iota(jnp.int32, (tm, LANES), 0) // MOBA_BLOCK
    ind = (lax.broadcasted_iota(jnp.int32, (tm, LANES), 1) == blk).astype(BF16)
    for g in range(G):
        kg = _rope(kv[:, g * DH:(g + 1) * DH], cos, slo, shi)
        k_ref[:, 2 * g * DH:(2 * g + 1) * DH] = kg.astype(BF16)
        k_ref[:, (2 * g + 1) * DH:(2 * g + 2) * DH] = ind
        for s in range(nsub):
            km_ref[0, s:s + 1, g * DH:(g + 1) * DH] = jnp.mean(
                kg[s * MOBA_BLOCK:(s + 1) * MOBA_BLOCK, :], axis=0, keepdims=True)


def kvq_proj(x, g_kv, g_q, w_kv, w_q, cos, sin_lo, sin_hi, *, seq, tm=1024):
    n, d = x.shape
    G, DH = ATT_KV_HEADS, HEAD_DIM
    nsub = tm // MOBA_BLOCK
    tiles_per_seq = seq // tm
    tab = pl.BlockSpec((tm, LANES), lambda i: (i % tiles_per_seq, 0))
    return pl.pallas_call(
        partial(_kvq_kernel, blocks_per_seq=seq // MOBA_BLOCK),
        out_shape=(jax.ShapeDtypeStruct((n, ATT_HEADS * DH), BF16),
                   jax.ShapeDtypeStruct((n, 2 * G * DH), BF16),
                   jax.ShapeDtypeStruct((n, 2 * G * DH), BF16),
                   jax.ShapeDtypeStruct((n // tm, nsub, G * DH), F32)),
        grid=(n // tm,),
        in_specs=[pl.BlockSpec((tm, d), lambda i: (i, 0)), _const_spec((1, d)), _const_spec((1, d)),
                  _const_spec(w_kv.shape), _const_spec(w_q.shape), tab, tab, tab],
        out_specs=(pl.BlockSpec((tm, ATT_HEADS * DH), lambda i: (i, 0)),
                   pl.BlockSpec((tm, 2 * G * DH), lambda i: (i, 0)),
                   pl.BlockSpec((tm, 2 * G * DH), lambda i: (i, 0)),
                   pl.BlockSpec((1, nsub, G * DH), lambda i: (i, 0, 0))),
        compiler_params=_params("parallel"),
        name="kvq_proj",
    )(x, g_kv, g_q, w_kv, w_q, cos, sin_lo, sin_hi)


def _moba_kernel(q_ref, k_ref, v_ref, km_ref, out_ref, m_ref, acc_ref):
    QB, DH = MOBA_BLOCK, HEAD_DIM
    G = ATT_KV_HEADS
    hpg = ATT_HEADS // G
    R = hpg * QB
    scale = DH ** -0.5
    j = pl.program_id(1)
    nb = km_ref.shape[0]

    exp_scale = scale * LOG2E
    wide = lambda a: jnp.concatenate([a] * (QB // LANES), axis=1)
    row_groups = [slice(i * R // ROW_GROUPS, (i + 1) * R // ROW_GROUPS) for i in range(ROW_GROUPS)]
    units = [(g, rs, slice(g * R + rs.start, g * R + rs.stop)) for g in range(G) for rs in row_groups]
    kcols = lambda g: slice(2 * g * DH, (2 * g + 2) * DH)
    own = pl.ds(pl.multiple_of(j * QB, QB), QB)

    qs = [jnp.concatenate([q_ref[:, (g * hpg + h) * DH:(g * hpg + h + 1) * DH] for h in range(hpg)], axis=0)
          for g in range(G)]

    own_scores = [_dot_nt(qs[g][rs], k_ref[own, 2 * g * DH:(2 * g + 1) * DH]) for g, rs, _ in units]

    blk = lax.broadcasted_iota(jnp.int32, (nb, R), 0)
    eye = jnp.where(lax.broadcasted_iota(jnp.int32, (nb, LANES), 0)
                    == lax.broadcasted_iota(jnp.int32, (nb, LANES), 1), 1.0, 0.0).astype(BF16)
    q_augs = []
    for g in range(G):
        km_hi, km_lo = _split_bf16(km_ref[:, g * DH:(g + 1) * DH])
        g2 = _dot_nt(jnp.concatenate([km_hi, km_lo], axis=0), qs[g])
        gate = g2[:nb] + g2[nb:]
        beats = jnp.zeros((nb, R), jnp.int32)
        for mb in range(nb):
            gm = gate[mb:mb + 1, :]
            ahead = (gm > gate) | ((gm == gate) & (mb < blk))
            beats = beats + jnp.where(ahead, 1, 0) * (mb < j).astype(jnp.int32)
        selected = jnp.where((blk < j) & (beats < MOBA_TOPK), 1.0, 0.0).astype(BF16)
        sel_rows = lax.dot_general(selected, eye, (((0,), (0,)), ((), ())), preferred_element_type=F32)
        pen = ((1.0 - sel_rows) * NEG_BIG).astype(BF16)
        q_augs.append(jnp.concatenate([qs[g], pen], axis=1))

    m0s, p0s = [], []
    for (g, rs, _), s in zip(units, own_scores):
        rows = rs.stop - rs.start
        t_pos = (lax.broadcasted_iota(jnp.int32, (rows, QB), 0) + rs.start) % QB
        s_pos = lax.broadcasted_iota(jnp.int32, (rows, QB), 1)
        s = jnp.where(s_pos <= t_pos, s, -jnp.inf)
        m0 = jnp.max(s, axis=-1, keepdims=True)
        m0s.append(m0)
        p0s.append(jnp.exp2((s - m0) * exp_scale).astype(BF16))
    for (g, rs, ar), m0, p0 in zip(units, m0s, p0s):
        m_ref[ar, :] = jnp.broadcast_to(m0, (rs.stop - rs.start, LANES))
        acc_ref[ar, :] = _dot(p0, v_ref[own, kcols(g)])

    def attend(first_blk, n_blk):
        keys = pl.ds(pl.multiple_of(first_blk * QB, QB), n_blk * QB)
        ss = [_dot_nt(q_augs[g][rs], k_ref[keys, kcols(g)]) for g, rs, _ in units]
        m_olds = [m_ref[ar, :] for _, _, ar in units]
        m_news = [jnp.maximum(mo, jnp.max(s, axis=-1, keepdims=True)) for mo, s in zip(m_olds, ss)]
        ps = [jnp.exp2((s - jnp.concatenate([mn] * (n_blk * QB // LANES), axis=1)) * exp_scale).astype(BF16)
              for s, mn in zip(ss, m_news)]
        pvs = [_dot(p, v_ref[keys, kcols(g)]) for p, (g, _, _) in zip(ps, units)]
        for (_, _, ar), mo, mn, pv in zip(units, m_olds, m_news, pvs):
            alpha = jnp.exp2((mo - mn) * exp_scale)
            acc_ref[ar, :] = wide(alpha) * acc_ref[ar, :] + pv
            m_ref[ar, :] = mn

    def quad(i, carry):
        attend(4 * i, 4)
        return carry

    n_quads = lax.shift_right_logical(j, 2)
    lax.fori_loop(0, n_quads, quad, 0)

    @pl.when((j % 4) >= 2)
    def _():
        attend(4 * n_quads, 2)

    @pl.when(j % 2 == 1)
    def _():
        attend(j - 1, 1)

    for h in range(ATT_HEADS):
        acc = acc_ref[h * QB:(h + 1) * QB, :]
        out_ref[:, h * DH:(h + 1) * DH] = (acc[:, :DH] / acc[:, DH:]).astype(BF16)


def moba_attn(q, k_aug, v, k_means, *, batch, seq):
    n = q.shape[0]
    G, DH, QB = ATT_KV_HEADS, HEAD_DIM, MOBA_BLOCK
    nb = seq // QB
    assert DH == LANES and nb <= LANES
    rows = ATT_HEADS * QB
    return pl.pallas_call(
        _moba_kernel,
        out_shape=jax.ShapeDtypeStruct((n, ATT_HEADS * DH), BF16),
        grid=(batch, nb),
        in_specs=[pl.BlockSpec((QB, ATT_HEADS * DH), lambda b, j: (b * nb + j, 0)),
                  pl.BlockSpec((seq, 2 * G * DH), lambda b, j: (b, 0)),
                  pl.BlockSpec((seq, 2 * G * DH), lambda b, j: (b, 0)),
                  pl.BlockSpec((nb, G * DH), lambda b, j: (b, 0))],
        out_specs=pl.BlockSpec((QB, ATT_HEADS * DH), lambda b, j: (b * nb + j, 0)),
        scratch_shapes=[pltpu.VMEM((rows, LANES), F32), pltpu.VMEM((rows, 2 * DH), F32)],
        compiler_params=_params("parallel", "arbitrary"),
        name="moba_attn",
    )(q, k_aug, v, k_means)


def _oproj_route_kernel(x_ref, a_ref, wo_ref, g_ref, r_ref, x_out_ref, hn_ref, sel_ref, comb_ref):
    x3 = x_ref[...] + _dot(a_ref[...], wo_ref[...])
    x_out_ref[...] = x3
    hn = x3 * _rms_scale(x3) * g_ref[...]
    hn_hi, hn_lo = _split_bf16(hn)
    hn_ref[...] = hn_hi
    logits = _dot_nt_split(r_ref[...], hn_hi, hn_lo)
    ne = logits.shape[0]
    idx = lax.broadcasted_iota(jnp.int32, logits.shape, 0)
    m1 = jnp.max(logits, axis=0, keepdims=True)
    i1 = jnp.min(jnp.where(logits == m1, idx, ne), axis=0, keepdims=True)
    first = idx == i1
    rest = jnp.where(first, -jnp.inf, logits)
    m2 = jnp.max(rest, axis=0, keepdims=True)
    i2 = jnp.min(jnp.where(rest == m2, idx, ne), axis=0, keepdims=True)
    second = idx == i2
    e = jnp.exp(m2 - m1)
    sel_ref[...] = jnp.where(first | second, 1.0, 0.0)
    comb_ref[...] = jnp.where(first, 1.0 / (1.0 + e), 0.0) + jnp.where(second, e / (1.0 + e), 0.0)


def oproj_route(x, att, w_o, gain, router_t, *, tm=1024):
    n, d = x.shape
    ne = router_t.shape[0]
    row = lambda w: pl.BlockSpec((tm, w), lambda i: (i, 0))
    lane_major = pl.BlockSpec((ne, tm), lambda i: (0, i))
    return pl.pallas_call(
        _oproj_route_kernel,
        out_shape=(jax.ShapeDtypeStruct((n, d), F32), jax.ShapeDtypeStruct((n, d), BF16),
                   jax.ShapeDtypeStruct((ne, n), F32), jax.ShapeDtypeStruct((ne, n), F32)),
        grid=(n // tm,),
        in_specs=[row(d), row(att.shape[1]), _const_spec(w_o.shape), _const_spec((1, d)),
                  _const_spec(router_t.shape)],
        out_specs=(row(d), row(d), lane_major, lane_major),
        compiler_params=_params("parallel"),
        name="oproj_route",
    )(x, att, w_o, gain, router_t)


TOK_BLOCK = 512
SEG_ALIGN = 8
BIG_CHUNK = 32
MOE_TM = 512

def _loc_rows(ne):
    need = 2 * TOK_BLOCK + ne * (SEG_ALIGN - 1)
    return -(-need // LANES) * LANES


def _dispatch_plan(sel_t, n_tiles):
    ne, n = sel_t.shape
    nblk = n // TOK_BLOCK
    i32 = jnp.int32
    cnt = sel_t.reshape(ne, nblk, TOK_BLOCK).sum(-1).astype(i32)
    pad = (cnt + SEG_ALIGN - 1) // SEG_ALIGN * SEG_ALIGN
    loff = jnp.cumsum(pad, axis=0) - pad
    tot = pad.sum(axis=1)
    reg = (tot + MOE_TM - 1) // MOE_TM * MOE_TM
    reg_start = jnp.cumsum(reg) - reg
    goff = reg_start[:, None] + jnp.cumsum(pad, axis=1) - pad
    tiles_end = jnp.cumsum(reg // MOE_TM)
    n_valid = tiles_end[-1]
    t = jnp.arange(n_tiles, dtype=i32)
    t_eff = jnp.minimum(t, n_valid - 1)
    tile_expert = (tiles_end[None, :] <= t_eff[:, None]).sum(axis=1).astype(i32)
    tile_valid = (t < n_valid).astype(i32)
    used = reg.sum()
    tail = jnp.concatenate([reg_start + tot, used[None], (reg - tot) // SEG_ALIGN,
                            ((n_tiles * MOE_TM - used) // SEG_ALIGN)[None]]).astype(i32)
    flat = lambda a: a.reshape(-1).astype(i32)
    return flat(pad), flat(loff), flat(goff), tail, tile_expert, t_eff.astype(i32), tile_valid


def _block_rows(pad_ref, blk, nblk, ne):
    tot = pad_ref[blk]
    for e in range(1, ne):
        tot = tot + pad_ref[e * nblk + blk]
    return tot


def _segment_copies(pad_ref, loff_ref, goff_ref, blk, nblk, ne, start_one):
    per_big = BIG_CHUNK // SEG_ALIGN
    for e in range(ne):
        n_small = lax.shift_right_logical(pad_ref[e * nblk + blk], 3)
        n_big = lax.div(n_small, per_big)
        lo, go = loff_ref[e * nblk + blk], goff_ref[e * nblk + blk]

        def big(i, carry, lo=lo, go=go):
            start_one(pl.multiple_of(lo + i * BIG_CHUNK, SEG_ALIGN), pl.multiple_of(go + i * BIG_CHUNK, SEG_ALIGN),
                      BIG_CHUNK)
            return carry

        def small(i, carry, lo=lo, go=go):
            start_one(pl.multiple_of(lo + i * SEG_ALIGN, SEG_ALIGN), pl.multiple_of(go + i * SEG_ALIGN, SEG_ALIGN),
                      SEG_ALIGN)
            return carry

        lax.fori_loop(0, n_big, big, 0)
        lax.fori_loop(n_big * per_big, n_small, small, 0)


def _local_positions(sel, loff_ref, blk, nblk):
    ne, tb = sel.shape
    sub = lax.broadcasted_iota(jnp.int32, (ne, tb), 0)
    row = lax.broadcasted_iota(jnp.int32, (tb, tb), 0)
    col = lax.broadcasted_iota(jnp.int32, (tb, tb), 1)
    before = (row < col).astype(BF16)
    rank = _dot(sel.astype(BF16), before)
    sub1 = lax.broadcasted_iota(jnp.int32, (ne, 1), 0)
    loff = jnp.zeros((ne, 1), F32)
    for e in range(ne):
        loff = jnp.where(sub1 == e, loff_ref[e * nblk + blk].astype(F32), loff)
    chosen = sel > 0.0
    e_lo = jnp.min(jnp.where(chosen, sub, ne), axis=0, keepdims=True)
    e_hi = jnp.max(jnp.where(chosen, sub, -1), axis=0, keepdims=True)
    return loff + rank, sub == e_lo, sub == e_hi


def _pick(a, mask):
    return jnp.sum(jnp.where(mask, a, 0.0), axis=0, keepdims=True)


def _dispatch_kernel(pad_ref, loff_ref, goff_ref, tail_ref, hn_ref, sel_ref, xs_ref, buf_ref, zero_ref,
                     sem, tail_sem, *, nblk, ne):
    b = pl.program_id(0)
    slot = b % 2
    rows = buf_ref.shape[1]

    def wait_block(blk, s):
        tot = pl.multiple_of(_block_rows(pad_ref, blk, nblk, ne), SEG_ALIGN)
        pltpu.make_async_copy(buf_ref.at[s, pl.ds(0, tot)], xs_ref.at[pl.ds(0, tot)], sem.at[s]).wait()

    @pl.when(b >= 2)
    def _():
        wait_block(b - 2, slot)

    lpos, lo_mask, hi_mask = _local_positions(sel_ref[...], loff_ref, b, nblk)
    p1 = _pick(lpos, lo_mask).astype(jnp.int32)
    p2 = _pick(lpos, hi_mask).astype(jnp.int32)
    r = lax.broadcasted_iota(jnp.int32, (rows, TOK_BLOCK), 0)
    perm = jnp.where((r == p1) | (r == p2), 1.0, 0.0).astype(BF16)
    buf_ref[slot] = _dot(perm, hn_ref[...])

    def start_one(lrow, grow, n):
        pltpu.make_async_copy(buf_ref.at[slot, pl.ds(lrow, n)], xs_ref.at[pl.ds(grow, n)], sem.at[slot]).start()

    _segment_copies(pad_ref, loff_ref, goff_ref, b, nblk, ne, start_one)

    @pl.when(b == nblk - 1)
    def _():
        zero_ref[...] = jnp.zeros_like(zero_ref)
        n_span = ne + 1
        for sp in range(n_span):
            def start_zero(i, carry, sp=sp):
                grow = pl.multiple_of(tail_ref[sp] + i * SEG_ALIGN, SEG_ALIGN)
                pltpu.make_async_copy(zero_ref, xs_ref.at[pl.ds(grow, SEG_ALIGN)], tail_sem).start()
                return carry

            lax.fori_loop(0, tail_ref[n_span + sp], start_zero, 0)

        for sp in range(n_span):
            def wait_zero(i, carry):
                pltpu.make_async_copy(zero_ref, xs_ref.at[pl.ds(0, SEG_ALIGN)], tail_sem).wait()
                return carry

            lax.fori_loop(0, tail_ref[n_span + sp], wait_zero, 0)

        wait_block(b, slot)
        if nblk > 1:
            wait_block(b - 1, 1 - slot)


def moe_dispatch(plan, hn, sel_t, *, n_tiles):
    pad, loff, goff, tail = plan[:4]
    n, d = hn.shape
    ne = sel_t.shape[0]
    nblk = n // TOK_BLOCK
    rows = _loc_rows(ne)
    grid_spec = pltpu.PrefetchScalarGridSpec(
        num_scalar_prefetch=4,
        grid=(nblk,),
        in_specs=[pl.BlockSpec((TOK_BLOCK, d), lambda b, *_: (b, 0)),
                  pl.BlockSpec((ne, TOK_BLOCK), lambda b, *_: (0, b))],
        out_specs=pl.BlockSpec(memory_space=pltpu.HBM),
        scratch_shapes=[pltpu.VMEM((2, rows, d), F32), pltpu.VMEM((SEG_ALIGN, d), F32),
                        pltpu.SemaphoreType.DMA((2,)), pltpu.SemaphoreType.DMA(())],
    )
    return pl.pallas_call(
        partial(_dispatch_kernel, nblk=nblk, ne=ne),
        out_shape=jax.ShapeDtypeStruct((n_tiles * MOE_TM, d), F32),
        grid_spec=grid_spec,
        compiler_params=_params("arbitrary"),
        name="moe_dispatch",
    )(pad, loff, goff, tail, hn, sel_t)


def _expert_ffn_kernel(texp_ref, teff_ref, tvalid_ref, x_ref, wg_ref, wu_ref, wd_ref, out_ref, *, f_chunk):
    i = pl.program_id(0)

    @pl.when(tvalid_ref[i] == 1)
    def _():
        xb = x_ref[...].astype(BF16)
        y = None
        for lo in range(0, wd_ref.shape[1], f_chunk):
            act = _silu_mul(_dot(xb, wg_ref[0, :, lo:lo + f_chunk]), _dot(xb, wu_ref[0, :, lo:lo + f_chunk]))
            part = _dot(act.astype(BF16), wd_ref[0, lo:lo + f_chunk, :])
            y = part if y is None else y + part
        out_ref[...] = y

    @pl.when(tvalid_ref[i] == 0)
    def _():
        out_ref[...] = jnp.zeros_like(out_ref)


def expert_ffn(plan, xs, w_gu, w_down, *, f_chunk=1792):
    tile_expert, tile_eff, tile_valid = plan[4:]
    n_rows, d = xs.shape
    d_exp = w_down.shape[1]
    assert d_exp % f_chunk == 0 and f_chunk % LANES == 0
    n_tiles = n_rows // MOE_TM
    grid_spec = pltpu.PrefetchScalarGridSpec(
        num_scalar_prefetch=3,
        grid=(n_tiles,),
        in_specs=[pl.BlockSpec((MOE_TM, d), lambda i, te, tf, tv: (tf[i], 0)),
                  pl.BlockSpec((1, d, d_exp), lambda i, te, tf, tv: (te[i], 0, 0)),
                  pl.BlockSpec((1, d, d_exp), lambda i, te, tf, tv: (te[i], 0, 1)),
                  pl.BlockSpec((1, d_exp, d), lambda i, te, tf, tv: (te[i], 0, 0), pipeline_mode=pl.Buffered(1))],
        out_specs=pl.BlockSpec((MOE_TM, d), lambda i, te, tf, tv: (i, 0)),
    )
    return pl.pallas_call(
        partial(_expert_ffn_kernel, f_chunk=f_chunk),
        out_shape=jax.ShapeDtypeStruct((n_rows, d), F32),
        grid_spec=grid_spec,
        compiler_params=_params("arbitrary"),
        name="expert_ffn",
    )(tile_expert, tile_eff, tile_valid, xs, w_gu, w_gu, w_down)


def _combine_kernel(pad_ref, loff_ref, goff_ref, ys_ref, x_ref, sel_ref, comb_ref, g_ref, out_ref, buf_ref, sem,
                    *, nblk, ne):
    b = pl.program_id(0)
    slot = b % 2
    rows = buf_ref.shape[1]

    def fetch_block(blk, s):
        def start_one(lrow, grow, n):
            pltpu.make_async_copy(ys_ref.at[pl.ds(grow, n)], buf_ref.at[s, pl.ds(lrow, n)], sem.at[s]).start()

        _segment_copies(pad_ref, loff_ref, goff_ref, blk, nblk, ne, start_one)

    @pl.when(b == 0)
    def _():
        buf_ref[...] = jnp.zeros_like(buf_ref)
        fetch_block(0, 0)

    @pl.when(b + 1 < nblk)
    def _():
        fetch_block(b + 1, 1 - slot)

    tot = pl.multiple_of(_block_rows(pad_ref, b, nblk, ne), SEG_ALIGN)
    pltpu.make_async_copy(ys_ref.at[pl.ds(0, tot)], buf_ref.at[slot, pl.ds(0, tot)], sem.at[slot]).wait()

    lpos, lo_mask, hi_mask = _local_positions(sel_ref[...], loff_ref, b, nblk)
    comb = comb_ref[...]
    p1, p2 = _pick(lpos, lo_mask), _pick(lpos, hi_mask)
    w1, w2 = _pick(comb, lo_mask), _pick(comb, hi_mask)
    r = lax.broadcasted_iota(jnp.int32, (rows, TOK_BLOCK), 0)
    w_rows = jnp.sum(jnp.where(r == p1.astype(jnp.int32), w1, 0.0) + jnp.where(r == p2.astype(jnp.int32), w2, 0.0),
                     axis=1, keepdims=True)
    local = (buf_ref[slot] * w_rows).astype(BF16)
    hi1, hi2 = jnp.floor(p1 * (1.0 / 32.0)), jnp.floor(p2 * (1.0 / 32.0))
    info = jnp.concatenate([hi1, p1 - 32.0 * hi1, hi2, p2 - 32.0 * hi2, jnp.zeros((4, TOK_BLOCK), F32)], axis=0)
    eye = jnp.where(lax.broadcasted_iota(jnp.int32, (TOK_BLOCK, TOK_BLOCK), 0)
                    == lax.broadcasted_iota(jnp.int32, (TOK_BLOCK, TOK_BLOCK), 1), 1.0, 0.0).astype(BF16)
    cols = _dot_nt(eye, info.astype(BF16))
    c1 = (32.0 * cols[:, 0:1] + cols[:, 1:2]).astype(jnp.int32)
    c2 = (32.0 * cols[:, 2:3] + cols[:, 3:4]).astype(jnp.int32)
    c = lax.broadcasted_iota(jnp.int32, (TOK_BLOCK, rows), 1)
    unsort = jnp.where((c == c1) | (c == c2), 1.0, 0.0).astype(BF16)
    y = x_ref[...] + _dot(unsort, local)
    out_ref[...] = y * _rms_scale(y) * g_ref[...]


def moe_combine(plan, ys, x, sel_t, comb_t, g_final):
    pad, loff, goff = plan[:3]
    n, d = x.shape
    ne = sel_t.shape[0]
    nblk = n // TOK_BLOCK
    rows = _loc_rows(ne)
    lane_major = pl.BlockSpec((ne, TOK_BLOCK), lambda b, *_: (0, b))
    grid_spec = pltpu.PrefetchScalarGridSpec(
        num_scalar_prefetch=3,
        grid=(nblk,),
        in_specs=[pl.BlockSpec(memory_space=pltpu.HBM),
                  pl.BlockSpec((TOK_BLOCK, d), lambda b, *_: (b, 0)), lane_major, lane_major,
                  pl.BlockSpec((1, d), lambda b, *_: (0, 0))],
        out_specs=pl.BlockSpec((TOK_BLOCK, d), lambda b, *_: (b, 0)),
        scratch_shapes=[pltpu.VMEM((2, rows, d), F32), pltpu.SemaphoreType.DMA((2,))],
    )
    return pl.pallas_call(
        partial(_combine_kernel, nblk=nblk, ne=ne),
        out_shape=jax.ShapeDtypeStruct((n, d), F32),
        grid_spec=grid_spec,
        compiler_params=_params("arbitrary"),
        name="moe_combine",
    )(pad, loff, goff, ys, x, sel_t, comb_t, g_final)


def moe_ffn(hn, sel_t, comb_t, w_gu, w_down, x, g_final):
    n = hn.shape[0]
    ne = sel_t.shape[0]
    nblk = n // TOK_BLOCK
    n_tiles = -(-(2 * n + nblk * ne * (SEG_ALIGN - 1)) // MOE_TM) + ne
    plan = _dispatch_plan(sel_t, n_tiles)
    xs = moe_dispatch(plan, hn, sel_t, n_tiles=n_tiles)
    ys = expert_ffn(plan, xs, w_gu, w_down)
    return moe_combine(plan, ys, x, sel_t, comb_t, g_final)


def _rope_tables(seq):
    half = ROT_DIM // 2
    pos = jnp.arange(seq, dtype=F32)
    inv_freq = ROPE_THETA ** (-jnp.arange(0, ROT_DIM, 2, dtype=F32) / ROT_DIM)
    ang = pos[:, None] * inv_freq[None, :]
    cos, sin = jnp.cos(ang), jnp.sin(ang)
    pad = jnp.zeros((seq, LANES - ROT_DIM), F32)
    zeros = jnp.zeros((seq, half), F32)
    cos_t = jnp.concatenate([cos, cos, pad + 1.0], axis=1)
    sin_lo = jnp.concatenate([-sin, zeros, pad], axis=1)
    sin_hi = jnp.concatenate([zeros, sin, pad], axis=1)
    return cos_t, sin_lo, sin_hi


def kernel(x, mlstm_norm, mlstm_w_in, mlstm_b_if, mlstm_out_norm, mlstm_w_out, ffn_norm, dense_w_gu,
           dense_w_down, moe_router, moe_w_gu, moe_w_down, kv_norm, kv_w, moba_norm, moba_w_q, moba_w_o,
           final_norm):
    batch, seq, d = x.shape
    n = batch * seq
    H = MLSTM_HEADS
    n_main = H * (2 * MLSTM_DQK + 2 * MLSTM_DV)
    xf = x.reshape(n, d)
    row = lambda g: g.reshape(1, -1).astype(F32)

    w_in = mlstm_w_in[0]
    proj, gates, w_out_b, dense_gu_b, kv_w_b, w_q_b, w_o_b = mlstm_proj(
        xf, row(mlstm_norm[0]), w_in[:, :n_main].astype(BF16), w_in[:, n_main:].T,
        mlstm_b_if[0].reshape(2 * H, 1),
        riders=(mlstm_w_out[0], dense_w_gu[0], kv_w, moba_w_q[0], moba_w_o[0]))
    hg = mlstm_core(proj, gates, row(mlstm_out_norm[0]), batch=batch, seq=seq)
    x2, moe_gu_bf16, moe_down_bf16 = mix_ffn(xf, hg, w_out_b, row(ffn_norm[0]), dense_gu_b,
                                             dense_w_down[0].astype(BF16), riders=(moe_w_gu[0], moe_w_down[0]))

    cos_t, sin_lo, sin_hi = _rope_tables(seq)
    q, k_aug, v, k_means = kvq_proj(x2, row(kv_norm), row(moba_norm[0]), kv_w_b, w_q_b, cos_t, sin_lo, sin_hi,
                                    seq=seq)
    k_means = k_means.reshape(n // MOBA_BLOCK, ATT_KV_HEADS * HEAD_DIM)
    att = moba_attn(q, k_aug, v, k_means, batch=batch, seq=seq)
    x3, hn, sel_t, comb_t = oproj_route(x2, att, w_o_b, row(ffn_norm[1]), moe_router[0].T)
    out = moe_ffn(hn, sel_t, comb_t, moe_gu_bf16, moe_down_bf16, x3, row(final_norm))
    return out.reshape(batch, seq, d)
```

```python
import math
from functools import partial

import jax
import jax.numpy as jnp
from jax import lax
from jax.experimental import pallas as pl
from jax.experimental.pallas import tpu as pltpu

NORM_EPS = 1e-6

MLSTM_HEADS = 4
MLSTM_DQK = 128
MLSTM_DV = 256
MLSTM_CHUNK = 256
MLSTM_ROWS = 1024

ATT_HEADS = 8
ATT_KV_HEADS = 2
HEAD_DIM = 128
MOBA_BLOCK = 256
MOBA_TOPK = 3
ROW_GROUPS = 4
ROPE_THETA = 500000.0
ROT_DIM = HEAD_DIM // 4

N_EXPERTS = 8

LANES = 128
NEG_BIG = -1e30
LOG2E = 1.4426950408889634
VMEM_LIMIT = 56 * 1024 * 1024

F32 = jnp.float32
BF16 = jnp.bfloat16

_NT = (((1,), (1,)), ((), ()))


def _dot(a, b):
    return jnp.dot(a, b, preferred_element_type=F32)


def _dot_nt(a, b, precision=None):
    return lax.dot_general(a, b, _NT, precision=precision, preferred_element_type=F32)


def _split_bf16(a):
    hi = a.astype(BF16)
    return hi, (a - hi.astype(F32)).astype(BF16)


def _dot_nt_split(w, x_hi, x_lo):
    r = w.shape[0]
    w_hi, w_lo = _split_bf16(w)
    a = _dot_nt(jnp.concatenate([w_hi, w_lo], axis=0), x_hi)
    return a[:r] + a[r:] + _dot_nt(w_hi, x_lo)


def _rms_scale(x):
    return lax.rsqrt(jnp.mean(x * x, axis=-1, keepdims=True) + NORM_EPS)


def _sigmoid(x):
    return 1.0 / (1.0 + jnp.exp(-x))


def _params(*sem):
    return pltpu.CompilerParams(dimension_semantics=sem, vmem_limit_bytes=VMEM_LIMIT)


def _const_spec(shape):
    nd = len(shape)
    return pl.BlockSpec(shape, lambda *_: (0,) * nd, pipeline_mode=pl.Buffered(1))


def _rider_slabs(riders, steps):
    slabs = []
    for w in riders:
        rows = w.size // (steps * w.shape[-1])
        assert rows * steps * w.shape[-1] == w.size and rows % 16 == 0 and w.shape[-2] % rows == 0
        slabs.append(w.reshape(steps, rows, w.shape[-1]))
    specs = [pl.BlockSpec((1,) + s.shape[1:], lambda i: (i, 0, 0)) for s in slabs]
    shapes = tuple(jax.ShapeDtypeStruct(s.shape, BF16) for s in slabs)
    return slabs, specs, shapes


def _cast_riders(rider_in, rider_out):
    for src, dst in zip(rider_in, rider_out):
        dst[...] = src[...].astype(BF16)


def _mlstm_proj_kernel(x_ref, g_ref, w_ref, wg_ref, b_ref, *rest, col_chunk, n_riders):
    rider_in, (proj_ref, gates_ref), rider_out = rest[:n_riders], rest[n_riders:n_riders + 2], rest[n_riders + 2:]
    x = x_ref[...]
    xn = x * _rms_scale(x) * g_ref[...]
    xb, x_lo = _split_bf16(xn)
    for c in range(w_ref.shape[1] // col_chunk):
        sl = slice(c * col_chunk, (c + 1) * col_chunk)
        proj_ref[:, sl] = _dot(xb, w_ref[:, sl]).astype(BF16)
    gates_ref[...] = _dot_nt_split(wg_ref[...], xb, x_lo) + b_ref[...]
    _cast_riders(rider_in, rider_out)


def mlstm_proj(x, gain, w_main, w_gate_t, bias, riders=(), *, tm=1024, col_chunk=512):
    n, d = x.shape
    p = w_main.shape[1]
    steps = n // tm
    slabs, slab_specs, slab_shapes = _rider_slabs(riders, steps)
    outs = pl.pallas_call(
        partial(_mlstm_proj_kernel, col_chunk=col_chunk, n_riders=len(slabs)),
        out_shape=(jax.ShapeDtypeStruct((n, p), BF16), jax.ShapeDtypeStruct((8, n), F32)) + slab_shapes,
        grid=(steps,),
        in_specs=[pl.BlockSpec((tm, d), lambda i: (i, 0)),
                  _const_spec((1, d)), _const_spec((d, p)), _const_spec((8, d)), _const_spec((8, 1))] + slab_specs,
        out_specs=(pl.BlockSpec((tm, p), lambda i: (i, 0)), pl.BlockSpec((8, tm), lambda i: (0, i)))
        + tuple(slab_specs),
        compiler_params=_params("parallel"),
        name="mlstm_proj",
    )(x, gain, w_main, w_gate_t, bias, *slabs)
    return outs[:2] + tuple(o.reshape(w.shape) for o, w in zip(outs[2:], riders))


def _mlstm_core_kernel(q_ref, k_ref, v_ref, o_ref, gates_ref, gain_ref, out_ref, state_ref, m_ref):
    L = MLSTM_CHUNK
    H, DK, DV = MLSTM_HEADS, MLSTM_DQK, MLSTM_DV
    scale = DK ** -0.5

    @pl.when(pl.program_id(1) == 0)
    def _():
        state_ref[...] = jnp.zeros_like(state_ref)
        m_ref[...] = jnp.zeros_like(m_ref)

    row = lax.broadcasted_iota(jnp.int32, (L, L), 0)
    col = lax.broadcasted_iota(jnp.int32, (L, L), 1)
    causal = col <= row
    upper_incl = (row <= col).astype(F32)
    ones_blk = jnp.ones((L, LANES), BF16)

    hs = range(H)
    cs = range(q_ref.shape[0] // L)
    units = [(h, c) for c in cs for h in hs]
    rs = lambda c: slice(c * L, (c + 1) * L)
    qs = {(h, c): q_ref[rs(c), h * DK:(h + 1) * DK] for h, c in units}
    ks = {(h, c): k_ref[rs(c), h * DK:(h + 1) * DK] for h, c in units}
    v_augs = {(h, c): jnp.concatenate([v_ref[rs(c), h * DV:(h + 1) * DV], ones_blk], axis=1)
              for h, c in units}
    scores = {u: _dot_nt(qs[u], ks[u]) for u in units}

    log2_scale = math.log2(scale)
    gate = {}
    m_prev = {h: m_ref[h, 0:1, 0:1] for h in hs}
    for h, c in units:
        ig = gates_ref[h:h + 1, rs(c)] * LOG2E
        fg = gates_ref[H + h:H + h + 1, rs(c)]
        logf = (jnp.minimum(fg, 0.0) - jnp.log1p(jnp.exp(-jnp.abs(fg)))) * LOG2E
        bcum = jnp.dot(jnp.broadcast_to(logf, (8, L)), upper_incl,
                       precision=lax.Precision.HIGHEST, preferred_element_type=F32)[0:1, :]
        bcum_s = jnp.broadcast_to(bcum, (L, L))
        bcum_t = bcum_s.T
        dmat = jnp.where(causal, bcum_t - bcum_s + ig, -jnp.inf)
        inter = bcum_t[:, 0:1] + m_prev[h]
        m_t = jnp.maximum(inter, jnp.max(dmat, axis=-1, keepdims=True))
        m_s = m_t - log2_scale
        g_tot = bcum[:, L - 1:L]
        a_end = g_tot - bcum + ig
        m_new = jnp.maximum(g_tot + m_prev[h], jnp.max(a_end, axis=-1, keepdims=True))
        gate[h, c] = dict(m_t=m_t, inter_w=jnp.exp2(inter - m_s), e=jnp.exp2(dmat - m_s),
                          w_row=jnp.exp2(a_end - m_new), decay=jnp.exp2(g_tot + m_prev[h] - m_new))
        m_prev[h] = m_new

    ps = {u: (scores[u] * gate[u]["e"]).astype(BF16) for u in units}
    ktws = {u: (ks[u].astype(F32).T * gate[u]["w_row"]).astype(BF16) for u in units}
    states = {h: state_ref[h] for h in hs}
    nums = {}
    for c in cs:
        for h in hs:
            nums[h, c] = (_dot(ps[h, c], v_augs[h, c])
                          + gate[h, c]["inter_w"] * _dot(qs[h, c], states[h].astype(BF16)))
        for h in hs:
            states[h] = gate[h, c]["decay"] * states[h] + _dot(ktws[h, c], v_augs[h, c])
    for h in hs:
        state_ref[h] = states[h]
        m_ref[h] = jnp.broadcast_to(m_prev[h], m_ref.shape[1:])
    for h, c in units:
        nq = nums[h, c][:, DV:DV + 1]
        den = jnp.maximum(jnp.abs(nq), jnp.exp2(-gate[h, c]["m_t"]))
        hval = nums[h, c][:, :DV] / den
        hn = hval * _rms_scale(hval) * gain_ref[:, h * DV:(h + 1) * DV]
        og = o_ref[rs(c), h * DV:(h + 1) * DV].astype(F32)
        out_ref[rs(c), h * DV:(h + 1) * DV] = (hn * _sigmoid(og)).astype(BF16)


def mlstm_core(proj, gates, out_gain, *, batch, seq):
    n = proj.shape[0]
    L = MLSTM_ROWS
    assert L % MLSTM_CHUNK == 0
    nc = seq // L
    H, DK, DV = MLSTM_HEADS, MLSTM_DQK, MLSTM_DV
    qw, vw = H * DK, H * DV
    assert 2 * qw == vw
    rows = lambda b, c: b * nc + c
    return pl.pallas_call(
        _mlstm_core_kernel,
        out_shape=jax.ShapeDtypeStruct((n, vw), BF16),
        grid=(batch, nc),
        in_specs=[pl.BlockSpec((L, qw), lambda b, c: (rows(b, c), 0)),
                  pl.BlockSpec((L, qw), lambda b, c: (rows(b, c), 1)),
                  pl.BlockSpec((L, vw), lambda b, c: (rows(b, c), 1)),
                  pl.BlockSpec((L, vw), lambda b, c: (rows(b, c), 2)),
                  pl.BlockSpec((8, L), lambda b, c: (0, rows(b, c))),
                  _const_spec((1, vw))],
        out_specs=pl.BlockSpec((L, vw), lambda b, c: (rows(b, c), 0)),
        scratch_shapes=[pltpu.VMEM((H, DK, DV + LANES), F32), pltpu.VMEM((H, 8, LANES), F32)],
        compiler_params=_params("parallel", "arbitrary"),
        name="mlstm_core",
    )(proj, proj, proj, proj, gates, out_gain)


def _silu_mul(g, u):
    return g * _sigmoid(g) * u


def _mix_ffn_kernel(x_ref, h_ref, wo_ref, g_ref, wgu_ref, wd_ref, *rest, f_chunk, n_riders):
    rider_in, out_ref, rider_out = rest[:n_riders], rest[n_riders], rest[n_riders + 1:]
    d_ff = wd_ref.shape[0]
    x1 = x_ref[...] + _dot(h_ref[...], wo_ref[...])
    hn = (x1 * _rms_scale(x1) * g_ref[...]).astype(BF16)
    acc = jnp.zeros_like(x1)
    for c in range(d_ff // f_chunk):
        lo = c * f_chunk
        g = _dot(hn, wgu_ref[:, lo:lo + f_chunk])
        u = _dot(hn, wgu_ref[:, d_ff + lo:d_ff + lo + f_chunk])
        acc = acc + _dot(_silu_mul(g, u).astype(BF16), wd_ref[lo:lo + f_chunk, :])
    out_ref[...] = x1 + acc
    _cast_riders(rider_in, rider_out)


def mix_ffn(x, h, w_out, gain, w_gu, w_down, riders=(), *, tm=512, f_chunk=256):
    n, d = x.shape
    d_ff = w_down.shape[0]
    steps = n // tm
    assert d_ff % f_chunk == 0
    slabs, slab_specs, slab_shapes = _rider_slabs(riders, steps)
    outs = pl.pallas_call(
        partial(_mix_ffn_kernel, f_chunk=f_chunk, n_riders=len(slabs)),
        out_shape=(jax.ShapeDtypeStruct((n, d), F32),) + slab_shapes,
        grid=(steps,),
        in_specs=[pl.BlockSpec((tm, d), lambda i: (i, 0)), pl.BlockSpec((tm, h.shape[1]), lambda i: (i, 0)),
                  _const_spec(w_out.shape), _const_spec((1, d)), _const_spec(w_gu.shape),
                  _const_spec(w_down.shape)] + slab_specs,
        out_specs=(pl.BlockSpec((tm, d), lambda i: (i, 0)),) + tuple(slab_specs),
        compiler_params=_params("parallel"),
        name="mix_ffn",
    )(x, h, w_out, gain, w_gu, w_down, *slabs)
    return (outs[0],) + tuple(o.reshape(w.shape) for o, w in zip(outs[1:], riders))


def _rope(t, cos, sin_lo, sin_hi):
    half = ROT_DIM // 2
    return t * cos + pltpu.roll(t, LANES - half, 1) * sin_lo + pltpu.roll(t, half, 1) * sin_hi


def _kvq_kernel(x_ref, gkv_ref, gq_ref, wkv_ref, wq_ref, cos_ref, slo_ref, shi_ref,
                q_ref, k_ref, v_ref, km_ref, *, blocks_per_seq):
    tm = x_ref.shape[0]
    nsub = tm // MOBA_BLOCK
    G, DH = ATT_KV_HEADS, HEAD_DIM
    x = x_ref[...]
    xs = x * _rms_scale(x)
    cos, slo, shi = cos_ref[...], slo_ref[...], shi_ref[...]

    q = _dot((xs * gq_ref[...]).astype(BF16), wq_ref[...])
    kv = _dot((xs * gkv_ref[...]).astype(BF16), wkv_ref[...])
    for h in range(ATT_HEADS):
        q_ref[:, h * DH:(h + 1) * DH] = _rope(q[:, h * DH:(h + 1) * DH], cos, slo, shi).astype(BF16)
    for g in range(G):
        v_ref[:, 2 * g * DH:(2 * g + 1) * DH] = kv[:, (G + g) * DH:(G + g + 1) * DH].astype(BF16)
        v_ref[:, (2 * g + 1) * DH:(2 * g + 2) * DH] = jnp.ones((tm, DH), BF16)
    first_blk = (pl.program_id(0) * nsub) % blocks_per_seq
    blk = first_blk + lax.broadcasted_iota(jnp.int32, (tm, LANES), 0) // MOBA_BLOCK
    ind = (lax.broadcasted_iota(jnp.int32, (tm, LANES), 1) == blk).astype(BF16)
    for g in range(G):
        kg = _rope(kv[:, g * DH:(g + 1) * DH], cos, slo, shi)
        k_ref[:, 2 * g * DH:(2 * g + 1) * DH] = kg.astype(BF16)
        k_ref[:, (2 * g + 1) * DH:(2 * g + 2) * DH] = ind
        for s in range(nsub):
            km_ref[0, s:s + 1, g * DH:(g + 1) * DH] = jnp.mean(
                kg[s * MOBA_BLOCK:(s + 1) * MOBA_BLOCK, :], axis=0, keepdims=True)


def kvq_proj(x, g_kv, g_q, w_kv, w_q, cos, sin_lo, sin_hi, *, seq, tm=1024):
    n, d = x.shape
    G, DH = ATT_KV_HEADS, HEAD_DIM
    nsub = tm // MOBA_BLOCK
    tiles_per_seq = seq // tm
    tab = pl.BlockSpec((tm, LANES), lambda i: (i % tiles_per_seq, 0))
    return pl.pallas_call(
        partial(_kvq_kernel, blocks_per_seq=seq // MOBA_BLOCK),
        out_shape=(jax.ShapeDtypeStruct((n, ATT_HEADS * DH), BF16),
                   jax.ShapeDtypeStruct((n, 2 * G * DH), BF16),
                   jax.ShapeDtypeStruct((n, 2 * G * DH), BF16),
                   jax.ShapeDtypeStruct((n // tm, nsub, G * DH), F32)),
        grid=(n // tm,),
        in_specs=[pl.BlockSpec((tm, d), lambda i: (i, 0)), _const_spec((1, d)), _const_spec((1, d)),
                  _const_spec(w_kv.shape), _const_spec(w_q.shape), tab, tab, tab],
        out_specs=(pl.BlockSpec((tm, ATT_HEADS * DH), lambda i: (i, 0)),
                   pl.BlockSpec((tm, 2 * G * DH), lambda i: (i, 0)),
                   pl.BlockSpec((tm, 2 * G * DH), lambda i: (i, 0)),
                   pl.BlockSpec((1, nsub, G * DH), lambda i: (i, 0, 0))),
        compiler_params=_params("parallel"),
        name="kvq_proj",
    )(x, g_kv, g_q, w_kv, w_q, cos, sin_lo, sin_hi)


def _moba_kernel(q_ref, k_ref, v_ref, km_ref, out_ref, m_ref, acc_ref):
    QB, DH = MOBA_BLOCK, HEAD_DIM
    G = ATT_KV_HEADS
    hpg = ATT_HEADS // G
    R = hpg * QB
    scale = DH ** -0.5
    j = pl.program_id(1)
    nb = km_ref.shape[0]

    exp_scale = scale * LOG2E
    wide = lambda a: jnp.concatenate([a] * (QB // LANES), axis=1)
    row_groups = [slice(i * R // ROW_GROUPS, (i + 1) * R // ROW_GROUPS) for i in range(ROW_GROUPS)]
    units = [(g, rs, slice(g * R + rs.start, g * R + rs.stop)) for g in range(G) for rs in row_groups]
    kcols = lambda g: slice(2 * g * DH, (2 * g + 2) * DH)
    own = pl.ds(pl.multiple_of(j * QB, QB), QB)

    qs = [jnp.concatenate([q_ref[:, (g * hpg + h) * DH:(g * hpg + h + 1) * DH] for h in range(hpg)], axis=0)
          for g in range(G)]

    own_scores = [_dot_nt(qs[g][rs], k_ref[own, 2 * g * DH:(2 * g + 1) * DH]) for g, rs, _ in units]

    blk = lax.broadcasted_iota(jnp.int32, (nb, R), 0)
    eye = jnp.where(lax.broadcasted_iota(jnp.int32, (nb, LANES), 0)
                    == lax.broadcasted_iota(jnp.int32, (nb, LANES), 1), 1.0, 0.0).astype(BF16)
    q_augs = []
    for g in range(G):
        km_hi, km_lo = _split_bf16(km_ref[:, g * DH:(g + 1) * DH])
        g2 = _dot_nt(jnp.concatenate([km_hi, km_lo], axis=0), qs[g])
        gate = g2[:nb] + g2[nb:]
        beats = jnp.zeros((nb, R), jnp.int32)
        for mb in range(nb):
            gm = gate[mb:mb + 1, :]
            ahead = (gm > gate) | ((gm == gate) & (mb < blk))
            beats = beats + jnp.where(ahead, 1, 0) * (mb < j).astype(jnp.int32)
        selected = jnp.where((blk < j) & (beats < MOBA_TOPK), 1.0, 0.0).astype(BF16)
        sel_rows = lax.dot_general(selected, eye, (((0,), (0,)), ((), ())), preferred_element_type=F32)
        pen = ((1.0 - sel_rows) * NEG_BIG).astype(BF16)
        q_augs.append(jnp.concatenate([qs[g], pen], axis=1))

    m0s, p0s = [], []
    for (g, rs, _), s in zip(units, own_scores):
        rows = rs.stop - rs.start
        t_pos = (lax.broadcasted_iota(jnp.int32, (rows, QB), 0) + rs.start) % QB
        s_pos = lax.broadcasted_iota(jnp.int32, (rows, QB), 1)
        s = jnp.where(s_pos <= t_pos, s, -jnp.inf)
        m0 = jnp.max(s, axis=-1, keepdims=True)
        m0s.append(m0)
        p0s.append(jnp.exp2((s - m0) * exp_scale).astype(BF16))
    for (g, rs, ar), m0, p0 in zip(units, m0s, p0s):
        m_ref[ar, :] = jnp.broadcast_to(m0, (rs.stop - rs.start, LANES))
        acc_ref[ar, :] = _dot(p0, v_ref[own, kcols(g)])

    def attend(first_blk, n_blk):
        keys = pl.ds(pl.multiple_of(first_blk * QB, QB), n_blk * QB)
        ss = [_dot_nt(q_augs[g][rs], k_ref[keys, kcols(g)]) for g, rs, _ in units]
        m_olds = [m_ref[ar, :] for _, _, ar in units]
        m_news = [jnp.maximum(mo, jnp.max(s, axis=-1, keepdims=True)) for mo, s in zip(m_olds, ss)]
        ps = [jnp.exp2((s - jnp.concatenate([mn] * (n_blk * QB // LANES), axis=1)) * exp_scale).astype(BF16)
              for s, mn in zip(ss, m_news)]
        pvs = [_dot(p, v_ref[keys, kcols(g)]) for p, (g, _, _) in zip(ps, units)]
        for (_, _, ar), mo, mn, pv in zip(units, m_olds, m_news, pvs):
            alpha = jnp.exp2((mo - mn) * exp_scale)
            acc_ref[ar, :] = wide(alpha) * acc_ref[ar, :] + pv
            m_ref[ar, :] = mn

    def quad(i, carry):
        attend(4 * i, 4)
        return carry

    n_quads = lax.shift_right_logical(j, 2)
    lax.fori_loop(0, n_quads, quad, 0)

    @pl.when((j % 4) >= 2)
    def _():
        attend(4 * n_quads, 2)

    @pl.when(j % 2 == 1)
    def _():
        attend(j - 1, 1)

    for h in range(ATT_HEADS):
        acc = acc_ref[h * QB:(h + 1) * QB, :]
        out_ref[:, h * DH:(h + 1) * DH] = (acc[:, :DH] / acc[:, DH:]).astype(BF16)


def moba_attn(q, k_aug, v, k_means, *, batch, seq):
    n = q.shape[0]
    G, DH, QB = ATT_KV_HEADS, HEAD_DIM, MOBA_BLOCK
    nb = seq // QB
    assert DH == LANES and nb <= LANES
    rows = ATT_HEADS * QB
    return pl.pallas_call(
        _moba_kernel,
        out_shape=jax.ShapeDtypeStruct((n, ATT_HEADS * DH), BF16),
        grid=(batch, nb),
        in_specs=[pl.BlockSpec((QB, ATT_HEADS * DH), lambda b, j: (b * nb + j, 0)),
                  pl.BlockSpec((seq, 2 * G * DH), lambda b, j: (b, 0)),
                  pl.BlockSpec((seq, 2 * G * DH), lambda b, j: (b, 0)),
                  pl.BlockSpec((nb, G * DH), lambda b, j: (b, 0))],
        out_specs=pl.BlockSpec((QB, ATT_HEADS * DH), lambda b, j: (b * nb + j, 0)),
        scratch_shapes=[pltpu.VMEM((rows, LANES), F32), pltpu.VMEM((rows, 2 * DH), F32)],
        compiler_params=_params("parallel", "arbitrary"),
        name="moba_attn",
    )(q, k_aug, v, k_means)


def _oproj_route_kernel(x_ref, a_ref, wo_ref, g_ref, r_ref, x_out_ref, hn_ref, sel_ref, comb_ref):
    x3 = x_ref[...] + _dot(a_ref[...], wo_ref[...])
    x_out_ref[...] = x3
    hn = x3 * _rms_scale(x3) * g_ref[...]
    hn_hi, hn_lo = _split_bf16(hn)
    hn_ref[...] = hn_hi
    logits = _dot_nt_split(r_ref[...], hn_hi, hn_lo)
    ne = logits.shape[0]
    idx = lax.broadcasted_iota(jnp.int32, logits.shape, 0)
    m1 = jnp.max(logits, axis=0, keepdims=True)
    i1 = jnp.min(jnp.where(logits == m1, idx, ne), axis=0, keepdims=True)
    first = idx == i1
    rest = jnp.where(first, -jnp.inf, logits)
    m2 = jnp.max(rest, axis=0, keepdims=True)
    i2 = jnp.min(jnp.where(rest == m2, idx, ne), axis=0, keepdims=True)
    second = idx == i2
    e = jnp.exp(m2 - m1)
    sel_ref[...] = jnp.where(first | second, 1.0, 0.0)
    comb_ref[...] = jnp.where(first, 1.0 / (1.0 + e), 0.0) + jnp.where(second, e / (1.0 + e), 0.0)


def oproj_route(x, att, w_o, gain, router_t, *, tm=1024):
    n, d = x.shape
    ne = router_t.shape[0]
    row = lambda w: pl.BlockSpec((tm, w), lambda i: (i, 0))
    lane_major = pl.BlockSpec((ne, tm), lambda i: (0, i))
    return pl.pallas_call(
        _oproj_route_kernel,
        out_shape=(jax.ShapeDtypeStruct((n, d), F32), jax.ShapeDtypeStruct((n, d), BF16),
                   jax.ShapeDtypeStruct((ne, n), F32), jax.ShapeDtypeStruct((ne, n), F32)),
        grid=(n // tm,),
        in_specs=[row(d), row(att.shape[1]), _const_spec(w_o.shape), _const_spec((1, d)),
                  _const_spec(router_t.shape)],
        out_specs=(row(d), row(d), lane_major, lane_major),
        compiler_params=_params("parallel"),
        name="oproj_route",
    )(x, att, w_o, gain, router_t)


TOK_BLOCK = 512
SEG_ALIGN = 8
BIG_CHUNK = 32
MOE_TM = 512

def _loc_rows(ne):
    need = 2 * TOK_BLOCK + ne * (SEG_ALIGN - 1)
    return -(-need // LANES) * LANES


def _dispatch_plan(sel_t, n_tiles):
    ne, n = sel_t.shape
    nblk = n // TOK_BLOCK
    i32 = jnp.int32
    cnt = sel_t.reshape(ne, nblk, TOK_BLOCK).sum(-1).astype(i32)
    pad = (cnt + SEG_ALIGN - 1) // SEG_ALIGN * SEG_ALIGN
    loff = jnp.cumsum(pad, axis=0) - pad
    tot = pad.sum(axis=1)
    reg = (tot + MOE_TM - 1) // MOE_TM * MOE_TM
    reg_start = jnp.cumsum(reg) - reg
    goff = reg_start[:, None] + jnp.cumsum(pad, axis=1) - pad
    tiles_end = jnp.cumsum(reg // MOE_TM)
    n_valid = tiles_end[-1]
    t = jnp.arange(n_tiles, dtype=i32)
    t_eff = jnp.minimum(t, n_valid - 1)
    tile_expert = (tiles_end[None, :] <= t_eff[:, None]).sum(axis=1).astype(i32)
    tile_valid = (t < n_valid).astype(i32)
    used = reg.sum()
    tail = jnp.concatenate([reg_start + tot, used[None], (reg - tot) // SEG_ALIGN,
                            ((n_tiles * MOE_TM - used) // SEG_ALIGN)[None]]).astype(i32)
    flat = lambda a: a.reshape(-1).astype(i32)
    return flat(pad), flat(loff), flat(goff), tail, tile_expert, t_eff.astype(i32), tile_valid


def _block_rows(pad_ref, blk, nblk, ne):
    tot = pad_ref[blk]
    for e in range(1, ne):
        tot = tot + pad_ref[e * nblk + blk]
    return tot


def _segment_copies(pad_ref, loff_ref, goff_ref, blk, nblk, ne, start_one):
    per_big = BIG_CHUNK // SEG_ALIGN
    for e in range(ne):
        n_small = lax.shift_right_logical(pad_ref[e * nblk + blk], 3)
        n_big = lax.div(n_small, per_big)
        lo, go = loff_ref[e * nblk + blk], goff_ref[e * nblk + blk]

        def big(i, carry, lo=lo, go=go):
            start_one(pl.multiple_of(lo + i * BIG_CHUNK, SEG_ALIGN), pl.multiple_of(go + i * BIG_CHUNK, SEG_ALIGN),
                      BIG_CHUNK)
            return carry

        def small(i, carry, lo=lo, go=go):
            start_one(pl.multiple_of(lo + i * SEG_ALIGN, SEG_ALIGN), pl.multiple_of(go + i * SEG_ALIGN, SEG_ALIGN),
                      SEG_ALIGN)
            return carry

        lax.fori_loop(0, n_big, big, 0)
        lax.fori_loop(n_big * per_big, n_small, small, 0)


def _local_positions(sel, loff_ref, blk, nblk):
    ne, tb = sel.shape
    sub = lax.broadcasted_iota(jnp.int32, (ne, tb), 0)
    row = lax.broadcasted_iota(jnp.int32, (tb, tb), 0)
    col = lax.broadcasted_iota(jnp.int32, (tb, tb), 1)
    before = (row < col).astype(BF16)
    rank = _dot(sel.astype(BF16), before)
    sub1 = lax.broadcasted_iota(jnp.int32, (ne, 1), 0)
    loff = jnp.zeros((ne, 1), F32)
    for e in range(ne):
        loff = jnp.where(sub1 == e, loff_ref[e * nblk + blk].astype(F32), loff)
    chosen = sel > 0.0
    e_lo = jnp.min(jnp.where(chosen, sub, ne), axis=0, keepdims=True)
    e_hi = jnp.max(jnp.where(chosen, sub, -1), axis=0, keepdims=True)
    return loff + rank, sub == e_lo, sub == e_hi


def _pick(a, mask):
    return jnp.sum(jnp.where(mask, a, 0.0), axis=0, keepdims=True)


def _dispatch_kernel(pad_ref, loff_ref, goff_ref, tail_ref, hn_ref, sel_ref, xs_ref, buf_ref, zero_ref,
                     sem, tail_sem, *, nblk, ne):
    b = pl.program_id(0)
    slot = b % 2
    rows = buf_ref.shape[1]

    def wait_block(blk, s):
        tot = pl.multiple_of(_block_rows(pad_ref, blk, nblk, ne), SEG_ALIGN)
        pltpu.make_async_copy(buf_ref.at[s, pl.ds(0, tot)], xs_ref.at[pl.ds(0, tot)], sem.at[s]).wait()

    @pl.when(b >= 2)
    def _():
        wait_block(b - 2, slot)

    lpos, lo_mask, hi_mask = _local_positions(sel_ref[...], loff_ref, b, nblk)
    p1 = _pick(lpos, lo_mask).astype(jnp.int32)
    p2 = _pick(lpos, hi_mask).astype(jnp.int32)
    r = lax.broadcasted_iota(jnp.int32, (rows, TOK_BLOCK), 0)
    perm = jnp.where((r == p1) | (r == p2), 1.0, 0.0).astype(BF16)
    buf_ref[slot] = _dot(perm, hn_ref[...])

    def start_one(lrow, grow, n):
        pltpu.make_async_copy(buf_ref.at[slot, pl.ds(lrow, n)], xs_ref.at[pl.ds(grow, n)], sem.at[slot]).start()

    _segment_copies(pad_ref, loff_ref, goff_ref, b, nblk, ne, start_one)

    @pl.when(b == nblk - 1)
    def _():
        zero_ref[...] = jnp.zeros_like(zero_ref)
        n_span = ne + 1
        for sp in range(n_span):
            def start_zero(i, carry, sp=sp):
                grow = pl.multiple_of(tail_ref[sp] + i * SEG_ALIGN, SEG_ALIGN)
                pltpu.make_async_copy(zero_ref, xs_ref.at[pl.ds(grow, SEG_ALIGN)], tail_sem).start()
                return carry

            lax.fori_loop(0, tail_ref[n_span + sp], start_zero, 0)

        for sp in range(n_span):
            def wait_zero(i, carry):
                pltpu.make_async_copy(zero_ref, xs_ref.at[pl.ds(0, SEG_ALIGN)], tail_sem).wait()
                return carry

            lax.fori_loop(0, tail_ref[n_span + sp], wait_zero, 0)

        wait_block(b, slot)
        if nblk > 1:
            wait_block(b - 1, 1 - slot)


def moe_dispatch(plan, hn, sel_t, *, n_tiles):
    pad, loff, goff, tail = plan[:4]
    n, d = hn.shape
    ne = sel_t.shape[0]
    nblk = n // TOK_BLOCK
    rows = _loc_rows(ne)
    grid_spec = pltpu.PrefetchScalarGridSpec(
        num_scalar_prefetch=4,
        grid=(nblk,),
        in_specs=[pl.BlockSpec((TOK_BLOCK, d), lambda b, *_: (b, 0)),
                  pl.BlockSpec((ne, TOK_BLOCK), lambda b, *_: (0, b))],
        out_specs=pl.BlockSpec(memory_space=pltpu.HBM),
        scratch_shapes=[pltpu.VMEM((2, rows, d), F32), pltpu.VMEM((SEG_ALIGN, d), F32),
                        pltpu.SemaphoreType.DMA((2,)), pltpu.SemaphoreType.DMA(())],
    )
    return pl.pallas_call(
        partial(_dispatch_kernel, nblk=nblk, ne=ne),
        out_shape=jax.ShapeDtypeStruct((n_tiles * MOE_TM, d), F32),
        grid_spec=grid_spec,
        compiler_params=_params("arbitrary"),
        name="moe_dispatch",
    )(pad, loff, goff, tail, hn, sel_t)


def _expert_ffn_kernel(texp_ref, teff_ref, tvalid_ref, x_ref, wg_ref, wu_ref, wd_ref, out_ref, *, f_chunk):
    i = pl.program_id(0)

    @pl.when(tvalid_ref[i] == 1)
    def _():
        xb = x_ref[...].astype(BF16)
        y = None
        for lo in range(0, wd_ref.shape[1], f_chunk):
            act = _silu_mul(_dot(xb, wg_ref[0, :, lo:lo + f_chunk]), _dot(xb, wu_ref[0, :, lo:lo + f_chunk]))
            part = _dot(act.astype(BF16), wd_ref[0, lo:lo + f_chunk, :])
            y = part if y is None else y + part
        out_ref[...] = y

    @pl.when(tvalid_ref[i] == 0)
    def _():
        out_ref[...] = jnp.zeros_like(out_ref)


def expert_ffn(plan, xs, w_gu, w_down, *, f_chunk=1792):
    tile_expert, tile_eff, tile_valid = plan[4:]
    n_rows, d = xs.shape
    d_exp = w_down.shape[1]
    assert d_exp % f_chunk == 0 and f_chunk % LANES == 0
    n_tiles = n_rows // MOE_TM
    grid_spec = pltpu.PrefetchScalarGridSpec(
        num_scalar_prefetch=3,
        grid=(n_tiles,),
        in_specs=[pl.BlockSpec((MOE_TM, d), lambda i, te, tf, tv: (tf[i], 0)),
                  pl.BlockSpec((1, d, d_exp), lambda i, te, tf, tv: (te[i], 0, 0)),
                  pl.BlockSpec((1, d, d_exp), lambda i, te, tf, tv: (te[i], 0, 1)),
                  pl.BlockSpec((1, d_exp, d), lambda i, te, tf, tv: (te[i], 0, 0), pipeline_mode=pl.Buffered(1))],
        out_specs=pl.BlockSpec((MOE_TM, d), lambda i, te, tf, tv: (i, 0)),
    )
    return pl.pallas_call(
        partial(_expert_ffn_kernel, f_chunk=f_chunk),
        out_shape=jax.ShapeDtypeStruct((n_rows, d), F32),
        grid_spec=grid_spec,
        compiler_params=_params("arbitrary"),
        name="expert_ffn",
    )(tile_expert, tile_eff, tile_valid, xs, w_gu, w_gu, w_down)


def _combine_kernel(pad_ref, loff_ref, goff_ref, ys_ref, x_ref, sel_ref, comb_ref, g_ref, out_ref, buf_ref, sem,
                    *, nblk, ne):
    b = pl.program_id(0)
    slot = b % 2
    rows = buf_ref.shape[1]

    def fetch_block(blk, s):
        def start_one(lrow, grow, n):
            pltpu.make_async_copy(ys_ref.at[pl.ds(grow, n)], buf_ref.at[s, pl.ds(lrow, n)], sem.at[s]).start()

        _segment_copies(pad_ref, loff_ref, goff_ref, blk, nblk, ne, start_one)

    @pl.when(b == 0)
    def _():
        buf_ref[...] = jnp.zeros_like(buf_ref)
        fetch_block(0, 0)

    @pl.when(b + 1 < nblk)
    def _():
        fetch_block(b + 1, 1 - slot)

    tot = pl.multiple_of(_block_rows(pad_ref, b, nblk, ne), SEG_ALIGN)
    pltpu.make_async_copy(ys_ref.at[pl.ds(0, tot)], buf_ref.at[slot, pl.ds(0, tot)], sem.at[slot]).wait()

    lpos, lo_mask, hi_mask = _local_positions(sel_ref[...], loff_ref, b, nblk)
    comb = comb_ref[...]
    p1, p2 = _pick(lpos, lo_mask), _pick(lpos, hi_mask)
    w1, w2 = _pick(comb, lo_mask), _pick(comb, hi_mask)
    r = lax.broadcasted_iota(jnp.int32, (rows, TOK_BLOCK), 0)
    w_rows = jnp.sum(jnp.where(r == p1.astype(jnp.int32), w1, 0.0) + jnp.where(r == p2.astype(jnp.int32), w2, 0.0),
                     axis=1, keepdims=True)
    local = (buf_ref[slot] * w_rows).astype(BF16)
    hi1, hi2 = jnp.floor(p1 * (1.0 / 32.0)), jnp.floor(p2 * (1.0 / 32.0))
    info = jnp.concatenate([hi1, p1 - 32.0 * hi1, hi2, p2 - 32.0 * hi2, jnp.zeros((4, TOK_BLOCK), F32)], axis=0)
    eye = jnp.where(lax.broadcasted_iota(jnp.int32, (TOK_BLOCK, TOK_BLOCK), 0)
                    == lax.broadcasted_iota(jnp.int32, (TOK_BLOCK, TOK_BLOCK), 1), 1.0, 0.0).astype(BF16)
    cols = _dot_nt(eye, info.astype(BF16))
    c1 = (32.0 * cols[:, 0:1] + cols[:, 1:2]).astype(jnp.int32)
    c2 = (32.0 * cols[:, 2:3] + cols[:, 3:4]).astype(jnp.int32)
    c = lax.broadcasted_iota(jnp.int32, (TOK_BLOCK, rows), 1)
    unsort = jnp.where((c == c1) | (c == c2), 1.0, 0.0).astype(BF16)
    y = x_ref[...] + _dot(unsort, local)
    out_ref[...] = y * _rms_scale(y) * g_ref[...]


def moe_combine(plan, ys, x, sel_t, comb_t, g_final):
    pad, loff, goff = plan[:3]
    n, d = x.shape
    ne = sel_t.shape[0]
    nblk = n // TOK_BLOCK
    rows = _loc_rows(ne)
    lane_major = pl.BlockSpec((ne, TOK_BLOCK), lambda b, *_: (0, b))
    grid_spec = pltpu.PrefetchScalarGridSpec(
        num_scalar_prefetch=3,
        grid=(nblk,),
        in_specs=[pl.BlockSpec(memory_space=pltpu.HBM),
                  pl.BlockSpec((TOK_BLOCK, d), lambda b, *_: (b, 0)), lane_major, lane_major,
                  pl.BlockSpec((1, d), lambda b, *_: (0, 0))],
        out_specs=pl.BlockSpec((TOK_BLOCK, d), lambda b, *_: (b, 0)),
        scratch_shapes=[pltpu.VMEM((2, rows, d), F32), pltpu.SemaphoreType.DMA((2,))],
    )
    return pl.pallas_call(
        partial(_combine_kernel, nblk=nblk, ne=ne),
        out_shape=jax.ShapeDtypeStruct((n, d), F32),
        grid_spec=grid_spec,
        compiler_params=_params("arbitrary"),
        name="moe_combine",
    )(pad, loff, goff, ys, x, sel_t, comb_t, g_final)


def moe_ffn(hn, sel_t, comb_t, w_gu, w_down, x, g_final):
    n = hn.shape[0]
    ne = sel_t.shape[0]
    nblk = n // TOK_BLOCK
    n_tiles = -(-(2 * n + nblk * ne * (SEG_ALIGN - 1)) // MOE_TM) + ne
    plan = _dispatch_plan(sel_t, n_tiles)
    xs = moe_dispatch(plan, hn, sel_t, n_tiles=n_tiles)
    ys = expert_ffn(plan, xs, w_gu, w_down)
    return moe_combine(plan, ys, x, sel_t, comb_t, g_final)


def _rope_tables(seq):
    half = ROT_DIM // 2
    pos = jnp.arange(seq, dtype=F32)
    inv_freq = ROPE_THETA ** (-jnp.arange(0, ROT_DIM, 2, dtype=F32) / ROT_DIM)
    ang = pos[:, None] * inv_freq[None, :]
    cos, sin = jnp.cos(ang), jnp.sin(ang)
    pad = jnp.zeros((seq, LANES - ROT_DIM), F32)
    zeros = jnp.zeros((seq, half), F32)
    cos_t = jnp.concatenate([cos, cos, pad + 1.0], axis=1)
    sin_lo = jnp.concatenate([-sin, zeros, pad], axis=1)
    sin_hi = jnp.concatenate([zeros, sin, pad], axis=1)
    return cos_t, sin_lo, sin_hi


def kernel(x, mlstm_norm, mlstm_w_in, mlstm_b_if, mlstm_out_norm, mlstm_w_out, ffn_norm, dense_w_gu,
           dense_w_down, moe_router, moe_w_gu, moe_w_down, kv_norm, kv_w, moba_norm, moba_w_q, moba_w_o,
           final_norm):
    batch, seq, d = x.shape
    n = batch * seq
    H = MLSTM_HEADS
    n_main = H * (2 * MLSTM_DQK + 2 * MLSTM_DV)
    xf = x.reshape(n, d)
    row = lambda g: g.reshape(1, -1).astype(F32)

    w_in = mlstm_w_in[0]
    proj, gates, w_out_b, dense_gu_b, kv_w_b, w_q_b, w_o_b = mlstm_proj(
        xf, row(mlstm_norm[0]), w_in[:, :n_main].astype(BF16), w_in[:, n_main:].T,
        mlstm_b_if[0].reshape(2 * H, 1),
        riders=(mlstm_w_out[0], dense_w_gu[0], kv_w, moba_w_q[0], moba_w_o[0]))
    hg = mlstm_core(proj, gates, row(mlstm_out_norm[0]), batch=batch, seq=seq)
    x2, moe_gu_bf16, moe_down_bf16 = mix_ffn(xf, hg, w_out_b, row(ffn_norm[0]), dense_gu_b,
                                             dense_w_down[0].astype(BF16), riders=(moe_w_gu[0], moe_w_down[0]))

    cos_t, sin_lo, sin_hi = _rope_tables(seq)
    q, k_aug, v, k_means = kvq_proj(x2, row(kv_norm), row(moba_norm[0]), kv_w_b, w_q_b, cos_t, sin_lo, sin_hi,
                                    seq=seq)
    k_means = k_means.reshape(n // MOBA_BLOCK, ATT_KV_HEADS * HEAD_DIM)
    att = moba_attn(q, k_aug, v, k_means, batch=batch, seq=seq)
    x3, hn, sel_t, comb_t = oproj_route(x2, att, w_o_b, row(ffn_norm[1]), moe_router[0].T)
    out = moe_ffn(hn, sel_t, comb_t, moe_gu_bf16, moe_down_bf16, x3, row(final_norm))
    return out.reshape(batch, seq, d)
```
